```python
import math
import jax
import jax.numpy as jnp
from jax import lax
import numpy as np

D_MODEL = 1024
BATCH = 8
SEQ = 2048
DEPTH = 4

GRID_W = 64
CTX_LEN = 256
Q_BLOCK = 128
N_MIXERS = 4
ROPE_THETA = 10000.0
EPS = 1e-6
NEG_INF = -1e30

GQA_HEADS = 8
GQA_KV_HEADS = 2
GQA_HEAD_DIM = 128
MLA_HEADS = 8
MLA_Q_LORA = 384
MLA_KV_LORA = 256
MLA_NOPE = 128
MLA_ROPE = 64
MLA_V = 128
WIN_HEADS = 16
WIN_KV_HEADS = 2
WIN_HEAD_DIM = 64
WINDOW = 128
DIFF_HEADS = 8
DIFF_HEAD_DIM = 64
DIFF_V_DIM = 2 * DIFF_HEAD_DIM
DENSE_FF = 3584
N_EXPERTS = 8
TOP_K = 2
EXPERT_FF = 3584

N_GQA = (DEPTH + 3) // 4
N_MLA = (DEPTH + 2) // 4
N_WIN = (DEPTH + 1) // 4
N_DIFF = DEPTH // 4
N_DENSE = (DEPTH + 1) // 2
N_MOE = DEPTH // 2

kernel_name = "hybrid_interleaved_diffusion_backbone"


def rms_norm(x, g):
    xf = x.astype(jnp.float32)
    y = xf * lax.rsqrt(jnp.mean(xf * xf, axis=-1, keepdims=True) + EPS)
    return (y * g.astype(jnp.float32)).astype(x.dtype)


def modulate(h, shift, scale):
    return h * (1.0 + scale) + shift


def axial_rope_tables(rows, rot_dim):
    quarter = rot_dim // 4
    inv_freq = ROPE_THETA ** (-jnp.arange(quarter, dtype=jnp.float32) / quarter)
    row = jnp.repeat(jnp.arange(rows, dtype=jnp.float32), GRID_W)
    col = jnp.tile(jnp.arange(GRID_W, dtype=jnp.float32), rows)
    ang = jnp.concatenate([row[:, None] * inv_freq, col[:, None] * inv_freq], axis=-1)
    return jnp.cos(ang), jnp.sin(ang)


def apply_rope(x, cos, sin):
    half = x.shape[-1] // 2
    shape = (1, x.shape[1]) + (1,) * (x.ndim - 3) + (half,)
    cs, sn = cos.reshape(shape), sin.reshape(shape)
    xf = x.astype(jnp.float32)
    x1, x2 = xf[..., :half], xf[..., half:]
    return jnp.concatenate([x1 * cs - x2 * sn, x1 * sn + x2 * cs], axis=-1).astype(x.dtype)


def attend(q, k, v, scale, mask=None, sink=None):
    s = jnp.einsum('bqhgd,bkhd->bhgqk', q, k).astype(jnp.float32) * scale
    if mask is not None:
        s = jnp.where(mask, s, NEG_INF)
    if sink is None:
        p = jax.nn.softmax(s, axis=-1)
    else:
        sk = sink.astype(jnp.float32)[None, :, :, None, None]
        m = jnp.maximum(jnp.max(s, axis=-1, keepdims=True), sk)
        e = jnp.exp(s - m)
        p = e / (jnp.sum(e, axis=-1, keepdims=True) + jnp.exp(sk - m))
    return jnp.einsum('bhgqk,bkhd->bqhgd', p.astype(v.dtype), v)


def sweep_query_blocks(fn, *qs):
    B, S = qs[0].shape[:2]
    nb = S // Q_BLOCK
    blocks = tuple(jnp.moveaxis(q.reshape((B, nb, Q_BLOCK) + q.shape[2:]), 1, 0) for q in qs)
    out = lax.map(lambda a: fn(a[0], *a[1:]), (jnp.arange(nb),) + blocks)
    out = jnp.moveaxis(out, 0, 1)
    return out.reshape((B, S) + out.shape[3:])


def gqa_project(h, w_qkv, q_gain, k_gain, n_kv, group, hd):
    B, T, _ = h.shape
    qkv = h @ w_qkv
    nq, nk = n_kv * group * hd, n_kv * hd
    q = qkv[..., :nq].reshape(B, T, n_kv, group, hd)
    k = qkv[..., nq:nq + nk].reshape(B, T, n_kv, hd)
    v = qkv[..., nq + nk:].reshape(B, T, n_kv, hd)
    return rms_norm(q, q_gain), rms_norm(k, k_gain), v


def gqa_axial(h_lat, h_ctx, w_qkv, q_gain, k_gain, w_o, rope, need_ctx):
    B, S, _ = h_lat.shape
    G = GQA_HEADS // GQA_KV_HEADS
    scale = GQA_HEAD_DIM ** -0.5
    q_l, k_l, v_l = gqa_project(h_lat, w_qkv, q_gain, k_gain, GQA_KV_HEADS, G, GQA_HEAD_DIM)
    q_l, k_l = apply_rope(q_l, *rope), apply_rope(k_l, *rope)
    q_c, k_c, v_c = gqa_project(h_ctx, w_qkv, q_gain, k_gain, GQA_KV_HEADS, G, GQA_HEAD_DIM)
    k_all = jnp.concatenate([k_c, k_l], axis=1)
    v_all = jnp.concatenate([v_c, v_l], axis=1)
    o_l = sweep_query_blocks(lambda i, qb: attend(qb, k_all, v_all, scale), q_l)
    o_l = o_l.reshape(B, S, -1) @ w_o
    o_c = attend(q_c, k_c, v_c, scale).reshape(B, h_ctx.shape[1], -1) @ w_o if need_ctx else None
    return o_l, o_c


def mla(h_lat, h_ctx, w_down, qa_gain, kva_gain, w_uq, w_ukv, q_gain, k_gain, w_o, rope, need_ctx):
    def project(h, rope_tab):
        B, T, _ = h.shape
        dn = h @ w_down
        cq = rms_norm(dn[..., :MLA_Q_LORA], qa_gain)
        ckv = rms_norm(dn[..., MLA_Q_LORA:MLA_Q_LORA + MLA_KV_LORA], kva_gain)
        k_pe = rms_norm(dn[..., MLA_Q_LORA + MLA_KV_LORA:], k_gain[MLA_NOPE:])
        q = (cq @ w_uq).reshape(B, T, MLA_HEADS, MLA_NOPE + MLA_ROPE)
        q_nope = rms_norm(q[..., :MLA_NOPE], q_gain[:MLA_NOPE])
        q_pe = rms_norm(q[..., MLA_NOPE:], q_gain[MLA_NOPE:])
        kv = (ckv @ w_ukv).reshape(B, T, MLA_HEADS, MLA_NOPE + MLA_V)
        k_nope = rms_norm(kv[..., :MLA_NOPE], k_gain[:MLA_NOPE])
        v = kv[..., MLA_NOPE:]
        if rope_tab is not None:
            q_pe, k_pe = apply_rope(q_pe, *rope_tab), apply_rope(k_pe, *rope_tab)
        q = jnp.concatenate([q_nope, q_pe], axis=-1)[:, :, :, None, :]
        k = jnp.concatenate([k_nope, jnp.broadcast_to(k_pe[:, :, None, :], (B, T, MLA_HEADS, MLA_ROPE))], axis=-1)
        return q, k, v

    B, S, _ = h_lat.shape
    scale = (MLA_NOPE + MLA_ROPE) ** -0.5
    q_l, k_l, v_l = project(h_lat, rope)
    q_c, k_c, v_c = project(h_ctx, None)
    k_all = jnp.concatenate([k_c, k_l], axis=1)
    v_all = jnp.concatenate([v_c, v_l], axis=1)
    o_l = sweep_query_blocks(lambda i, qb: attend(qb, k_all, v_all, scale), q_l)
    o_l = o_l.reshape(B, S, -1) @ w_o
    o_c = attend(q_c, k_c, v_c, scale).reshape(B, h_ctx.shape[1], -1) @ w_o if need_ctx else None
    return o_l, o_c


def window_gqa(h_lat, h_ctx, w_qkv, q_gain, k_gain, sink, w_o, rope, need_ctx):
    B, S, _ = h_lat.shape
    G = WIN_HEADS // WIN_KV_HEADS
    scale = WIN_HEAD_DIM ** -0.5
    sink = sink.reshape(WIN_KV_HEADS, G)
    q_l, k_l, v_l = gqa_project(h_lat, w_qkv, q_gain, k_gain, WIN_KV_HEADS, G, WIN_HEAD_DIM)
    q_l, k_l = apply_rope(q_l, *rope), apply_rope(k_l, *rope)
    q_c, k_c, v_c = gqa_project(h_ctx, w_qkv, q_gain, k_gain, WIN_KV_HEADS, G, WIN_HEAD_DIM)
    pad = ((0, 0), (Q_BLOCK, Q_BLOCK), (0, 0), (0, 0))
    k_pad, v_pad = jnp.pad(k_l, pad), jnp.pad(v_l, pad)
    q_off = jnp.arange(Q_BLOCK)
    k_off = jnp.arange(3 * Q_BLOCK) - Q_BLOCK
    band = jnp.abs(q_off[:, None] - k_off[None, :]) <= WINDOW
    ctx_mask = jnp.ones((Q_BLOCK, k_c.shape[1]), dtype=bool)

    def block(i, qb):
        kpos = i * Q_BLOCK + k_off
        valid = band & ((kpos >= 0) & (kpos < S))[None, :]
        kb = lax.dynamic_slice_in_dim(k_pad, i * Q_BLOCK, 3 * Q_BLOCK, axis=1)
        vb = lax.dynamic_slice_in_dim(v_pad, i * Q_BLOCK, 3 * Q_BLOCK, axis=1)
        k_all = jnp.concatenate([k_c, kb], axis=1)
        v_all = jnp.concatenate([v_c, vb], axis=1)
        mask = jnp.concatenate([ctx_mask, valid], axis=1)
        return attend(qb, k_all, v_all, scale, mask=mask, sink=sink)

    o_l = sweep_query_blocks(block, q_l).reshape(B, S, -1) @ w_o
    o_c = attend(q_c, k_c, v_c, scale, sink=sink).reshape(B, h_ctx.shape[1], -1) @ w_o if need_ctx else None
    return o_l, o_c


def diff_attention(h_lat, h_ctx, w_qkv, q_gain, k_gain, lam_p, subln, w_o, rope, lam_init, need_ctx):
    nq = DIFF_HEADS * 2 * DIFF_HEAD_DIM

    def project(h):
        B, T, _ = h.shape
        qkv = h @ w_qkv
        q = qkv[..., :nq].reshape(B, T, DIFF_HEADS, 2, DIFF_HEAD_DIM)
        k = qkv[..., nq:2 * nq].reshape(B, T, DIFF_HEADS, 2, DIFF_HEAD_DIM)
        v = qkv[..., 2 * nq:].reshape(B, T, DIFF_HEADS, DIFF_V_DIM)
        return rms_norm(q, q_gain), rms_norm(k, k_gain), v

    lp = lam_p.astype(jnp.float32)
    lam = jnp.exp(jnp.sum(lp[0] * lp[1])) - jnp.exp(jnp.sum(lp[2] * lp[3])) + lam_init
    scale = DIFF_HEAD_DIM ** -0.5

    def diff_attend(q, k, v):
        s = jnp.einsum('bqhmd,bkhmd->bhmqk', q, k).astype(jnp.float32) * scale
        p = jax.nn.softmax(s, axis=-1)
        p = p[:, :, 0] - lam * p[:, :, 1]
        return jnp.einsum('bhqk,bkhd->bqhd', p.astype(v.dtype), v)

    def finish(o):
        B, T = o.shape[:2]
        return (rms_norm(o, subln) * (1.0 - lam_init)).reshape(B, T, -1) @ w_o

    q_l, k_l, v_l = project(h_lat)
    q_l, k_l = apply_rope(q_l, *rope), apply_rope(k_l, *rope)
    q_c, k_c, v_c = project(h_ctx)
    k_all = jnp.concatenate([k_c, k_l], axis=1)
    v_all = jnp.concatenate([v_c, v_l], axis=1)
    o_l = finish(sweep_query_blocks(lambda i, qb: diff_attend(qb, k_all, v_all), q_l))
    o_c = finish(diff_attend(q_c, k_c, v_c)) if need_ctx else None
    return o_l, o_c


def swiglu(h, w13, w2):
    a = h @ w13
    g, u = jnp.split(a, 2, axis=-1)
    return (jax.nn.silu(g) * u) @ w2


def moe_swiglu(h, router, w13, w2):
    logits = (h @ router).astype(jnp.float32)
    top_v, top_i = lax.top_k(logits, TOP_K)
    gates = jax.nn.softmax(top_v, axis=-1)
    combine = jnp.sum(jax.nn.one_hot(top_i, N_EXPERTS, dtype=jnp.float32) * gates[..., None], axis=-2)
    out = jnp.zeros_like(h)
    for e in range(N_EXPERTS):
        out = out + combine[..., e:e + 1].astype(h.dtype) * swiglu(h, w13[e], w2[e])
    return out


def channel_mixer(h, i, ffn_w13, ffn_w2, moe_router, moe_w13, moe_w2):
    j = i // 2
    if i % 2 == 0:
        return swiglu(h, ffn_w13[j], ffn_w2[j])
    return moe_swiglu(h, moe_router[j], moe_w13[j], moe_w2[j])


def setup_inputs(seed: int = 0) -> dict:
    key = jax.random.key(seed)
    ks = iter(jax.random.split(key, 40))
    D = D_MODEL

    def nrm(shape, s):
        return jax.random.normal(next(ks), shape, jnp.float32) * s

    def w(shape):
        return nrm(shape, shape[-2] ** -0.5)

    def gain(shape):
        return 1.0 + nrm(shape, 0.02)

    return {
        "x": nrm((BATCH, SEQ, D), 1.0),
        "c": nrm((BATCH, D), 1.0),
        "ctx": nrm((BATCH, CTX_LEN, D), 1.0),
        "c_ctx": nrm((D,), 1.0),
        "ada_w": nrm((DEPTH, D, 6 * D), 0.5 * D ** -0.5),
        "ada_b": nrm((DEPTH, 6 * D), 0.02),
        "norm_mix": gain((DEPTH, D)),
        "norm_ffn": gain((DEPTH, D)),
        "gqa_wqkv": w((N_GQA, D, (GQA_HEADS + 2 * GQA_KV_HEADS) * GQA_HEAD_DIM)),
        "gqa_q_gain": gain((N_GQA, GQA_HEAD_DIM)),
        "gqa_k_gain": gain((N_GQA, GQA_HEAD_DIM)),
        "gqa_wo": w((N_GQA, GQA_HEADS * GQA_HEAD_DIM, D)),
        "mla_wdown": w((N_MLA, D, MLA_Q_LORA + MLA_KV_LORA + MLA_ROPE)),
        "mla_qa_gain": gain((N_MLA, MLA_Q_LORA)),
        "mla_kva_gain": gain((N_MLA, MLA_KV_LORA)),
        "mla_wuq": w((N_MLA, MLA_Q_LORA, MLA_HEADS * (MLA_NOPE + MLA_ROPE))),
        "mla_wukv": w((N_MLA, MLA_KV_LORA, MLA_HEADS * (MLA_NOPE + MLA_V))),
        "mla_q_gain": gain((N_MLA, MLA_NOPE + MLA_ROPE)),
        "mla_k_gain": gain((N_MLA, MLA_NOPE + MLA_ROPE)),
        "mla_wo": w((N_MLA, MLA_HEADS * MLA_V, D)),
        "win_wqkv": w((N_WIN, D, (WIN_HEADS + 2 * WIN_KV_HEADS) * WIN_HEAD_DIM)),
        "win_q_gain": gain((N_WIN, WIN_HEAD_DIM)),
        "win_k_gain": gain((N_WIN, WIN_HEAD_DIM)),
        "win_sink": nrm((N_WIN, WIN_HEADS), 0.5),
        "win_wo": w((N_WIN, WIN_HEADS * WIN_HEAD_DIM, D)),
        "diff_wqkv": w((N_DIFF, D, 2 * DIFF_HEADS * 2 * DIFF_HEAD_DIM + DIFF_HEADS * DIFF_V_DIM)),
        "diff_q_gain": gain((N_DIFF, DIFF_HEAD_DIM)),
        "diff_k_gain": gain((N_DIFF, DIFF_HEAD_DIM)),
        "diff_lambda": nrm((N_DIFF, 4, DIFF_HEAD_DIM), 0.1),
        "diff_subln": gain((N_DIFF, DIFF_V_DIM)),
        "diff_wo": w((N_DIFF, DIFF_HEADS * DIFF_V_DIM, D)),
        "ffn_w13": w((N_DENSE, D, 2 * DENSE_FF)),
        "ffn_w2": w((N_DENSE, DENSE_FF, D)),
        "moe_router": w((N_MOE, D, N_EXPERTS)),
        "moe_w13": w((N_MOE, N_EXPERTS, D, 2 * EXPERT_FF)),
        "moe_w2": w((N_MOE, N_EXPERTS, EXPERT_FF, D)),
    }


def reference(x, c, ctx, c_ctx, ada_w, ada_b, norm_mix, norm_ffn,
              gqa_wqkv, gqa_q_gain, gqa_k_gain, gqa_wo,
              mla_wdown, mla_qa_gain, mla_kva_gain, mla_wuq, mla_wukv, mla_q_gain, mla_k_gain, mla_wo,
              win_wqkv, win_q_gain, win_k_gain, win_sink, win_wo,
              diff_wqkv, diff_q_gain, diff_k_gain, diff_lambda, diff_subln, diff_wo,
              ffn_w13, ffn_w2, moe_router, moe_w13, moe_w2):
    S = x.shape[1]
    rows = S // GRID_W
    rope_gqa = axial_rope_tables(rows, GQA_HEAD_DIM)
    rope_mla = axial_rope_tables(rows, MLA_ROPE)
    rope_win = axial_rope_tables(rows, WIN_HEAD_DIM)
    rope_diff = axial_rope_tables(rows, DIFF_HEAD_DIM)
    silu_c = jax.nn.silu(c)
    silu_cc = jax.nn.silu(c_ctx)

    for i in range(DEPTH):
        need_ctx = i < DEPTH - 1
        j = i // N_MIXERS
        kind = i % N_MIXERS
        mod_l = (silu_c @ ada_w[i] + ada_b[i])[:, None, :]
        mod_c = silu_cc @ ada_w[i] + ada_b[i]
        sh_a, sc_a, g_a, sh_f, sc_f, g_f = jnp.split(mod_l, 6, axis=-1)
        csh_a, csc_a, cg_a, csh_f, csc_f, cg_f = jnp.split(mod_c, 6, axis=-1)

        h_l = modulate(rms_norm(x, norm_mix[i]), sh_a, sc_a)
        h_c = modulate(rms_norm(ctx, norm_mix[i]), csh_a, csc_a)
        if kind == 0:
            o_l, o_c = gqa_axial(h_l, h_c, gqa_wqkv[j], gqa_q_gain[j], gqa_k_gain[j], gqa_wo[j],
                                 rope_gqa, need_ctx)
        elif kind == 1:
            o_l, o_c = mla(h_l, h_c, mla_wdown[j], mla_qa_gain[j], mla_kva_gain[j], mla_wuq[j],
                           mla_wukv[j], mla_q_gain[j], mla_k_gain[j], mla_wo[j], rope_mla, need_ctx)
        elif kind == 2:
            o_l, o_c = window_gqa(h_l, h_c, win_wqkv[j], win_q_gain[j], win_k_gain[j], win_sink[j],
                                  win_wo[j], rope_win, need_ctx)
        else:
            lam_init = 0.8 - 0.6 * math.exp(-0.3 * i)
            o_l, o_c = diff_attention(h_l, h_c, diff_wqkv[j], diff_q_gain[j], diff_k_gain[j],
                                      diff_lambda[j], diff_subln[j], diff_wo[j], rope_diff,
                                      lam_init, need_ctx)
        x = x + g_a * o_l
        if need_ctx:
            ctx = ctx + cg_a * o_c

        h_l = modulate(rms_norm(x, norm_ffn[i]), sh_f, sc_f)
        x = x + g_f * channel_mixer(h_l, i, ffn_w13, ffn_w2, moe_router, moe_w13, moe_w2)
        if need_ctx:
            h_c = modulate(rms_norm(ctx, norm_ffn[i]), csh_f, csc_f)
            ctx = ctx + cg_f * channel_mixer(h_c, i, ffn_w13, ffn_w2, moe_router, moe_w13, moe_w2)
    return x
```

```python
import functools
import math

import jax
import jax.numpy as jnp
import numpy as np
from jax import lax
from jax.experimental import pallas as pl
from jax.experimental.pallas import tpu as pltpu

GRID_W = 64
ROPE_THETA = 10000.0
EPS = 1e-6
NEG_INF = -1e30
LANES = 128
ROW_TILE = 256
FFN_ROW_TILE = 768
FF_CHUNK = 512
VMEM_LIMIT = 56 * 1024 * 1024

GQA_HEADS, GQA_KV_HEADS, GQA_HEAD_DIM = 8, 2, 128
MLA_HEADS, MLA_Q_LORA, MLA_KV_LORA, MLA_NOPE, MLA_ROPE, MLA_V = 8, 384, 256, 128, 64, 128
WIN_HEADS, WIN_KV_HEADS, WIN_HEAD_DIM, WINDOW = 16, 2, 64, 128
DIFF_HEADS, DIFF_HEAD_DIM = 8, 64
N_EXPERTS = 8

F32 = jnp.float32
BF16 = jnp.bfloat16


def _cparams(*sem):
    return pltpu.CompilerParams(dimension_semantics=sem, vmem_limit_bytes=VMEM_LIMIT)


def _dot(a, b):
    return jnp.dot(a, b, preferred_element_type=F32)


def _dot_nt(a, b):
    return lax.dot_general(a, b, (((1,), (1,)), ((), ())), preferred_element_type=F32)


def _split_bf16(v):
    hi = v.astype(BF16)
    lo = (v - hi.astype(F32)).astype(BF16)
    return hi, lo


def _adaln_kernel(c_ref, w_ref, b_ref, o_ref):
    c = c_ref[...]
    sc = c * (1.0 / (1.0 + jnp.exp(-c)))
    o_ref[0] = jnp.dot(sc, w_ref[0], preferred_element_type=F32,
                       precision=lax.Precision.HIGHEST) + b_ref[0]


def adaln(cond, ada_w, ada_b):
    depth, d, n = ada_w.shape
    rows = cond.shape[0]
    bn = 1536
    return pl.pallas_call(
        _adaln_kernel,
        grid=(depth, n // bn),
        in_specs=[
            pl.BlockSpec((rows, d), lambda l, j: (0, 0)),
            pl.BlockSpec((1, d, bn), lambda l, j: (l, 0, j)),
            pl.BlockSpec((1, 1, bn), lambda l, j: (l, 0, j)),
        ],
        out_specs=pl.BlockSpec((1, rows, bn), lambda l, j: (l, 0, j)),
        out_shape=jax.ShapeDtypeStruct((depth, rows, n), F32),
        compiler_params=_cparams("parallel", "parallel"),
        name="adaln",
    )(cond, ada_w, ada_b.reshape(depth, 1, n))


def _mod_spec(k, n_ctx_tiles, d):
    return pl.BlockSpec((None, None, None, 1, d),
                        lambda b, i, *_: (b, jnp.minimum(i // n_ctx_tiles, 1), k, 0, 0))


def _norm_mod(x, g, shift, scale):
    ms = jnp.mean(x * x, axis=-1, keepdims=True)
    return x * lax.rsqrt(ms + EPS) * g * (1.0 + scale) + shift


def _norm_mod_kernel(x_ref, g_ref, sh_ref, sc_ref, h_ref):
    h_ref[0] = _norm_mod(x_ref[0], g_ref[...], sh_ref[...], sc_ref[...]).astype(BF16)


def _norm_mod_router_kernel(x_ref, g_ref, sh_ref, sc_ref, r_ref, h_ref, cw_ref):
    h = _norm_mod(x_ref[0], g_ref[...], sh_ref[...], sc_ref[...])
    h_ref[0] = h.astype(BF16)
    logits = jnp.dot(h, r_ref[...], preferred_element_type=F32, precision=lax.Precision.HIGHEST)
    lane = lax.broadcasted_iota(jnp.int32, logits.shape, 1)
    logits = jnp.where(lane < N_EXPERTS, logits, -jnp.inf)
    m1 = jnp.max(logits, axis=-1, keepdims=True)
    i1 = jnp.min(jnp.where(logits == m1, lane, LANES), axis=-1, keepdims=True)
    rest = jnp.where(lane == i1, -jnp.inf, logits)
    m2 = jnp.max(rest, axis=-1, keepdims=True)
    i2 = jnp.min(jnp.where(rest == m2, lane, LANES), axis=-1, keepdims=True)
    e2 = jnp.exp(m2 - m1)
    den = 1.0 + e2
    cw_ref[0] = jnp.where(lane == i1, 1.0 / den, 0.0) + jnp.where(lane == i2, e2 / den, 0.0)


def norm_mod(xa, gain, mods, k_shift, k_scale, n_ctx, router=None):
    b, ta, d = xa.shape
    bm = ROW_TILE
    nct = n_ctx // bm
    in_specs = [
        pl.BlockSpec((1, bm, d), lambda b_, i: (b_, i, 0)),
        pl.BlockSpec((1, d), lambda b_, i: (0, 0)),
        _mod_spec(k_shift, nct, d),
        _mod_spec(k_scale, nct, d),
    ]
    h_spec = pl.BlockSpec((1, bm, d), lambda b_, i: (b_, i, 0))
    h_shape = jax.ShapeDtypeStruct((b, ta, d), BF16)
    args = [xa, gain.reshape(1, d), mods, mods]
    if router is None:
        return pl.pallas_call(
            _norm_mod_kernel, grid=(b, ta // bm), in_specs=in_specs, out_specs=h_spec,
            out_shape=h_shape, compiler_params=_cparams("parallel", "parallel"), name="norm_mod",
        )(*args)
    rpad = jnp.zeros((d, LANES), F32).at[:, :N_EXPERTS].set(router)
    return pl.pallas_call(
        _norm_mod_router_kernel, grid=(b, ta // bm),
        in_specs=in_specs + [pl.BlockSpec((d, LANES), lambda b_, i: (0, 0))],
        out_specs=[h_spec, pl.BlockSpec((1, bm, LANES), lambda b_, i: (b_, i, 0))],
        out_shape=[h_shape, jax.ShapeDtypeStruct((b, ta, LANES), F32)],
        compiler_params=_cparams("parallel", "parallel"), name="norm_mod_router",
    )(*args, rpad)


def _proj_kernel(h_ref, w_ref, e_ref, et_ref, igs_ref, u_ref, gain_ref, c_ref, s_ref, o_ref, *,
                 rope_blocks, rope_half):
    y = _dot(h_ref[0], w_ref[...])
    s_hi, s_lo = _split_bf16(y * y)
    ss = _dot(s_hi, e_ref[...]) + _dot(s_lo, e_ref[...])
    inv = lax.rsqrt(ss * igs_ref[...] + EPS)
    i_hi, i_lo = _split_bf16(inv)
    fac = _dot(i_hi, et_ref[...]) + _dot(i_lo, et_ref[...]) + u_ref[...]
    z = y * (fac * gain_ref[...])
    n = z.shape[1]
    cos = c_ref[...]
    sin = s_ref[...]
    if rope_half == LANES // 2:
        first = None
    else:
        lane = lax.broadcasted_iota(jnp.int32, cos.shape, 1)
        first = (lane % (2 * rope_half)) < rope_half
    for blk in range(n // LANES):
        zb = z[:, blk * LANES:(blk + 1) * LANES]
        if blk in rope_blocks:
            if first is None:
                rot = pltpu.roll(zb, LANES // 2, 1)
            else:
                rot = jnp.where(first, pltpu.roll(zb, LANES - rope_half, 1), pltpu.roll(zb, rope_half, 1))
            zb = zb * cos + rot * sin
        o_ref[0, :, blk * LANES:(blk + 1) * LANES] = zb.astype(BF16)


def proj(h, kin_block, w, groups, gain, rope_tabs, rope_blocks, rope_half, name):
    b, ta, _ = h.shape
    k, n = w.shape
    bm = ROW_TILE
    e = np.zeros((n, LANES), np.float32)
    igs = np.zeros((1, LANES), np.float32)
    u = np.ones((1, n), np.float32)
    for gi, (start, size) in enumerate(groups):
        e[start:start + size, gi] = 1.0
        igs[0, gi] = 1.0 / size
        u[0, start:start + size] = 0.0
    cos, sin = rope_tabs
    kern = functools.partial(_proj_kernel, rope_blocks=frozenset(rope_blocks), rope_half=rope_half)
    return pl.pallas_call(
        kern,
        grid=(b, ta // bm),
        in_specs=[
            pl.BlockSpec((1, bm, k), lambda b_, i: (b_, i, kin_block)),
            pl.BlockSpec((k, n), lambda b_, i: (0, 0)),
            pl.BlockSpec((n, LANES), lambda b_, i: (0, 0)),
            pl.BlockSpec((LANES, n), lambda b_, i: (0, 0)),
            pl.BlockSpec((1, LANES), lambda b_, i: (0, 0)),
            pl.BlockSpec((1, n), lambda b_, i: (0, 0)),
            pl.BlockSpec((1, n), lambda b_, i: (0, 0)),
            pl.BlockSpec((bm, LANES), lambda b_, i: (i, 0)),
            pl.BlockSpec((bm, LANES), lambda b_, i: (i, 0)),
        ],
        out_specs=pl.BlockSpec((1, bm, n), lambda b_, i: (b_, i, 0)),
        out_shape=jax.ShapeDtypeStruct((b, ta, n), BF16),
        compiler_params=_cparams("parallel", "parallel"),
        name=name,
    )(h, w, jnp.asarray(e, BF16), jnp.asarray(e.T, BF16), jnp.asarray(igs), jnp.asarray(u),
      gain.reshape(1, n).astype(F32), cos, sin)


def _softmax_pv(s, v):
    m = jnp.max(s, axis=-1, keepdims=True)
    p = jnp.exp(s - m)
    l = jnp.sum(p, axis=-1, keepdims=True)
    return _dot(p.astype(BF16), v) / l


def _gqa_attn_kernel(q_ref, k_ref, v_ref, o_ref, *, n_ctx, group, hd):
    i = pl.program_id(2)

    def attend(nk):
        k = k_ref[0, :nk, :]
        v = v_ref[0, :nk, :]
        for g in range(group):
            q = q_ref[0, :, g * hd:(g + 1) * hd]
            o = _softmax_pv(_dot_nt(q, k), v)
            o_ref[0, :, g * hd:(g + 1) * hd] = o.astype(BF16)

    nct = n_ctx // ROW_TILE

    @pl.when(i < nct)
    def _():
        attend(n_ctx)

    @pl.when(i >= nct)
    def _():
        attend(k_ref.shape[1])


def gqa_attention(qkv, n_ctx):
    b, ta, _ = qkv.shape
    hd, group, kvh = GQA_HEAD_DIM, GQA_HEADS // GQA_KV_HEADS, GQA_KV_HEADS
    bq = ROW_TILE
    qw = group * hd
    kb = GQA_HEADS
    vb = GQA_HEADS + kvh
    kern = functools.partial(_gqa_attn_kernel, n_ctx=n_ctx, group=group, hd=hd)
    return pl.pallas_call(
        kern,
        grid=(b, kvh, ta // bq),
        in_specs=[
            pl.BlockSpec((1, bq, qw), lambda b_, h, i: (b_, i, h)),
            pl.BlockSpec((1, ta, hd), lambda b_, h, i: (b_, 0, kb + h)),
            pl.BlockSpec((1, ta, hd), lambda b_, h, i: (b_, 0, vb + h)),
        ],
        out_specs=pl.BlockSpec((1, bq, qw), lambda b_, h, i: (b_, i, h)),
        out_shape=jax.ShapeDtypeStruct((b, ta, GQA_HEADS * hd), BF16),
        compiler_params=_cparams("parallel", "parallel", "arbitrary"),
        name="gqa_attn",
    )(qkv, qkv, qkv)


def _mla_attn_kernel(qn_ref, qp_ref, kn_ref, kp_ref, v_ref, o_ref, *, n_ctx):
    i = pl.program_id(2)

    def attend(nk):
        s = _dot_nt(qn_ref[0], kn_ref[0, :nk, :]) + _dot_nt(qp_ref[0], kp_ref[0, :nk, :])
        o_ref[0] = _softmax_pv(s, v_ref[0, :nk, :]).astype(BF16)

    nct = n_ctx // ROW_TILE

    @pl.when(i < nct)
    def _():
        attend(n_ctx)

    @pl.when(i >= nct)
    def _():
        attend(kn_ref.shape[1])


def mla_attention(q, kv, dn, kpe_block, n_ctx):
    b, ta, _ = q.shape
    bq = ROW_TILE
    hh = MLA_HEADS
    kern = functools.partial(_mla_attn_kernel, n_ctx=n_ctx)
    return pl.pallas_call(
        kern,
        grid=(b, hh, ta // bq),
        in_specs=[
            pl.BlockSpec((1, bq, LANES), lambda b_, h, i: (b_, i, h)),
            pl.BlockSpec((1, bq, LANES), lambda b_, h, i: (b_, i, hh + h)),
            pl.BlockSpec((1, ta, LANES), lambda b_, h, i: (b_, 0, 2 * h)),
            pl.BlockSpec((1, ta, LANES), lambda b_, h, i: (b_, 0, kpe_block)),
            pl.BlockSpec((1, ta, LANES), lambda b_, h, i: (b_, 0, 2 * h + 1)),
        ],
        out_specs=pl.BlockSpec((1, bq, LANES), lambda b_, h, i: (b_, i, h)),
        out_shape=jax.ShapeDtypeStruct((b, ta, hh * MLA_V), BF16),
        compiler_params=_cparams("parallel", "parallel", "arbitrary"),
        name="mla_attn",
    )(q, q, kv, dn, kv)


def _half_masks(shape):
    lane = lax.broadcasted_iota(jnp.int32, shape, 1)
    return lane < (LANES // 2)


def _diff_attn_kernel(q_ref, k_ref, v_ref, lam_ref, sub_ref, o_ref, *, n_ctx, lam_init):
    i = pl.program_id(2)
    lp = lam_ref[...]
    lam = (jnp.exp(jnp.sum(lp[0:1] * lp[1:2], axis=-1, keepdims=True))
           - jnp.exp(jnp.sum(lp[2:3] * lp[3:4], axis=-1, keepdims=True)) + lam_init)

    def attend(nk):
        q = q_ref[0]
        lo = _half_masks(q.shape)
        zero = jnp.zeros_like(q)
        k = k_ref[0, :nk, :]
        v = v_ref[0, :nk, :]
        o0 = _softmax_pv(_dot_nt(jnp.where(lo, q, zero), k), v)
        o1 = _softmax_pv(_dot_nt(jnp.where(lo, zero, q), k), v)
        o = o0 - lam * o1
        ms = jnp.mean(o * o, axis=-1, keepdims=True)
        o_ref[0] = (o * lax.rsqrt(ms + EPS) * sub_ref[...] * (1.0 - lam_init)).astype(BF16)

    nct = n_ctx // ROW_TILE

    @pl.when(i < nct)
    def _():
        attend(n_ctx)

    @pl.when(i >= nct)
    def _():
        attend(k_ref.shape[1])


def diff_attention(qkv, lam_p, subln, lam_init, n_ctx):
    b, ta, _ = qkv.shape
    bq = ROW_TILE
    hh = DIFF_HEADS
    kern = functools.partial(_diff_attn_kernel, n_ctx=n_ctx, lam_init=lam_init)
    return pl.pallas_call(
        kern,
        grid=(b, hh, ta // bq),
        in_specs=[
            pl.BlockSpec((1, bq, LANES), lambda b_, h, i: (b_, i, h)),
            pl.BlockSpec((1, ta, LANES), lambda b_, h, i: (b_, 0, hh + h)),
            pl.BlockSpec((1, ta, LANES), lambda b_, h, i: (b_, 0, 2 * hh + h)),
            pl.BlockSpec((4, DIFF_HEAD_DIM), lambda b_, h, i: (0, 0)),
            pl.BlockSpec((1, LANES), lambda b_, h, i: (0, 0)),
        ],
        out_specs=pl.BlockSpec((1, bq, LANES), lambda b_, h, i: (b_, i, h)),
        out_shape=jax.ShapeDtypeStruct((b, ta, hh * 2 * DIFF_HEAD_DIM), BF16),
        compiler_params=_cparams("parallel", "parallel", "arbitrary"),
        name="diff_attn",
    )(qkv, qkv, qkv, lam_p.astype(F32), subln.reshape(1, LANES).astype(F32))


def _win_attn_kernel(sink_ref, q_ref, k_ref, v_ref, o_ref, *, n_ctx, seq, n_pairs):
    i = pl.program_id(1)
    bq = q_ref.shape[1]
    band = bq + 2 * WINDOW
    nct = n_ctx // bq

    def head_out(qm, kv_idx, pair, blocks):
        sk = sink_ref[kv_idx * n_pairs + pair]
        ss = []
        m = None
        for k, _, mask in blocks:
            s = _dot_nt(qm, k)
            if mask is not None:
                s = jnp.where(mask, s, NEG_INF)
            ss.append(s)
            bm_ = jnp.max(s, axis=-1, keepdims=True)
            m = bm_ if m is None else jnp.maximum(m, bm_)
        m = jnp.maximum(m, sk)
        l = jnp.exp(sk - m)
        o = None
        for s, (_, v, _) in zip(ss, blocks):
            p = jnp.exp(s - m)
            l = l + jnp.sum(p, axis=-1, keepdims=True)
            pv = _dot(p.astype(BF16), v)
            o = pv if o is None else o + pv
        return o / l

    def run(blocks):
        for pair in range(n_pairs):
            q = q_ref[0, :, pair * LANES:(pair + 1) * LANES]
            lo = _half_masks(q.shape)
            zero = jnp.zeros_like(q)
            o0 = head_out(jnp.where(lo, q, zero), 0, pair, blocks)
            o1 = head_out(jnp.where(lo, zero, q), 1, pair, blocks)
            o_ref[0, :, pair * LANES:(pair + 1) * LANES] = jnp.where(lo, o0, o1).astype(BF16)

    @pl.when(i < nct)
    def _():
        run([(k_ref[0, :n_ctx, :], v_ref[0, :n_ctx, :], None)])

    @pl.when(i >= nct)
    def _():
        q0 = (i - nct) * bq
        start = jnp.clip(q0 - WINDOW, 0, seq - band)
        start = pl.multiple_of(start, WINDOW)
        qpos = q0 + lax.broadcasted_iota(jnp.int32, (bq, band), 0)
        kpos = start + lax.broadcasted_iota(jnp.int32, (bq, band), 1)
        mask = jnp.abs(qpos - kpos) <= WINDOW
        kb = k_ref[0, pl.ds(n_ctx + start, band), :]
        vb = v_ref[0, pl.ds(n_ctx + start, band), :]
        run([(k_ref[0, :n_ctx, :], v_ref[0, :n_ctx, :], None), (kb, vb, mask)])


def win_attention(qkv, sink, n_ctx):
    b, ta, _ = qkv.shape
    bq = ROW_TILE
    n_pairs = WIN_HEADS // WIN_KV_HEADS
    qw = n_pairs * LANES
    kern = functools.partial(_win_attn_kernel, n_ctx=n_ctx, seq=ta - n_ctx, n_pairs=n_pairs)
    return pl.pallas_call(
        kern,
        grid_spec=pltpu.PrefetchScalarGridSpec(
            num_scalar_prefetch=1,
            grid=(b, ta // bq),
            in_specs=[
                pl.BlockSpec((1, bq, qw), lambda b_, i, s: (b_, i, 0)),
                pl.BlockSpec((1, ta, LANES), lambda b_, i, s: (b_, 0, n_pairs)),
                pl.BlockSpec((1, ta, LANES), lambda b_, i, s: (b_, 0, n_pairs + 1)),
            ],
            out_specs=pl.BlockSpec((1, bq, qw), lambda b_, i, s: (b_, i, 0)),
        ),
        out_shape=jax.ShapeDtypeStruct((b, ta, qw), BF16),
        compiler_params=_cparams("parallel", "arbitrary"),
        name="win_attn",
    )(sink.astype(F32), qkv, qkv, qkv)


def _out_proj_kernel(o_ref, w_ref, x_ref, g_ref, y_ref):
    y_ref[0] = x_ref[0] + g_ref[...] * _dot(o_ref[0], w_ref[...])


def out_proj_residual(o, w, xa, mods, k_gate, n_ctx):
    b, ta, d = xa.shape
    kdim = o.shape[2]
    bm = ROW_TILE
    return pl.pallas_call(
        _out_proj_kernel,
        grid=(b, ta // bm),
        in_specs=[
            pl.BlockSpec((1, bm, kdim), lambda b_, i: (b_, i, 0)),
            pl.BlockSpec((kdim, d), lambda b_, i: (0, 0)),
            pl.BlockSpec((1, bm, d), lambda b_, i: (b_, i, 0)),
            _mod_spec(k_gate, n_ctx // bm, d),
        ],
        out_specs=pl.BlockSpec((1, bm, d), lambda b_, i: (b_, i, 0)),
        out_shape=jax.ShapeDtypeStruct(xa.shape, F32),
        input_output_aliases={2: 0},
        compiler_params=_cparams("parallel", "parallel"),
        name="out_proj",
    )(o, w, xa, mods)


def _ffn_kernel(h_ref, wg_ref, wu_ref, w2_ref, x_ref, gc_ref, gl_ref, *rest, n_ctx, weighted):
    if weighted:
        cw_ref, y_ref, acc_ref = rest
    else:
        y_ref, acc_ref = rest
    i = pl.program_id(1)
    f = pl.program_id(2)

    @pl.when(f == 0)
    def _():
        acc_ref[...] = jnp.zeros_like(acc_ref)

    h = h_ref[0]
    g = _dot(h, wg_ref[...])
    u = _dot(h, wu_ref[...])
    act = (g * (1.0 / (1.0 + jnp.exp(-g))) * u).astype(BF16)
    acc_ref[...] += _dot(act, w2_ref[...])

    @pl.when(f == pl.num_programs(2) - 1)
    def _():
        bm = acc_ref.shape[0]
        row = i * bm + lax.broadcasted_iota(jnp.int32, (bm, 1), 0)
        gate = jnp.where(row < n_ctx, gc_ref[...], gl_ref[...])
        out = acc_ref[...]
        if weighted:
            out = out * cw_ref[0]
        y_ref[0] = x_ref[0] + gate * out


def ffn_residual(h, w13, w2, xa, mods, k_gate, n_ctx, row_weight=None):
    b, ta, d = xa.shape
    ff = w2.shape[0]
    bm, fk = FFN_ROW_TILE, FF_CHUNK
    nf = ff // fk
    weighted = row_weight is not None
    kern = functools.partial(_ffn_kernel, n_ctx=n_ctx, weighted=weighted)
    in_specs = [
        pl.BlockSpec((1, bm, d), lambda b_, i, f: (b_, i, 0)),
        pl.BlockSpec((d, fk), lambda b_, i, f: (0, f)),
        pl.BlockSpec((d, fk), lambda b_, i, f: (0, nf + f)),
        pl.BlockSpec((fk, d), lambda b_, i, f: (f, 0)),
        pl.BlockSpec((1, bm, d), lambda b_, i, f: (b_, i, 0)),
        pl.BlockSpec((None, None, None, 1, d), lambda b_, i, f: (b_, 0, k_gate, 0, 0)),
        pl.BlockSpec((None, None, None, 1, d), lambda b_, i, f: (b_, 1, k_gate, 0, 0)),
    ]
    args = [h, w13, w13, w2, xa, mods, mods]
    if weighted:
        in_specs.append(pl.BlockSpec((1, bm, 1), lambda b_, i, f: (b_, i, 0)))
        args.append(row_weight)
    return pl.pallas_call(
        kern,
        grid=(b, ta // bm, nf),
        in_specs=in_specs,
        out_specs=pl.BlockSpec((1, bm, d), lambda b_, i, f: (b_, i, 0)),
        out_shape=jax.ShapeDtypeStruct(xa.shape, F32),
        scratch_shapes=[pltpu.VMEM((bm, d), F32)],
        input_output_aliases={4: 0},
        compiler_params=_cparams("parallel", "parallel", "arbitrary"),
        name="ffn",
    )(*args)


def _rope_tables(seq, n_ctx, rot_dim):
    rows = seq // GRID_W
    quarter = rot_dim // 4
    inv_freq = ROPE_THETA ** (-jnp.arange(quarter, dtype=F32) / quarter)
    row = jnp.repeat(jnp.arange(rows, dtype=F32), GRID_W)
    col = jnp.tile(jnp.arange(GRID_W, dtype=F32), rows)
    ang = jnp.concatenate([row[:, None] * inv_freq, col[:, None] * inv_freq], axis=-1)
    cos, sin = jnp.cos(ang), jnp.sin(ang)
    reps = LANES // rot_dim
    c = jnp.tile(jnp.concatenate([cos, cos], axis=-1), (1, reps))
    s = jnp.tile(jnp.concatenate([-sin, sin], axis=-1), (1, reps))
    c = jnp.concatenate([jnp.ones((n_ctx, LANES), F32), c], axis=0)
    s = jnp.concatenate([jnp.zeros((n_ctx, LANES), F32), s], axis=0)
    return c, s


def kernel(x, c, ctx, c_ctx, ada_w, ada_b, norm_mix, norm_ffn, gqa_wqkv, gqa_q_gain, gqa_k_gain, gqa_wo, mla_wdown, mla_qa_gain, mla_kva_gain, mla_wuq, mla_wukv, mla_q_gain, mla_k_gain, mla_wo, win_wqkv, win_q_gain, win_k_gain, win_sink, win_wo, diff_wqkv, diff_q_gain, diff_k_gain, diff_lambda, diff_subln, diff_wo, ffn_w13, ffn_w2, moe_router, moe_w13, moe_w2):
    b, seq, d = x.shape
    n_ctx = ctx.shape[1]
    depth = ada_w.shape[0]
    ones = lambda n: jnp.ones((n,), F32)
    zeros = lambda n: jnp.zeros((n,), F32)

    crows = -(-(b + 1) // 8) * 8
    cond = jnp.zeros((crows, d), F32).at[:b].set(c).at[b].set(c_ctx)
    mods_all = adaln(cond, ada_w, ada_b)

    xa = jnp.concatenate([ctx, x], axis=1)

    for i in range(depth):
        need_ctx = i < depth - 1
        j = i // 4
        kind = i % 4
        ml = mods_all[i, :b].reshape(b, 6, 1, d)
        mc = jnp.broadcast_to(mods_all[i, b].reshape(1, 6, 1, d), (b, 6, 1, d))
        mods = jnp.stack([mc, ml], axis=1)

        h = norm_mod(xa, norm_mix[i], mods, 0, 1, n_ctx)
        if kind == 0:
            hd = GQA_HEAD_DIM
            scale = hd ** -0.5
            nq, nk = GQA_HEADS * hd, GQA_KV_HEADS * hd
            groups = [(g * hd, hd) for g in range(GQA_HEADS + GQA_KV_HEADS)]
            gain = jnp.concatenate([jnp.tile(gqa_q_gain[j] * scale, GQA_HEADS),
                                    jnp.tile(gqa_k_gain[j], GQA_KV_HEADS), ones(nk)])
            qkv = proj(h, 0, gqa_wqkv[j].astype(BF16), groups, gain, _rope_tables(seq, n_ctx, hd),
                       range((nq + nk) // LANES), hd // 2, "gqa_qkv")
            o = gqa_attention(qkv, n_ctx)
            w_o = gqa_wo[j]
        elif kind == 1:
            scale = (MLA_NOPE + MLA_ROPE) ** -0.5
            hh = MLA_HEADS
            wd = mla_wdown[j]
            pad = jnp.zeros((d, LANES - MLA_ROPE), F32)
            wd2 = jnp.concatenate([wd[:, :MLA_Q_LORA], wd[:, MLA_Q_LORA + MLA_KV_LORA:], pad,
                                   wd[:, MLA_Q_LORA:MLA_Q_LORA + MLA_KV_LORA]], axis=1)
            kpe0 = MLA_Q_LORA
            ckv0 = MLA_Q_LORA + LANES
            groups = [(0, MLA_Q_LORA), (kpe0, MLA_ROPE), (ckv0, MLA_KV_LORA)]
            gain = jnp.concatenate([mla_qa_gain[j], mla_k_gain[j][MLA_NOPE:], zeros(LANES - MLA_ROPE),
                                    mla_kva_gain[j]])
            rope = _rope_tables(seq, n_ctx, MLA_ROPE)
            dn = proj(h, 0, wd2.astype(BF16), groups, gain, rope, [kpe0 // LANES], MLA_ROPE // 2, "mla_down")
            wq = mla_wuq[j].reshape(MLA_Q_LORA, hh, MLA_NOPE + MLA_ROPE)
            wq_n = wq[:, :, :MLA_NOPE].reshape(MLA_Q_LORA, hh * MLA_NOPE)
            wq_p = jnp.pad(wq[:, :, MLA_NOPE:], ((0, 0), (0, 0), (0, LANES - MLA_ROPE))).reshape(MLA_Q_LORA, hh * LANES)
            wq2 = jnp.concatenate([wq_n, wq_p], axis=1)
            groups = ([(g * MLA_NOPE, MLA_NOPE) for g in range(hh)]
                      + [(hh * MLA_NOPE + g * LANES, MLA_ROPE) for g in range(hh)])
            qg = mla_q_gain[j] * scale
            gain = jnp.concatenate([jnp.tile(qg[:MLA_NOPE], hh),
                                    jnp.tile(jnp.concatenate([qg[MLA_NOPE:], zeros(LANES - MLA_ROPE)]), hh)])
            q = proj(dn, 0, wq2.astype(BF16), groups, gain, rope, range(hh, 2 * hh), MLA_ROPE // 2, "mla_q")
            groups = [(g * (MLA_NOPE + MLA_V), MLA_NOPE) for g in range(hh)]
            gain = jnp.tile(jnp.concatenate([mla_k_gain[j][:MLA_NOPE], ones(MLA_V)]), hh)
            kv = proj(dn, ckv0 // MLA_KV_LORA, mla_wukv[j].astype(BF16), groups, gain, rope, [], MLA_ROPE // 2, "mla_kv")
            o = mla_attention(q, kv, dn, kpe0 // LANES, n_ctx)
            w_o = mla_wo[j]
        elif kind == 2:
            hd = WIN_HEAD_DIM
            scale = hd ** -0.5
            n_pairs = WIN_HEADS // WIN_KV_HEADS
            nq = WIN_HEADS * hd
            perm = np.array([(kv * n_pairs + g) * hd + t for g in range(n_pairs)
                             for kv in range(WIN_KV_HEADS) for t in range(hd)])
            wq = win_wqkv[j]
            w2_ = jnp.concatenate([wq[:, perm], wq[:, nq:]], axis=1)
            groups = [(g * hd, hd) for g in range(WIN_HEADS + WIN_KV_HEADS)]
            gain = jnp.concatenate([jnp.tile(win_q_gain[j] * scale, WIN_HEADS),
                                    jnp.tile(win_k_gain[j], WIN_KV_HEADS), ones(WIN_KV_HEADS * hd)])
            qkv = proj(h, 0, w2_.astype(BF16), groups, gain, _rope_tables(seq, n_ctx, hd),
                       range(n_pairs + 1), hd // 2, "win_qkv")
            o = win_attention(qkv, win_sink[j], n_ctx)
            w_o = win_wo[j][perm, :]
        else:
            hd = DIFF_HEAD_DIM
            scale = hd ** -0.5
            lam_init = 0.8 - 0.6 * math.exp(-0.3 * i)
            nqk = 2 * DIFF_HEADS
            groups = [(g * hd, hd) for g in range(2 * nqk)]
            gain = jnp.concatenate([jnp.tile(diff_q_gain[j] * scale, nqk), jnp.tile(diff_k_gain[j], nqk),
                                    ones(DIFF_HEADS * 2 * hd)])
            qkv = proj(h, 0, diff_wqkv[j].astype(BF16), groups, gain, _rope_tables(seq, n_ctx, hd),
                       range(2 * DIFF_HEADS), hd // 2, "diff_qkv")
            o = diff_attention(qkv, diff_lambda[j], diff_subln[j], lam_init, n_ctx)
            w_o = diff_wo[j]
        xa = out_proj_residual(o, w_o.astype(BF16), xa, mods, 2, n_ctx)

        jj = i // 2
        if i % 2 == 0:
            h = norm_mod(xa, norm_ffn[i], mods, 3, 4, n_ctx)
            xa = ffn_residual(h, ffn_w13[jj].astype(BF16), ffn_w2[jj].astype(BF16), xa, mods, 5, n_ctx)
        else:
            h, cw = norm_mod(xa, norm_ffn[i], mods, 3, 4, n_ctx, router=moe_router[jj])
            for e in range(N_EXPERTS):
                xa = ffn_residual(h, moe_w13[jj, e].astype(BF16), moe_w2[jj, e].astype(BF16), xa, mods, 5,
                                  n_ctx, row_weight=cw[:, :, e:e + 1])
    return xa[:, n_ctx:]
```

```python
import functools
import math

import jax
import jax.numpy as jnp
import numpy as np
from jax import lax
from jax.experimental import pallas as pl
from jax.experimental.pallas import tpu as pltpu

GRID_W = 64
ROPE_THETA = 10000.0
EPS = 1e-6
NEG_INF = -1e30
LANES = 128
ROW_TILE = 256
FFN_ROW_TILE = 768
FF_CHUNK = 512
VMEM_LIMIT = 56 * 1024 * 1024

GQA_HEADS, GQA_KV_HEADS, GQA_HEAD_DIM = 8, 2, 128
MLA_HEADS, MLA_Q_LORA, MLA_KV_LORA, MLA_NOPE, MLA_ROPE, MLA_V = 8, 384, 256, 128, 64, 128
WIN_HEADS, WIN_KV_HEADS, WIN_HEAD_DIM, WINDOW = 16, 2, 64, 128
DIFF_HEADS, DIFF_HEAD_DIM = 8, 64
N_EXPERTS = 8

F32 = jnp.float32
BF16 = jnp.bfloat16


def _cparams(*sem):
    return pltpu.CompilerParams(dimension_semantics=sem, vmem_limit_bytes=VMEM_LIMIT)


def _dot(a, b):
    return jnp.dot(a, b, preferred_element_type=F32)


def _dot_nt(a, b):
    return lax.dot_general(a, b, (((1,), (1,)), ((), ())), preferred_element_type=F32)


def _split_bf16(v):
    hi = v.astype(BF16)
    lo = (v - hi.astype(F32)).astype(BF16)
    return hi, lo


def _adaln_kernel(c_ref, w_ref, b_ref, o_ref):
    c = c_ref[...]
    sc = c * (1.0 / (1.0 + jnp.exp(-c)))
    o_ref[0] = jnp.dot(sc, w_ref[0], preferred_element_type=F32,
                       precision=lax.Precision.HIGHEST) + b_ref[0]


def adaln(cond, ada_w, ada_b):
    depth, d, n = ada_w.shape
    rows = cond.shape[0]
    bn = 1536
    return pl.pallas_call(
        _adaln_kernel,
        grid=(depth, n // bn),
        in_specs=[
            pl.BlockSpec((rows, d), lambda l, j: (0, 0)),
            pl.BlockSpec((1, d, bn), lambda l, j: (l, 0, j)),
            pl.BlockSpec((1, 1, bn), lambda l, j: (l, 0, j)),
        ],
        out_specs=pl.BlockSpec((1, rows, bn), lambda l, j: (l, 0, j)),
        out_shape=jax.ShapeDtypeStruct((depth, rows, n), F32),
        compiler_params=_cparams("parallel", "parallel"),
        name="adaln",
    )(cond, ada_w, ada_b.reshape(depth, 1, n))


def _mod_spec(k, n_ctx_tiles, d):
    return pl.BlockSpec((None, None, None, 1, d),
                        lambda b, i, *_: (b, jnp.minimum(i // n_ctx_tiles, 1), k, 0, 0))


def _norm_mod(x, g, shift, scale):
    ms = jnp.mean(x * x, axis=-1, keepdims=True)
    return x * lax.rsqrt(ms + EPS) * g * (1.0 + scale) + shift


def _norm_mod_kernel(x_ref, g_ref, sh_ref, sc_ref, h_ref):
    h_ref[0] = _norm_mod(x_ref[0], g_ref[...], sh_ref[...], sc_ref[...]).astype(BF16)


def _norm_mod_router_kernel(x_ref, g_ref, sh_ref, sc_ref, r_ref, h_ref, cw_ref):
    h = _norm_mod(x_ref[0], g_ref[...], sh_ref[...], sc_ref[...])
    h_ref[0] = h.astype(BF16)
    logits = jnp.dot(h, r_ref[...], preferred_element_type=F32, precision=lax.Precision.HIGHEST)
    lane = lax.broadcasted_iota(jnp.int32, logits.shape, 1)
    logits = jnp.where(lane < N_EXPERTS, logits, -jnp.inf)
    m1 = jnp.max(logits, axis=-1, keepdims=True)
    i1 = jnp.min(jnp.where(logits == m1, lane, LANES), axis=-1, keepdims=True)
    rest = jnp.where(lane == i1, -jnp.inf, logits)
    m2 = jnp.max(rest, axis=-1, keepdims=True)
    i2 = jnp.min(jnp.where(rest == m2, lane, LANES), axis=-1, keepdims=True)
    e2 = jnp.exp(m2 - m1)
    den = 1.0 + e2
    cw_ref[0] = jnp.where(lane == 0, i1.astype(F32), jnp.where(lane == 1, i2.astype(F32),
                          jnp.where(lane == 2, 1.0 / den, jnp.where(lane == 3, e2 / den, 0.0))))


def norm_mod(xa, gain, mods, k_shift, k_scale, n_ctx, router=None):
    b, ta, d = xa.shape
    bm = ROW_TILE
    nct = n_ctx // bm
    in_specs = [
        pl.BlockSpec((1, bm, d), lambda b_, i: (b_, i, 0)),
        pl.BlockSpec((1, d), lambda b_, i: (0, 0)),
        _mod_spec(k_shift, nct, d),
        _mod_spec(k_scale, nct, d),
    ]
    h_spec = pl.BlockSpec((1, bm, d), lambda b_, i: (b_, i, 0))
    h_shape = jax.ShapeDtypeStruct((b, ta, d), BF16)
    args = [xa, gain.reshape(1, d), mods, mods]
    if router is None:
        return pl.pallas_call(
            _norm_mod_kernel, grid=(b, ta // bm), in_specs=in_specs, out_specs=h_spec,
            out_shape=h_shape, compiler_params=_cparams("parallel", "parallel"), name="norm_mod",
        )(*args)
    rpad = jnp.zeros((d, LANES), F32).at[:, :N_EXPERTS].set(router)
    return pl.pallas_call(
        _norm_mod_router_kernel, grid=(b, ta // bm),
        in_specs=in_specs + [pl.BlockSpec((d, LANES), lambda b_, i: (0, 0))],
        out_specs=[h_spec, pl.BlockSpec((1, bm, LANES), lambda b_, i: (b_, i, 0))],
        out_shape=[h_shape, jax.ShapeDtypeStruct((b, ta, LANES), F32)],
        compiler_params=_cparams("parallel", "parallel"), name="norm_mod_router",
    )(*args, rpad)


def _proj_kernel(h_ref, w_ref, e_ref, et_ref, igs_ref, u_ref, gain_ref, c_ref, s_ref, o_ref, *,
                 rope_blocks, rope_half):
    y = _dot(h_ref[0], w_ref[...])
    s_hi, s_lo = _split_bf16(y * y)
    ss = _dot(s_hi, e_ref[...]) + _dot(s_lo, e_ref[...])
    inv = lax.rsqrt(ss * igs_ref[...] + EPS)
    i_hi, i_lo = _split_bf16(inv)
    fac = _dot(i_hi, et_ref[...]) + _dot(i_lo, et_ref[...]) + u_ref[...]
    z = y * (fac * gain_ref[...])
    n = z.shape[1]
    cos = c_ref[...]
    sin = s_ref[...]
    if rope_half == LANES // 2:
        first = None
    else:
        lane = lax.broadcasted_iota(jnp.int32, cos.shape, 1)
        first = (lane % (2 * rope_half)) < rope_half
    for blk in range(n // LANES):
        zb = z[:, blk * LANES:(blk + 1) * LANES]
        if blk in rope_blocks:
            if first is None:
                rot = pltpu.roll(zb, LANES // 2, 1)
            else:
                rot = jnp.where(first, pltpu.roll(zb, LANES - rope_half, 1), pltpu.roll(zb, rope_half, 1))
            zb = zb * cos + rot * sin
        o_ref[0, :, blk * LANES:(blk + 1) * LANES] = zb.astype(BF16)


def proj(h, kin_block, w, groups, gain, rope_tabs, rope_blocks, rope_half, name):
    b, ta, _ = h.shape
    k, n = w.shape
    bm = ROW_TILE
    e = np.zeros((n, LANES), np.float32)
    igs = np.zeros((1, LANES), np.float32)
    u = np.ones((1, n), np.float32)
    for gi, (start, size) in enumerate(groups):
        e[start:start + size, gi] = 1.0
        igs[0, gi] = 1.0 / size
        u[0, start:start + size] = 0.0
    cos, sin = rope_tabs
    kern = functools.partial(_proj_kernel, rope_blocks=frozenset(rope_blocks), rope_half=rope_half)
    return pl.pallas_call(
        kern,
        grid=(b, ta // bm),
        in_specs=[
            pl.BlockSpec((1, bm, k), lambda b_, i: (b_, i, kin_block)),
            pl.BlockSpec((k, n), lambda b_, i: (0, 0)),
            pl.BlockSpec((n, LANES), lambda b_, i: (0, 0)),
            pl.BlockSpec((LANES, n), lambda b_, i: (0, 0)),
            pl.BlockSpec((1, LANES), lambda b_, i: (0, 0)),
            pl.BlockSpec((1, n), lambda b_, i: (0, 0)),
            pl.BlockSpec((1, n), lambda b_, i: (0, 0)),
            pl.BlockSpec((bm, LANES), lambda b_, i: (i, 0)),
            pl.BlockSpec((bm, LANES), lambda b_, i: (i, 0)),
        ],
        out_specs=pl.BlockSpec((1, bm, n), lambda b_, i: (b_, i, 0)),
        out_shape=jax.ShapeDtypeStruct((b, ta, n), BF16),
        compiler_params=_cparams("parallel", "parallel"),
        name=name,
    )(h, w, jnp.asarray(e, BF16), jnp.asarray(e.T, BF16), jnp.asarray(igs), jnp.asarray(u),
      gain.reshape(1, n).astype(F32), cos, sin)


def _softmax_pv(s, v):
    m = jnp.max(s, axis=-1, keepdims=True)
    p = jnp.exp(s - m)
    l = jnp.sum(p, axis=-1, keepdims=True)
    return _dot(p.astype(BF16), v) / l


def _gqa_attn_kernel(q_ref, k_ref, v_ref, o_ref, *, n_ctx, group, hd):
    i = pl.program_id(2)

    def attend(nk):
        k = k_ref[0, :nk, :]
        v = v_ref[0, :nk, :]
        for g in range(group):
            q = q_ref[0, :, g * hd:(g + 1) * hd]
            o = _softmax_pv(_dot_nt(q, k), v)
            o_ref[0, :, g * hd:(g + 1) * hd] = o.astype(BF16)

    nct = n_ctx // ROW_TILE

    @pl.when(i < nct)
    def _():
        attend(n_ctx)

    @pl.when(i >= nct)
    def _():
        attend(k_ref.shape[1])


def gqa_attention(qkv, n_ctx):
    b, ta, _ = qkv.shape
    hd, group, kvh = GQA_HEAD_DIM, GQA_HEADS // GQA_KV_HEADS, GQA_KV_HEADS
    bq = ROW_TILE
    qw = group * hd
    kb = GQA_HEADS
    vb = GQA_HEADS + kvh
    kern = functools.partial(_gqa_attn_kernel, n_ctx=n_ctx, group=group, hd=hd)
    return pl.pallas_call(
        kern,
        grid=(b, kvh, ta // bq),
        in_specs=[
            pl.BlockSpec((1, bq, qw), lambda b_, h, i: (b_, i, h)),
            pl.BlockSpec((1, ta, hd), lambda b_, h, i: (b_, 0, kb + h)),
            pl.BlockSpec((1, ta, hd), lambda b_, h, i: (b_, 0, vb + h)),
        ],
        out_specs=pl.BlockSpec((1, bq, qw), lambda b_, h, i: (b_, i, h)),
        out_shape=jax.ShapeDtypeStruct((b, ta, GQA_HEADS * hd), BF16),
        compiler_params=_cparams("parallel", "parallel", "arbitrary"),
        name="gqa_attn",
    )(qkv, qkv, qkv)


def _mla_attn_kernel(qn_ref, qp_ref, kn_ref, kp_ref, v_ref, o_ref, *, n_ctx):
    i = pl.program_id(2)

    def attend(nk):
        s = _dot_nt(qn_ref[0], kn_ref[0, :nk, :]) + _dot_nt(qp_ref[0], kp_ref[0, :nk, :])
        o_ref[0] = _softmax_pv(s, v_ref[0, :nk, :]).astype(BF16)

    nct = n_ctx // ROW_TILE

    @pl.when(i < nct)
    def _():
        attend(n_ctx)

    @pl.when(i >= nct)
    def _():
        attend(kn_ref.shape[1])


def mla_attention(q, kv, dn, kpe_block, n_ctx):
    b, ta, _ = q.shape
    bq = ROW_TILE
    hh = MLA_HEADS
    kern = functools.partial(_mla_attn_kernel, n_ctx=n_ctx)
    return pl.pallas_call(
        kern,
        grid=(b, hh, ta // bq),
        in_specs=[
            pl.BlockSpec((1, bq, LANES), lambda b_, h, i: (b_, i, h)),
            pl.BlockSpec((1, bq, LANES), lambda b_, h, i: (b_, i, hh + h)),
            pl.BlockSpec((1, ta, LANES), lambda b_, h, i: (b_, 0, 2 * h)),
            pl.BlockSpec((1, ta, LANES), lambda b_, h, i: (b_, 0, kpe_block)),
            pl.BlockSpec((1, ta, LANES), lambda b_, h, i: (b_, 0, 2 * h + 1)),
        ],
        out_specs=pl.BlockSpec((1, bq, LANES), lambda b_, h, i: (b_, i, h)),
        out_shape=jax.ShapeDtypeStruct((b, ta, hh * MLA_V), BF16),
        compiler_params=_cparams("parallel", "parallel", "arbitrary"),
        name="mla_attn",
    )(q, q, kv, dn, kv)


def _half_masks(shape):
    lane = lax.broadcasted_iota(jnp.int32, shape, 1)
    return lane < (LANES // 2)


def _diff_attn_kernel(q_ref, k_ref, v_ref, lam_ref, sub_ref, o_ref, *, n_ctx, lam_init):
    i = pl.program_id(2)
    lp = lam_ref[...]
    lam = (jnp.exp(jnp.sum(lp[0:1] * lp[1:2], axis=-1, keepdims=True))
           - jnp.exp(jnp.sum(lp[2:3] * lp[3:4], axis=-1, keepdims=True)) + lam_init)

    def attend(nk):
        q = q_ref[0]
        lo = _half_masks(q.shape)
        zero = jnp.zeros_like(q)
        k = k_ref[0, :nk, :]
        v = v_ref[0, :nk, :]
        o0 = _softmax_pv(_dot_nt(jnp.where(lo, q, zero), k), v)
        o1 = _softmax_pv(_dot_nt(jnp.where(lo, zero, q), k), v)
        o = o0 - lam * o1
        ms = jnp.mean(o * o, axis=-1, keepdims=True)
        o_ref[0] = (o * lax.rsqrt(ms + EPS) * sub_ref[...] * (1.0 - lam_init)).astype(BF16)

    nct = n_ctx // ROW_TILE

    @pl.when(i < nct)
    def _():
        attend(n_ctx)

    @pl.when(i >= nct)
    def _():
        attend(k_ref.shape[1])


def diff_attention(qkv, lam_p, subln, lam_init, n_ctx):
    b, ta, _ = qkv.shape
    bq = ROW_TILE
    hh = DIFF_HEADS
    kern = functools.partial(_diff_attn_kernel, n_ctx=n_ctx, lam_init=lam_init)
    return pl.pallas_call(
        kern,
        grid=(b, hh, ta // bq),
        in_specs=[
            pl.BlockSpec((1, bq, LANES), lambda b_, h, i: (b_, i, h)),
            pl.BlockSpec((1, ta, LANES), lambda b_, h, i: (b_, 0, hh + h)),
            pl.BlockSpec((1, ta, LANES), lambda b_, h, i: (b_, 0, 2 * hh + h)),
            pl.BlockSpec((4, DIFF_HEAD_DIM), lambda b_, h, i: (0, 0)),
            pl.BlockSpec((1, LANES), lambda b_, h, i: (0, 0)),
        ],
        out_specs=pl.BlockSpec((1, bq, LANES), lambda b_, h, i: (b_, i, h)),
        out_shape=jax.ShapeDtypeStruct((b, ta, hh * 2 * DIFF_HEAD_DIM), BF16),
        compiler_params=_cparams("parallel", "parallel", "arbitrary"),
        name="diff_attn",
    )(qkv, qkv, qkv, lam_p.astype(F32), subln.reshape(1, LANES).astype(F32))


def _win_attn_kernel(sink_ref, q_ref, k_ref, v_ref, o_ref, *, n_ctx, seq, n_pairs):
    i = pl.program_id(1)
    bq = q_ref.shape[1]
    band = bq + 2 * WINDOW
    nct = n_ctx // bq

    def head_out(qm, kv_idx, pair, blocks):
        sk = sink_ref[kv_idx * n_pairs + pair]
        ss = []
        m = None
        for k, _, mask in blocks:
            s = _dot_nt(qm, k)
            if mask is not None:
                s = jnp.where(mask, s, NEG_INF)
            ss.append(s)
            bm_ = jnp.max(s, axis=-1, keepdims=True)
            m = bm_ if m is None else jnp.maximum(m, bm_)
        m = jnp.maximum(m, sk)
        l = jnp.exp(sk - m)
        o = None
        for s, (_, v, _) in zip(ss, blocks):
            p = jnp.exp(s - m)
            l = l + jnp.sum(p, axis=-1, keepdims=True)
            pv = _dot(p.astype(BF16), v)
            o = pv if o is None else o + pv
        return o / l

    def run(blocks):
        for pair in range(n_pairs):
            q = q_ref[0, :, pair * LANES:(pair + 1) * LANES]
            lo = _half_masks(q.shape)
            zero = jnp.zeros_like(q)
            o0 = head_out(jnp.where(lo, q, zero), 0, pair, blocks)
            o1 = head_out(jnp.where(lo, zero, q), 1, pair, blocks)
            o_ref[0, :, pair * LANES:(pair + 1) * LANES] = jnp.where(lo, o0, o1).astype(BF16)

    @pl.when(i < nct)
    def _():
        run([(k_ref[0, :n_ctx, :], v_ref[0, :n_ctx, :], None)])

    @pl.when(i >= nct)
    def _():
        q0 = (i - nct) * bq
        start = jnp.clip(q0 - WINDOW, 0, seq - band)
        start = pl.multiple_of(start, WINDOW)
        qpos = q0 + lax.broadcasted_iota(jnp.int32, (bq, band), 0)
        kpos = start + lax.broadcasted_iota(jnp.int32, (bq, band), 1)
        mask = jnp.abs(qpos - kpos) <= WINDOW
        kb = k_ref[0, pl.ds(n_ctx + start, band), :]
        vb = v_ref[0, pl.ds(n_ctx + start, band), :]
        run([(k_ref[0, :n_ctx, :], v_ref[0, :n_ctx, :], None), (kb, vb, mask)])


def win_attention(qkv, sink, n_ctx):
    b, ta, _ = qkv.shape
    bq = ROW_TILE
    n_pairs = WIN_HEADS // WIN_KV_HEADS
    qw = n_pairs * LANES
    kern = functools.partial(_win_attn_kernel, n_ctx=n_ctx, seq=ta - n_ctx, n_pairs=n_pairs)
    return pl.pallas_call(
        kern,
        grid_spec=pltpu.PrefetchScalarGridSpec(
            num_scalar_prefetch=1,
            grid=(b, ta // bq),
            in_specs=[
                pl.BlockSpec((1, bq, qw), lambda b_, i, s: (b_, i, 0)),
                pl.BlockSpec((1, ta, LANES), lambda b_, i, s: (b_, 0, n_pairs)),
                pl.BlockSpec((1, ta, LANES), lambda b_, i, s: (b_, 0, n_pairs + 1)),
            ],
            out_specs=pl.BlockSpec((1, bq, qw), lambda b_, i, s: (b_, i, 0)),
        ),
        out_shape=jax.ShapeDtypeStruct((b, ta, qw), BF16),
        compiler_params=_cparams("parallel", "arbitrary"),
        name="win_attn",
    )(sink.astype(F32), qkv, qkv, qkv)


def _out_proj_kernel(o_ref, w_ref, x_ref, g_ref, y_ref):
    y_ref[0] = x_ref[0] + g_ref[...] * _dot(o_ref[0], w_ref[...])


def out_proj_residual(o, w, xa, mods, k_gate, n_ctx):
    b, ta, d = xa.shape
    kdim = o.shape[2]
    bm = ROW_TILE
    return pl.pallas_call(
        _out_proj_kernel,
        grid=(b, ta // bm),
        in_specs=[
            pl.BlockSpec((1, bm, kdim), lambda b_, i: (b_, i, 0)),
            pl.BlockSpec((kdim, d), lambda b_, i: (0, 0)),
            pl.BlockSpec((1, bm, d), lambda b_, i: (b_, i, 0)),
            _mod_spec(k_gate, n_ctx // bm, d),
        ],
        out_specs=pl.BlockSpec((1, bm, d), lambda b_, i: (b_, i, 0)),
        out_shape=jax.ShapeDtypeStruct(xa.shape, F32),
        input_output_aliases={2: 0},
        compiler_params=_cparams("parallel", "parallel"),
        name="out_proj",
    )(o, w, xa, mods)


def _ffn_kernel(h_ref, wg_ref, wu_ref, w2_ref, x_ref, gc_ref, gl_ref, y_ref, acc_ref, *, n_ctx):
    i = pl.program_id(1)
    f = pl.program_id(2)

    @pl.when(f == 0)
    def _():
        acc_ref[...] = jnp.zeros_like(acc_ref)

    h = h_ref[0]
    g = _dot(h, wg_ref[...])
    u = _dot(h, wu_ref[...])
    act = (g * (1.0 / (1.0 + jnp.exp(-g))) * u).astype(BF16)
    acc_ref[...] += _dot(act, w2_ref[...])

    @pl.when(f == pl.num_programs(2) - 1)
    def _():
        bm = acc_ref.shape[0]
        row = i * bm + lax.broadcasted_iota(jnp.int32, (bm, 1), 0)
        gate = jnp.where(row < n_ctx, gc_ref[...], gl_ref[...])
        y_ref[0] = x_ref[0] + gate * acc_ref[...]


def ffn_residual(h, w13, w2, xa, mods, k_gate, n_ctx):
    b, ta, d = xa.shape
    ff = w2.shape[0]
    bm, fk = FFN_ROW_TILE, FF_CHUNK
    nf = ff // fk
    kern = functools.partial(_ffn_kernel, n_ctx=n_ctx)
    in_specs = [
        pl.BlockSpec((1, bm, d), lambda b_, i, f: (b_, i, 0)),
        pl.BlockSpec((d, fk), lambda b_, i, f: (0, f)),
        pl.BlockSpec((d, fk), lambda b_, i, f: (0, nf + f)),
        pl.BlockSpec((fk, d), lambda b_, i, f: (f, 0)),
        pl.BlockSpec((1, bm, d), lambda b_, i, f: (b_, i, 0)),
        pl.BlockSpec((None, None, None, 1, d), lambda b_, i, f: (b_, 0, k_gate, 0, 0)),
        pl.BlockSpec((None, None, None, 1, d), lambda b_, i, f: (b_, 1, k_gate, 0, 0)),
    ]
    args = [h, w13, w13, w2, xa, mods, mods]
    return pl.pallas_call(
        kern,
        grid=(b, ta // bm, nf),
        in_specs=in_specs,
        out_specs=pl.BlockSpec((1, bm, d), lambda b_, i, f: (b_, i, 0)),
        out_shape=jax.ShapeDtypeStruct(xa.shape, F32),
        scratch_shapes=[pltpu.VMEM((bm, d), F32)],
        input_output_aliases={4: 0},
        compiler_params=_cparams("parallel", "parallel", "arbitrary"),
        name="ffn",
    )(*args)


MOE_SLOT_TILE = 512
MOE_GATHER_CHUNK = 512
MOE_COMBINE_CHUNK = 768
_VALID, _FIRST, _LAST = 1, 2, 4


def _one_hot_bf16(cond):
    return jnp.where(cond, 1.0, 0.0).astype(BF16)


def _moe_gather_kernel(tile_ref, chunk_ref, flag_ref, src_ref, h_ref, o_ref, acc_ref):
    k = pl.program_id(0)
    fl = flag_ref[k]

    @pl.when((fl & _FIRST) != 0)
    def _():
        acc_ref[...] = jnp.zeros_like(acc_ref)

    @pl.when((fl & _VALID) != 0)
    def _():
        bm, ch = acc_ref.shape[0], h_ref.shape[0]
        tok = chunk_ref[k] * ch + lax.broadcasted_iota(jnp.int32, (bm, ch), 1)
        acc_ref[...] += _dot(_one_hot_bf16(src_ref[...] == tok), h_ref[...])

    @pl.when((fl & _LAST) != 0)
    def _():
        o_ref[...] = acc_ref[...].astype(BF16)


def _moe_ffn_kernel(te_ref, nu_ref, xs_ref, wg_ref, wu_ref, w2_ref, gs_ref, y_ref, acc_ref):
    i = pl.program_id(0)
    f = pl.program_id(1)

    @pl.when(i < nu_ref[0])
    def _():
        @pl.when(f == 0)
        def _():
            acc_ref[...] = jnp.zeros_like(acc_ref)

        xs = xs_ref[...]
        g = _dot(xs, wg_ref[...])
        u = _dot(xs, wu_ref[...])
        act = (g * (1.0 / (1.0 + jnp.exp(-g))) * u).astype(BF16)
        acc_ref[...] += _dot(act, w2_ref[...])

        @pl.when(f == pl.num_programs(1) - 1)
        def _():
            y_ref[...] = (acc_ref[...] * gs_ref[...]).astype(BF16)


def _moe_combine_kernel(chunk_ref, tile_ref, flag_ref, p1_ref, p2_ref, y_ref, x_ref, gc_ref, gl_ref,
                        o_ref, acc_ref, *, n_ctx, chunks_per_batch):
    k = pl.program_id(0)
    fl = flag_ref[k]
    ch, bm = acc_ref.shape[0], y_ref.shape[0]

    @pl.when((fl & _FIRST) != 0)
    def _():
        acc_ref[...] = jnp.zeros_like(acc_ref)

    @pl.when((fl & _VALID) != 0)
    def _():
        slot = tile_ref[k] * bm + lax.broadcasted_iota(jnp.int32, (ch, bm), 1)
        w = _one_hot_bf16((p1_ref[...] == slot) | (p2_ref[...] == slot))
        acc_ref[...] += _dot(w, y_ref[...])

    @pl.when((fl & _LAST) != 0)
    def _():
        row = (chunk_ref[k] % chunks_per_batch) * ch + lax.broadcasted_iota(jnp.int32, (ch, 1), 0)
        gate = jnp.where(row < n_ctx, gc_ref[...], gl_ref[...])
        o_ref[...] = x_ref[...] + gate * acc_ref[...]


def _expand_items(counts, n_items):
    off = jnp.cumsum(counts)
    total = off[-1]
    k = jnp.minimum(jnp.arange(n_items, dtype=jnp.int32), total - 1)
    grp = jnp.minimum(jnp.searchsorted(off, k, side="right").astype(jnp.int32), counts.shape[0] - 1)
    local = k - (off[grp] - counts[grp])
    valid = jnp.arange(n_items, dtype=jnp.int32) < total
    return grp, local, valid


def moe_residual(h, route, w13, w2, xa, mods, k_gate, n_ctx, ctx_active):
    b, ta, d = xa.shape
    t = b * ta
    ne, ff = w2.shape[0], w2.shape[1]
    bm, gch, cch, fk = MOE_SLOT_TILE, MOE_GATHER_CHUNK, MOE_COMBINE_CHUNK, FF_CHUNK
    nf = ff // fk
    nt = (2 * t) // bm + ne
    ns = nt * bm
    i32 = jnp.int32

    r = route.reshape(t, LANES)
    e1, e2 = r[:, 0].astype(i32), r[:, 1].astype(i32)
    g1, g2 = r[:, 2], r[:, 3]
    tok = jnp.arange(t, dtype=i32)
    active = jnp.ones((t,), bool) if ctx_active else (tok % ta) >= n_ctx

    eid = jnp.arange(ne, dtype=i32)[:, None]
    member = ((e1[None] == eid) | (e2[None] == eid)) & active[None]
    csum = jnp.cumsum(member.astype(i32), axis=1)
    cap = (csum[:, -1] + bm - 1) // bm * bm
    end = jnp.cumsum(cap)
    start = end - cap
    n_used = (end[-1] // bm).astype(i32)

    def slot_of(e_sel):
        rank = jnp.take_along_axis(csum, e_sel[None], axis=0)[0] - 1
        return jnp.where(active, start[e_sel] + rank, -1)

    pos1, pos2 = slot_of(e1), slot_of(e2)
    drop1, drop2 = jnp.where(pos1 >= 0, pos1, ns), jnp.where(pos2 >= 0, pos2, ns)
    src = jnp.full((ns,), -1, i32).at[drop1].set(tok, mode="drop").at[drop2].set(tok, mode="drop")
    gslot = jnp.zeros((ns,), F32).at[drop1].set(g1, mode="drop").at[drop2].set(g2, mode="drop")
    tile_ids = jnp.arange(nt, dtype=i32)
    tile_expert = jnp.minimum(jnp.sum(end[None] <= (tile_ids * bm)[:, None], axis=1), ne - 1).astype(i32)

    src_t = src.reshape(nt, bm)
    c_lo = src_t[:, 0] // gch
    c_hi = jnp.max(src_t, axis=1) // gch
    n_ch = jnp.where(tile_ids < n_used, c_hi - c_lo + 1, 0)
    ni_g = ne * (t // gch) + nt
    g_tile, g_local, g_valid = _expand_items(n_ch, ni_g)
    g_chunk = c_lo[g_tile] + g_local
    g_flag = g_valid * (_VALID + _FIRST * (g_local == 0) + _LAST * (g_local == n_ch[g_tile] - 1))

    xs = pl.pallas_call(
        _moe_gather_kernel,
        grid_spec=pltpu.PrefetchScalarGridSpec(
            num_scalar_prefetch=3,
            grid=(ni_g,),
            in_specs=[
                pl.BlockSpec((bm, 1), lambda k, tl, cn, fl: (tl[k], 0)),
                pl.BlockSpec((gch, d), lambda k, tl, cn, fl: (cn[k], 0)),
            ],
            out_specs=pl.BlockSpec((bm, d), lambda k, tl, cn, fl: (tl[k], 0)),
            scratch_shapes=[pltpu.VMEM((bm, d), F32)],
        ),
        out_shape=jax.ShapeDtypeStruct((ns, d), BF16),
        compiler_params=_cparams("arbitrary"),
        name="moe_gather",
    )(g_tile, g_chunk, g_flag.astype(i32), src.reshape(ns, 1), h.reshape(t, d))

    def last_used(i, nu):
        return jnp.minimum(i, nu[0] - 1)

    def f_eff(i, f, nu):
        return jnp.where(i < nu[0], f, nf - 1)

    y = pl.pallas_call(
        _moe_ffn_kernel,
        grid_spec=pltpu.PrefetchScalarGridSpec(
            num_scalar_prefetch=2,
            grid=(nt, nf),
            in_specs=[
                pl.BlockSpec((bm, d), lambda i, f, te, nu: (last_used(i, nu), 0)),
                pl.BlockSpec((None, d, fk), lambda i, f, te, nu: (te[i], 0, f_eff(i, f, nu))),
                pl.BlockSpec((None, d, fk), lambda i, f, te, nu: (te[i], 0, nf + f_eff(i, f, nu))),
                pl.BlockSpec((None, fk, d), lambda i, f, te, nu: (te[i], f_eff(i, f, nu), 0)),
                pl.BlockSpec((bm, 1), lambda i, f, te, nu: (last_used(i, nu), 0)),
            ],
            out_specs=pl.BlockSpec((bm, d), lambda i, f, te, nu: (last_used(i, nu), 0)),
            scratch_shapes=[pltpu.VMEM((bm, d), F32)],
        ),
        out_shape=jax.ShapeDtypeStruct((ns, d), BF16),
        compiler_params=_cparams("arbitrary", "arbitrary"),
        name="moe_ffn",
    )(tile_expert, n_used.reshape(1), xs, w13, w13, w2, gslot.reshape(ns, 1))

    nc = t // cch
    cs0 = jnp.concatenate([jnp.zeros((ne, 1), i32), csum], axis=1)[:, ::cch]
    before, through = cs0[:, :-1].T, cs0[:, 1:].T
    t_lo = (start[None] + before) // bm
    t_hi = (start[None] + through - 1) // bm
    n_tl = jnp.where(through > before, t_hi - t_lo + 1, 0).reshape(-1)
    ni_c = ne * nc + nt
    c_pair, c_local, c_valid = _expand_items(n_tl, ni_c)
    c_chunk = c_pair // ne
    c_tile = t_lo.reshape(-1)[c_pair] + c_local
    off = jnp.cumsum(n_tl)
    k_eff = jnp.minimum(jnp.arange(ni_c, dtype=i32), off[-1] - 1)
    chunk_first = (off - n_tl)[c_chunk * ne]
    chunk_last = off[c_chunk * ne + ne - 1] - 1
    c_flag = c_valid * (_VALID + _FIRST * (k_eff == chunk_first) + _LAST * (k_eff == chunk_last))

    cpb = ta // cch
    kern = functools.partial(_moe_combine_kernel, n_ctx=n_ctx, chunks_per_batch=cpb)
    out = pl.pallas_call(
        kern,
        grid_spec=pltpu.PrefetchScalarGridSpec(
            num_scalar_prefetch=3,
            grid=(ni_c,),
            in_specs=[
                pl.BlockSpec((cch, 1), lambda k, cn, tl, fl: (cn[k], 0)),
                pl.BlockSpec((cch, 1), lambda k, cn, tl, fl: (cn[k], 0)),
                pl.BlockSpec((bm, d), lambda k, cn, tl, fl: (tl[k], 0)),
                pl.BlockSpec((cch, d), lambda k, cn, tl, fl: (cn[k], 0)),
                pl.BlockSpec((None, None, None, 1, d), lambda k, cn, tl, fl: (cn[k] // cpb, 0, k_gate, 0, 0)),
                pl.BlockSpec((None, None, None, 1, d), lambda k, cn, tl, fl: (cn[k] // cpb, 1, k_gate, 0, 0)),
            ],
            out_specs=pl.BlockSpec((cch, d), lambda k, cn, tl, fl: (cn[k], 0)),
            scratch_shapes=[pltpu.VMEM((cch, d), F32)],
        ),
        out_shape=jax.ShapeDtypeStruct((t, d), F32),
        input_output_aliases={6: 0},
        compiler_params=_cparams("arbitrary"),
        name="moe_combine",
    )(c_chunk, c_tile, c_flag.astype(i32), pos1.reshape(t, 1), pos2.reshape(t, 1), y, xa.reshape(t, d), mods, mods)
    return out.reshape(b, ta, d)


def _rope_tables(seq, n_ctx, rot_dim):
    rows = seq // GRID_W
    quarter = rot_dim // 4
    inv_freq = ROPE_THETA ** (-jnp.arange(quarter, dtype=F32) / quarter)
    row = jnp.repeat(jnp.arange(rows, dtype=F32), GRID_W)
    col = jnp.tile(jnp.arange(GRID_W, dtype=F32), rows)
    ang = jnp.concatenate([row[:, None] * inv_freq, col[:, None] * inv_freq], axis=-1)
    cos, sin = jnp.cos(ang), jnp.sin(ang)
    reps = LANES // rot_dim
    c = jnp.tile(jnp.concatenate([cos, cos], axis=-1), (1, reps))
    s = jnp.tile(jnp.concatenate([-sin, sin], axis=-1), (1, reps))
    c = jnp.concatenate([jnp.ones((n_ctx, LANES), F32), c], axis=0)
    s = jnp.concatenate([jnp.zeros((n_ctx, LANES), F32), s], axis=0)
    return c, s


def kernel(x, c, ctx, c_ctx, ada_w, ada_b, norm_mix, norm_ffn, gqa_wqkv, gqa_q_gain, gqa_k_gain, gqa_wo, mla_wdown, mla_qa_gain, mla_kva_gain, mla_wuq, mla_wukv, mla_q_gain, mla_k_gain, mla_wo, win_wqkv, win_q_gain, win_k_gain, win_sink, win_wo, diff_wqkv, diff_q_gain, diff_k_gain, diff_lambda, diff_subln, diff_wo, ffn_w13, ffn_w2, moe_router, moe_w13, moe_w2):
    b, seq, d = x.shape
    n_ctx = ctx.shape[1]
    depth = ada_w.shape[0]
    ones = lambda n: jnp.ones((n,), F32)
    zeros = lambda n: jnp.zeros((n,), F32)

    crows = -(-(b + 1) // 8) * 8
    cond = jnp.zeros((crows, d), F32).at[:b].set(c).at[b].set(c_ctx)
    mods_all = adaln(cond, ada_w, ada_b)

    xa = jnp.concatenate([ctx, x], axis=1)

    for i in range(depth):
        need_ctx = i < depth - 1
        j = i // 4
        kind = i % 4
        ml = mods_all[i, :b].reshape(b, 6, 1, d)
        mc = jnp.broadcast_to(mods_all[i, b].reshape(1, 6, 1, d), (b, 6, 1, d))
        mods = jnp.stack([mc, ml], axis=1)

        h = norm_mod(xa, norm_mix[i], mods, 0, 1, n_ctx)
        if kind == 0:
            hd = GQA_HEAD_DIM
            scale = hd ** -0.5
            nq, nk = GQA_HEADS * hd, GQA_KV_HEADS * hd
            groups = [(g * hd, hd) for g in range(GQA_HEADS + GQA_KV_HEADS)]
            gain = jnp.concatenate([jnp.tile(gqa_q_gain[j] * scale, GQA_HEADS),
                                    jnp.tile(gqa_k_gain[j], GQA_KV_HEADS), ones(nk)])
            qkv = proj(h, 0, gqa_wqkv[j].astype(BF16), groups, gain, _rope_tables(seq, n_ctx, hd),
                       range((nq + nk) // LANES), hd // 2, "gqa_qkv")
            o = gqa_attention(qkv, n_ctx)
            w_o = gqa_wo[j]
        elif kind == 1:
            scale = (MLA_NOPE + MLA_ROPE) ** -0.5
            hh = MLA_HEADS
            wd = mla_wdown[j]
            pad = jnp.zeros((d, LANES - MLA_ROPE), F32)
            wd2 = jnp.concatenate([wd[:, :MLA_Q_LORA], wd[:, MLA_Q_LORA + MLA_KV_LORA:], pad,
                                   wd[:, MLA_Q_LORA:MLA_Q_LORA + MLA_KV_LORA]], axis=1)
            kpe0 = MLA_Q_LORA
            ckv0 = MLA_Q_LORA + LANES
            groups = [(0, MLA_Q_LORA), (kpe0, MLA_ROPE), (ckv0, MLA_KV_LORA)]
            gain = jnp.concatenate([mla_qa_gain[j], mla_k_gain[j][MLA_NOPE:], zeros(LANES - MLA_ROPE),
                                    mla_kva_gain[j]])
            rope = _rope_tables(seq, n_ctx, MLA_ROPE)
            dn = proj(h, 0, wd2.astype(BF16), groups, gain, rope, [kpe0 // LANES], MLA_ROPE // 2, "mla_down")
            wq = mla_wuq[j].reshape(MLA_Q_LORA, hh, MLA_NOPE + MLA_ROPE)
            wq_n = wq[:, :, :MLA_NOPE].reshape(MLA_Q_LORA, hh * MLA_NOPE)
            wq_p = jnp.pad(wq[:, :, MLA_NOPE:], ((0, 0), (0, 0), (0, LANES - MLA_ROPE))).reshape(MLA_Q_LORA, hh * LANES)
            wq2 = jnp.concatenate([wq_n, wq_p], axis=1)
            groups = ([(g * MLA_NOPE, MLA_NOPE) for g in range(hh)]
                      + [(hh * MLA_NOPE + g * LANES, MLA_ROPE) for g in range(hh)])
            qg = mla_q_gain[j] * scale
            gain = jnp.concatenate([jnp.tile(qg[:MLA_NOPE], hh),
                                    jnp.tile(jnp.concatenate([qg[MLA_NOPE:], zeros(LANES - MLA_ROPE)]), hh)])
            q = proj(dn, 0, wq2.astype(BF16), groups, gain, rope, range(hh, 2 * hh), MLA_ROPE // 2, "mla_q")
            groups = [(g * (MLA_NOPE + MLA_V), MLA_NOPE) for g in range(hh)]
            gain = jnp.tile(jnp.concatenate([mla_k_gain[j][:MLA_NOPE], ones(MLA_V)]), hh)
            kv = proj(dn, ckv0 // MLA_KV_LORA, mla_wukv[j].astype(BF16), groups, gain, rope, [], MLA_ROPE // 2, "mla_kv")
            o = mla_attention(q, kv, dn, kpe0 // LANES, n_ctx)
            w_o = mla_wo[j]
        elif kind == 2:
            hd = WIN_HEAD_DIM
            scale = hd ** -0.5
            n_pairs = WIN_HEADS // WIN_KV_HEADS
            nq = WIN_HEADS * hd
            perm = np.array([(kv * n_pairs + g) * hd + t for g in range(n_pairs)
                             for kv in range(WIN_KV_HEADS) for t in range(hd)])
            wq = win_wqkv[j]
            w2_ = jnp.concatenate([wq[:, perm], wq[:, nq:]], axis=1)
            groups = [(g * hd, hd) for g in range(WIN_HEADS + WIN_KV_HEADS)]
            gain = jnp.concatenate([jnp.tile(win_q_gain[j] * scale, WIN_HEADS),
                                    jnp.tile(win_k_gain[j], WIN_KV_HEADS), ones(WIN_KV_HEADS * hd)])
            qkv = proj(h, 0, w2_.astype(BF16), groups, gain, _rope_tables(seq, n_ctx, hd),
                       range(n_pairs + 1), hd // 2, "win_qkv")
            o = win_attention(qkv, win_sink[j], n_ctx)
            w_o = win_wo[j][perm, :]
        else:
            hd = DIFF_HEAD_DIM
            scale = hd ** -0.5
            lam_init = 0.8 - 0.6 * math.exp(-0.3 * i)
            nqk = 2 * DIFF_HEADS
            groups = [(g * hd, hd) for g in range(2 * nqk)]
            gain = jnp.concatenate([jnp.tile(diff_q_gain[j] * scale, nqk), jnp.tile(diff_k_gain[j], nqk),
                                    ones(DIFF_HEADS * 2 * hd)])
            qkv = proj(h, 0, diff_wqkv[j].astype(BF16), groups, gain, _rope_tables(seq, n_ctx, hd),
                       range(2 * DIFF_HEADS), hd // 2, "diff_qkv")
            o = diff_attention(qkv, diff_lambda[j], diff_subln[j], lam_init, n_ctx)
            w_o = diff_wo[j]
        xa = out_proj_residual(o, w_o.astype(BF16), xa, mods, 2, n_ctx)

        jj = i // 2
        if i % 2 == 0:
            h = norm_mod(xa, norm_ffn[i], mods, 3, 4, n_ctx)
            xa = ffn_residual(h, ffn_w13[jj].astype(BF16), ffn_w2[jj].astype(BF16), xa, mods, 5, n_ctx)
        else:
            h, route = norm_mod(xa, norm_ffn[i], mods, 3, 4, n_ctx, router=moe_router[jj])
            xa = moe_residual(h, route, moe_w13[jj].astype(BF16), moe_w2[jj].astype(BF16), xa, mods, 5,
                              n_ctx, need_ctx)
    return xa[:, n_ctx:]
```

```python
import functools
import math

import jax
import jax.numpy as jnp
import numpy as np
from jax import lax
from jax.experimental import pallas as pl
from jax.experimental.pallas import tpu as pltpu

GRID_W = 64
ROPE_THETA = 10000.0
EPS = 1e-6
NEG_INF = -1e30
LANES = 128
ROW_TILE = 256
FFN_ROW_TILE = 768
FF_CHUNK = 512
VMEM_LIMIT = 56 * 1024 * 1024

GQA_HEADS, GQA_KV_HEADS, GQA_HEAD_DIM = 8, 2, 128
MLA_HEADS, MLA_Q_LORA, MLA_KV_LORA, MLA_NOPE, MLA_ROPE, MLA_V = 8, 384, 256, 128, 64, 128
WIN_HEADS, WIN_KV_HEADS, WIN_HEAD_DIM, WINDOW = 16, 2, 64, 128
DIFF_HEADS, DIFF_HEAD_DIM = 8, 64
N_EXPERTS = 8

F32 = jnp.float32
BF16 = jnp.bfloat16


def _cparams(*sem):
    return pltpu.CompilerParams(dimension_semantics=sem, vmem_limit_bytes=VMEM_LIMIT)


def _dot(a, b):
    return jnp.dot(a, b, preferred_element_type=F32)


def _dot_nt(a, b):
    return lax.dot_general(a, b, (((1,), (1,)), ((), ())), preferred_element_type=F32)


def _split_bf16(v):
    hi = v.astype(BF16)
    lo = (v - hi.astype(F32)).astype(BF16)
    return hi, lo


def _adaln_kernel(c_ref, w_ref, b_ref, o_ref):
    c = c_ref[...]
    sc = c * (1.0 / (1.0 + jnp.exp(-c)))
    o_ref[0] = jnp.dot(sc, w_ref[0], preferred_element_type=F32,
                       precision=lax.Precision.HIGHEST) + b_ref[0]


def adaln(cond, ada_w, ada_b):
    depth, d, n = ada_w.shape
    rows = cond.shape[0]
    bn = 1536
    return pl.pallas_call(
        _adaln_kernel,
        grid=(depth, n // bn),
        in_specs=[
            pl.BlockSpec((rows, d), lambda l, j: (0, 0)),
            pl.BlockSpec((1, d, bn), lambda l, j: (l, 0, j)),
            pl.BlockSpec((1, 1, bn), lambda l, j: (l, 0, j)),
        ],
        out_specs=pl.BlockSpec((1, rows, bn), lambda l, j: (l, 0, j)),
        out_shape=jax.ShapeDtypeStruct((depth, rows, n), F32),
        compiler_params=_cparams("parallel", "parallel"),
        name="adaln",
    )(cond, ada_w, ada_b.reshape(depth, 1, n))


def _mod_spec(k, n_ctx_tiles, d):
    return pl.BlockSpec((None, None, None, 1, d),
                        lambda b, i, *_: (b, jnp.minimum(i // n_ctx_tiles, 1), k, 0, 0))


def _norm_mod(x, g, shift, scale):
    ms = jnp.mean(x * x, axis=-1, keepdims=True)
    return x * lax.rsqrt(ms + EPS) * g * (1.0 + scale) + shift


def _norm_mod_kernel(x_ref, g_ref, sh_ref, sc_ref, h_ref):
    h_ref[0] = _norm_mod(x_ref[0], g_ref[...], sh_ref[...], sc_ref[...]).astype(BF16)


def _norm_mod_router_kernel(x_ref, g_ref, sh_ref, sc_ref, r_ref, h_ref, cw_ref):
    h = _norm_mod(x_ref[0], g_ref[...], sh_ref[...], sc_ref[...])
    h_ref[0] = h.astype(BF16)
    logits = jnp.dot(h, r_ref[...], preferred_element_type=F32, precision=lax.Precision.HIGHEST)
    lane = lax.broadcasted_iota(jnp.int32, logits.shape, 1)
    logits = jnp.where(lane < N_EXPERTS, logits, -jnp.inf)
    m1 = jnp.max(logits, axis=-1, keepdims=True)
    i1 = jnp.min(jnp.where(logits == m1, lane, LANES), axis=-1, keepdims=True)
    rest = jnp.where(lane == i1, -jnp.inf, logits)
    m2 = jnp.max(rest, axis=-1, keepdims=True)
    i2 = jnp.min(jnp.where(rest == m2, lane, LANES), axis=-1, keepdims=True)
    e2 = jnp.exp(m2 - m1)
    den = 1.0 + e2
    cw_ref[0] = jnp.where(lane == 0, i1.astype(F32), jnp.where(lane == 1, i2.astype(F32),
                          jnp.where(lane == 2, 1.0 / den, jnp.where(lane == 3, e2 / den, 0.0))))


def norm_mod(xa, gain, mods, k_shift, k_scale, n_ctx, router=None):
    b, ta, d = xa.shape
    bm = ROW_TILE
    nct = n_ctx // bm
    in_specs = [
        pl.BlockSpec((1, bm, d), lambda b_, i: (b_, i, 0)),
        pl.BlockSpec((1, d), lambda b_, i: (0, 0)),
        _mod_spec(k_shift, nct, d),
        _mod_spec(k_scale, nct, d),
    ]
    h_spec = pl.BlockSpec((1, bm, d), lambda b_, i: (b_, i, 0))
    h_shape = jax.ShapeDtypeStruct((b, ta, d), BF16)
    args = [xa, gain.reshape(1, d), mods, mods]
    if router is None:
        return pl.pallas_call(
            _norm_mod_kernel, grid=(b, ta // bm), in_specs=in_specs, out_specs=h_spec,
            out_shape=h_shape, compiler_params=_cparams("parallel", "parallel"), name="norm_mod",
        )(*args)
    rpad = jnp.zeros((d, LANES), F32).at[:, :N_EXPERTS].set(router)
    return pl.pallas_call(
        _norm_mod_router_kernel, grid=(b, ta // bm),
        in_specs=in_specs + [pl.BlockSpec((d, LANES), lambda b_, i: (0, 0))],
        out_specs=[h_spec, pl.BlockSpec((1, bm, LANES), lambda b_, i: (b_, i, 0))],
        out_shape=[h_shape, jax.ShapeDtypeStruct((b, ta, LANES), F32)],
        compiler_params=_cparams("parallel", "parallel"), name="norm_mod_router",
    )(*args, rpad)


def _proj_kernel(*refs, rope_blocks, rope_half, prenorm):
    if prenorm:
        x_ref, g_ref, sh_ref, sc_ref = refs[:4]
        h = _norm_mod(x_ref[0], g_ref[...], sh_ref[...], sc_ref[...]).astype(BF16)
        refs = refs[4:]
    else:
        h = refs[0][0]
        refs = refs[1:]
    w_ref, e_ref, et_ref, igs_ref, u_ref, gain_ref, c_ref, s_ref, o_ref = refs
    y = _dot(h, w_ref[...])
    s_hi, s_lo = _split_bf16(y * y)
    ss = _dot(s_hi, e_ref[...]) + _dot(s_lo, e_ref[...])
    inv = lax.rsqrt(ss * igs_ref[...] + EPS)
    i_hi, i_lo = _split_bf16(inv)
    fac = _dot(i_hi, et_ref[...]) + _dot(i_lo, et_ref[...]) + u_ref[...]
    z = y * (fac * gain_ref[...])
    n = z.shape[1]
    cos = c_ref[...]
    sin = s_ref[...]
    if rope_half == LANES // 2:
        first = None
    else:
        lane = lax.broadcasted_iota(jnp.int32, cos.shape, 1)
        first = (lane % (2 * rope_half)) < rope_half
    for blk in range(n // LANES):
        zb = z[:, blk * LANES:(blk + 1) * LANES]
        if blk in rope_blocks:
            if first is None:
                rot = pltpu.roll(zb, LANES // 2, 1)
            else:
                rot = jnp.where(first, pltpu.roll(zb, LANES - rope_half, 1), pltpu.roll(zb, rope_half, 1))
            zb = zb * cos + rot * sin
        o_ref[0, :, blk * LANES:(blk + 1) * LANES] = zb.astype(BF16)


def proj(h, kin_block, w, groups, gain, rope_tabs, rope_blocks, rope_half, name, prenorm=None):
    b, ta, _ = h.shape
    k, n = w.shape
    bm = ROW_TILE
    e = np.zeros((n, LANES), np.float32)
    igs = np.zeros((1, LANES), np.float32)
    u = np.ones((1, n), np.float32)
    for gi, (start, size) in enumerate(groups):
        e[start:start + size, gi] = 1.0
        igs[0, gi] = 1.0 / size
        u[0, start:start + size] = 0.0
    cos, sin = rope_tabs
    kern = functools.partial(_proj_kernel, rope_blocks=frozenset(rope_blocks), rope_half=rope_half,
                             prenorm=prenorm is not None)
    lead_specs = [pl.BlockSpec((1, bm, k), lambda b_, i: (b_, i, kin_block))]
    lead_args = [h]
    if prenorm is not None:
        ngain, mods, k_shift, k_scale, n_ctx = prenorm
        nct = n_ctx // bm
        lead_specs += [pl.BlockSpec((1, k), lambda b_, i: (0, 0)), _mod_spec(k_shift, nct, k),
                       _mod_spec(k_scale, nct, k)]
        lead_args += [ngain.reshape(1, k), mods, mods]
    return pl.pallas_call(
        kern,
        grid=(b, ta // bm),
        in_specs=lead_specs + [
            pl.BlockSpec((k, n), lambda b_, i: (0, 0)),
            pl.BlockSpec((n, LANES), lambda b_, i: (0, 0)),
            pl.BlockSpec((LANES, n), lambda b_, i: (0, 0)),
            pl.BlockSpec((1, LANES), lambda b_, i: (0, 0)),
            pl.BlockSpec((1, n), lambda b_, i: (0, 0)),
            pl.BlockSpec((1, n), lambda b_, i: (0, 0)),
            pl.BlockSpec((bm, LANES), lambda b_, i: (i, 0)),
            pl.BlockSpec((bm, LANES), lambda b_, i: (i, 0)),
        ],
        out_specs=pl.BlockSpec((1, bm, n), lambda b_, i: (b_, i, 0)),
        out_shape=jax.ShapeDtypeStruct((b, ta, n), BF16),
        compiler_params=_cparams("parallel", "parallel"),
        name=name,
    )(*lead_args, w, jnp.asarray(e, BF16), jnp.asarray(e.T, BF16), jnp.asarray(igs), jnp.asarray(u),
      gain.reshape(1, n).astype(F32), cos, sin)


LOG2E = math.log2(math.e)


def _softmax_pv(s, v):
    m = jnp.max(s, axis=-1, keepdims=True)
    p = jnp.exp2(s - m)
    l = jnp.sum(p, axis=-1, keepdims=True)
    return _dot(p.astype(BF16), v) / l


def _gqa_attn_kernel(q_ref, k_ref, v_ref, o_ref, *, n_ctx, group, hd):
    i = pl.program_id(2)

    def attend(nk):
        k = k_ref[0, :nk, :]
        v = v_ref[0, :nk, :]
        for g in range(group):
            q = q_ref[0, :, g * hd:(g + 1) * hd]
            o = _softmax_pv(_dot_nt(q, k), v)
            o_ref[0, :, g * hd:(g + 1) * hd] = o.astype(BF16)

    nct = n_ctx // ROW_TILE

    @pl.when(i < nct)
    def _():
        attend(n_ctx)

    @pl.when(i >= nct)
    def _():
        attend(k_ref.shape[1])


def gqa_attention(qkv, n_ctx):
    b, ta, _ = qkv.shape
    hd, group, kvh = GQA_HEAD_DIM, GQA_HEADS // GQA_KV_HEADS, GQA_KV_HEADS
    bq = ROW_TILE
    qw = group * hd
    kb = GQA_HEADS
    vb = GQA_HEADS + kvh
    kern = functools.partial(_gqa_attn_kernel, n_ctx=n_ctx, group=group, hd=hd)
    return pl.pallas_call(
        kern,
        grid=(b, kvh, ta // bq),
        in_specs=[
            pl.BlockSpec((1, bq, qw), lambda b_, h, i: (b_, i, h)),
            pl.BlockSpec((1, ta, hd), lambda b_, h, i: (b_, 0, kb + h)),
            pl.BlockSpec((1, ta, hd), lambda b_, h, i: (b_, 0, vb + h)),
        ],
        out_specs=pl.BlockSpec((1, bq, qw), lambda b_, h, i: (b_, i, h)),
        out_shape=jax.ShapeDtypeStruct((b, ta, GQA_HEADS * hd), BF16),
        compiler_params=_cparams("parallel", "parallel", "arbitrary"),
        name="gqa_attn",
    )(qkv, qkv, qkv)


def _mla_attn_kernel(qn_ref, qp_ref, kn_ref, kp_ref, v_ref, o_ref, *, n_ctx):
    i = pl.program_id(2)

    def attend(nk):
        s = _dot_nt(qn_ref[0], kn_ref[0, :nk, :]) + _dot_nt(qp_ref[0], kp_ref[0, :nk, :])
        o_ref[0] = _softmax_pv(s, v_ref[0, :nk, :]).astype(BF16)

    nct = n_ctx // ROW_TILE

    @pl.when(i < nct)
    def _():
        attend(n_ctx)

    @pl.when(i >= nct)
    def _():
        attend(kn_ref.shape[1])


def mla_attention(q, kv, dn, kpe_block, n_ctx):
    b, ta, _ = q.shape
    bq = ROW_TILE
    hh = MLA_HEADS
    kern = functools.partial(_mla_attn_kernel, n_ctx=n_ctx)
    return pl.pallas_call(
        kern,
        grid=(b, hh, ta // bq),
        in_specs=[
            pl.BlockSpec((1, bq, LANES), lambda b_, h, i: (b_, i, h)),
            pl.BlockSpec((1, bq, LANES), lambda b_, h, i: (b_, i, hh + h)),
            pl.BlockSpec((1, ta, LANES), lambda b_, h, i: (b_, 0, 2 * h)),
            pl.BlockSpec((1, ta, LANES), lambda b_, h, i: (b_, 0, kpe_block)),
            pl.BlockSpec((1, ta, LANES), lambda b_, h, i: (b_, 0, 2 * h + 1)),
        ],
        out_specs=pl.BlockSpec((1, bq, LANES), lambda b_, h, i: (b_, i, h)),
        out_shape=jax.ShapeDtypeStruct((b, ta, hh * MLA_V), BF16),
        compiler_params=_cparams("parallel", "parallel", "arbitrary"),
        name="mla_attn",
    )(q, q, kv, dn, kv)


def _half_masks(shape):
    lane = lax.broadcasted_iota(jnp.int32, shape, 1)
    return lane < (LANES // 2)


def _diff_attn_kernel(q_ref, k_ref, v_ref, lam_ref, sub_ref, o_ref, *, n_ctx, lam_init):
    i = pl.program_id(2)
    lp = lam_ref[...]
    lam = (jnp.exp(jnp.sum(lp[0:1] * lp[1:2], axis=-1, keepdims=True))
           - jnp.exp(jnp.sum(lp[2:3] * lp[3:4], axis=-1, keepdims=True)) + lam_init)

    def attend(nk):
        q = q_ref[0]
        lo = _half_masks(q.shape)
        zero = jnp.zeros_like(q)
        k = k_ref[0, :nk, :]
        v = v_ref[0, :nk, :]
        o0 = _softmax_pv(_dot_nt(jnp.where(lo, q, zero), k), v)
        o1 = _softmax_pv(_dot_nt(jnp.where(lo, zero, q), k), v)
        o = o0 - lam * o1
        ms = jnp.mean(o * o, axis=-1, keepdims=True)
        o_ref[0] = (o * lax.rsqrt(ms + EPS) * sub_ref[...] * (1.0 - lam_init)).astype(BF16)

    nct = n_ctx // ROW_TILE

    @pl.when(i < nct)
    def _():
        attend(n_ctx)

    @pl.when(i >= nct)
    def _():
        attend(k_ref.shape[1])


def diff_attention(qkv, lam_p, subln, lam_init, n_ctx):
    b, ta, _ = qkv.shape
    bq = ROW_TILE
    hh = DIFF_HEADS
    kern = functools.partial(_diff_attn_kernel, n_ctx=n_ctx, lam_init=lam_init)
    return pl.pallas_call(
        kern,
        grid=(b, hh, ta // bq),
        in_specs=[
            pl.BlockSpec((1, bq, LANES), lambda b_, h, i: (b_, i, h)),
            pl.BlockSpec((1, ta, LANES), lambda b_, h, i: (b_, 0, hh + h)),
            pl.BlockSpec((1, ta, LANES), lambda b_, h, i: (b_, 0, 2 * hh + h)),
            pl.BlockSpec((4, DIFF_HEAD_DIM), lambda b_, h, i: (0, 0)),
            pl.BlockSpec((1, LANES), lambda b_, h, i: (0, 0)),
        ],
        out_specs=pl.BlockSpec((1, bq, LANES), lambda b_, h, i: (b_, i, h)),
        out_shape=jax.ShapeDtypeStruct((b, ta, hh * 2 * DIFF_HEAD_DIM), BF16),
        compiler_params=_cparams("parallel", "parallel", "arbitrary"),
        name="diff_attn",
    )(qkv, qkv, qkv, lam_p.astype(F32), subln.reshape(1, LANES).astype(F32))


def _win_attn_kernel(sink_ref, q_ref, k_ref, v_ref, o_ref, *, n_ctx, seq, n_pairs):
    i = pl.program_id(1)
    bq = q_ref.shape[1]
    band = bq + 2 * WINDOW
    nct = n_ctx // bq

    def head_out(qm, kv_idx, pair, blocks):
        sk = sink_ref[kv_idx * n_pairs + pair]
        ss = []
        m = None
        for k, _, mask in blocks:
            s = _dot_nt(qm, k)
            if mask is not None:
                s = jnp.where(mask, s, NEG_INF)
            ss.append(s)
            bm_ = jnp.max(s, axis=-1, keepdims=True)
            m = bm_ if m is None else jnp.maximum(m, bm_)
        m = jnp.maximum(m, sk)
        l = jnp.exp2(sk - m)
        o = None
        for s, (_, v, _) in zip(ss, blocks):
            p = jnp.exp2(s - m)
            l = l + jnp.sum(p, axis=-1, keepdims=True)
            pv = _dot(p.astype(BF16), v)
            o = pv if o is None else o + pv
        return o / l

    def run(blocks):
        for pair in range(n_pairs):
            q = q_ref[0, :, pair * LANES:(pair + 1) * LANES]
            lo = _half_masks(q.shape)
            zero = jnp.zeros_like(q)
            o0 = head_out(jnp.where(lo, q, zero), 0, pair, blocks)
            o1 = head_out(jnp.where(lo, zero, q), 1, pair, blocks)
            o_ref[0, :, pair * LANES:(pair + 1) * LANES] = jnp.where(lo, o0, o1).astype(BF16)

    @pl.when(i < nct)
    def _():
        run([(k_ref[0, :n_ctx, :], v_ref[0, :n_ctx, :], None)])

    @pl.when(i >= nct)
    def _():
        q0 = (i - nct) * bq
        start = jnp.clip(q0 - WINDOW, 0, seq - band)
        start = pl.multiple_of(start, WINDOW)
        qpos = q0 + lax.broadcasted_iota(jnp.int32, (bq, band), 0)
        kpos = start + lax.broadcasted_iota(jnp.int32, (bq, band), 1)
        mask = jnp.abs(qpos - kpos) <= WINDOW
        kb = k_ref[0, pl.ds(n_ctx + start, band), :]
        vb = v_ref[0, pl.ds(n_ctx + start, band), :]
        run([(k_ref[0, :n_ctx, :], v_ref[0, :n_ctx, :], None), (kb, vb, mask)])


def win_attention(qkv, sink, n_ctx):
    b, ta, _ = qkv.shape
    bq = ROW_TILE
    n_pairs = WIN_HEADS // WIN_KV_HEADS
    qw = n_pairs * LANES
    kern = functools.partial(_win_attn_kernel, n_ctx=n_ctx, seq=ta - n_ctx, n_pairs=n_pairs)
    return pl.pallas_call(
        kern,
        grid_spec=pltpu.PrefetchScalarGridSpec(
            num_scalar_prefetch=1,
            grid=(b, ta // bq),
            in_specs=[
                pl.BlockSpec((1, bq, qw), lambda b_, i, s: (b_, i, 0)),
                pl.BlockSpec((1, ta, LANES), lambda b_, i, s: (b_, 0, n_pairs)),
                pl.BlockSpec((1, ta, LANES), lambda b_, i, s: (b_, 0, n_pairs + 1)),
            ],
            out_specs=pl.BlockSpec((1, bq, qw), lambda b_, i, s: (b_, i, 0)),
        ),
        out_shape=jax.ShapeDtypeStruct((b, ta, qw), BF16),
        compiler_params=_cparams("parallel", "arbitrary"),
        name="win_attn",
    )(sink.astype(F32), qkv, qkv, qkv)


def _out_proj_kernel(o_ref, w_ref, x_ref, g_ref, y_ref):
    y_ref[0] = x_ref[0] + g_ref[...] * _dot(o_ref[0], w_ref[...])


def out_proj_residual(o, w, xa, mods, k_gate, n_ctx):
    b, ta, d = xa.shape
    kdim = o.shape[2]
    bm = ROW_TILE
    return pl.pallas_call(
        _out_proj_kernel,
        grid=(b, ta // bm),
        in_specs=[
            pl.BlockSpec((1, bm, kdim), lambda b_, i: (b_, i, 0)),
            pl.BlockSpec((kdim, d), lambda b_, i: (0, 0)),
            pl.BlockSpec((1, bm, d), lambda b_, i: (b_, i, 0)),
            _mod_spec(k_gate, n_ctx // bm, d),
        ],
        out_specs=pl.BlockSpec((1, bm, d), lambda b_, i: (b_, i, 0)),
        out_shape=jax.ShapeDtypeStruct(xa.shape, F32),
        input_output_aliases={2: 0},
        compiler_params=_cparams("parallel", "parallel"),
        name="out_proj",
    )(o, w, xa, mods)


def _ffn_kernel(ng_ref, wg_ref, wu_ref, w2_ref, x_ref, mc_ref, ml_ref, y_ref, acc_ref, h_ref, *,
                n_ctx, k_shift, k_scale, k_gate):
    i = pl.program_id(1)
    f = pl.program_id(2)
    bm = acc_ref.shape[0]

    def row_mod(k):
        row = i * bm + lax.broadcasted_iota(jnp.int32, (bm, 1), 0)
        return jnp.where(row < n_ctx, mc_ref[k], ml_ref[k])

    @pl.when(f == 0)
    def _():
        acc_ref[...] = jnp.zeros_like(acc_ref)
        h_ref[...] = _norm_mod(x_ref[0], ng_ref[...], row_mod(k_shift), row_mod(k_scale)).astype(BF16)

    h = h_ref[...]
    g = _dot(h, wg_ref[...])
    u = _dot(h, wu_ref[...])
    act = (g * (1.0 / (1.0 + jnp.exp(-g))) * u).astype(BF16)
    acc_ref[...] += _dot(act, w2_ref[...])

    @pl.when(f == pl.num_programs(2) - 1)
    def _():
        y_ref[0] = x_ref[0] + row_mod(k_gate) * acc_ref[...]


def ffn_residual(norm_gain, w13, w2, xa, mods, k_shift, k_scale, k_gate, n_ctx):
    b, ta, d = xa.shape
    ff = w2.shape[0]
    bm, fk = FFN_ROW_TILE, FF_CHUNK
    nf = ff // fk
    kern = functools.partial(_ffn_kernel, n_ctx=n_ctx, k_shift=k_shift, k_scale=k_scale, k_gate=k_gate)
    in_specs = [
        pl.BlockSpec((1, d), lambda b_, i, f: (0, 0)),
        pl.BlockSpec((d, fk), lambda b_, i, f: (0, f)),
        pl.BlockSpec((d, fk), lambda b_, i, f: (0, nf + f)),
        pl.BlockSpec((fk, d), lambda b_, i, f: (f, 0)),
        pl.BlockSpec((1, bm, d), lambda b_, i, f: (b_, i, 0)),
        pl.BlockSpec((None, None, 6, 1, d), lambda b_, i, f: (b_, 0, 0, 0, 0)),
        pl.BlockSpec((None, None, 6, 1, d), lambda b_, i, f: (b_, 1, 0, 0, 0)),
    ]
    return pl.pallas_call(
        kern,
        grid=(b, ta // bm, nf),
        in_specs=in_specs,
        out_specs=pl.BlockSpec((1, bm, d), lambda b_, i, f: (b_, i, 0)),
        out_shape=jax.ShapeDtypeStruct(xa.shape, F32),
        scratch_shapes=[pltpu.VMEM((bm, d), F32), pltpu.VMEM((bm, d), BF16)],
        input_output_aliases={4: 0},
        compiler_params=_cparams("parallel", "parallel", "arbitrary"),
        name="ffn",
    )(norm_gain.reshape(1, d), w13, w13, w2, xa, mods, mods)


MOE_SLOT_TILE = 512
MOE_GATHER_CHUNK = 512
MOE_COMBINE_CHUNK = 768
_VALID, _FIRST, _LAST = 1, 2, 4


def _one_hot_bf16(cond):
    return jnp.where(cond, 1.0, 0.0).astype(BF16)


def _moe_gather_kernel(tile_ref, chunk_ref, flag_ref, src_ref, h_ref, o_ref, acc_ref):
    k = pl.program_id(0)
    fl = flag_ref[k]

    @pl.when((fl & _FIRST) != 0)
    def _():
        acc_ref[...] = jnp.zeros_like(acc_ref)

    @pl.when((fl & _VALID) != 0)
    def _():
        bm, ch = acc_ref.shape[0], h_ref.shape[0]
        tok = chunk_ref[k] * ch + lax.broadcasted_iota(jnp.int32, (bm, ch), 1)
        acc_ref[...] += _dot(_one_hot_bf16(src_ref[...] == tok), h_ref[...])

    @pl.when((fl & _LAST) != 0)
    def _():
        o_ref[...] = acc_ref[...].astype(BF16)


def _moe_ffn_kernel(te_ref, nu_ref, xs_ref, wg_ref, wu_ref, w2_ref, gs_ref, y_ref, acc_ref):
    i = pl.program_id(0)
    f = pl.program_id(1)

    @pl.when(i < nu_ref[0])
    def _():
        @pl.when(f == 0)
        def _():
            acc_ref[...] = jnp.zeros_like(acc_ref)

        xs = xs_ref[...]
        g = _dot(xs, wg_ref[...])
        u = _dot(xs, wu_ref[...])
        act = (g * (1.0 / (1.0 + jnp.exp(-g))) * u).astype(BF16)
        acc_ref[...] += _dot(act, w2_ref[...])

        @pl.when(f == pl.num_programs(1) - 1)
        def _():
            y_ref[...] = (acc_ref[...] * gs_ref[...]).astype(BF16)


def _moe_combine_kernel(chunk_ref, tile_ref, flag_ref, p1_ref, p2_ref, y_ref, x_ref, gc_ref, gl_ref,
                        o_ref, acc_ref, *, n_ctx, chunks_per_batch):
    k = pl.program_id(0)
    fl = flag_ref[k]
    ch, bm = acc_ref.shape[0], y_ref.shape[0]

    @pl.when((fl & _FIRST) != 0)
    def _():
        acc_ref[...] = jnp.zeros_like(acc_ref)

    @pl.when((fl & _VALID) != 0)
    def _():
        slot = tile_ref[k] * bm + lax.broadcasted_iota(jnp.int32, (ch, bm), 1)
        w = _one_hot_bf16((p1_ref[...] == slot) | (p2_ref[...] == slot))
        acc_ref[...] += _dot(w, y_ref[...])

    @pl.when((fl & _LAST) != 0)
    def _():
        row = (chunk_ref[k] % chunks_per_batch) * ch + lax.broadcasted_iota(jnp.int32, (ch, 1), 0)
        gate = jnp.where(row < n_ctx, gc_ref[...], gl_ref[...])
        o_ref[...] = x_ref[...] + gate * acc_ref[...]


def _expand_items(counts, n_items):
    off = jnp.cumsum(counts)
    total = off[-1]
    k = jnp.minimum(jnp.arange(n_items, dtype=jnp.int32), total - 1)
    grp = jnp.minimum(jnp.searchsorted(off, k, side="right").astype(jnp.int32), counts.shape[0] - 1)
    local = k - (off[grp] - counts[grp])
    valid = jnp.arange(n_items, dtype=jnp.int32) < total
    return grp, local, valid


def moe_residual(h, route, w13, w2, xa, mods, k_gate, n_ctx, ctx_active):
    b, ta, d = xa.shape
    t = b * ta
    ne, ff = w2.shape[0], w2.shape[1]
    bm, gch, cch, fk = MOE_SLOT_TILE, MOE_GATHER_CHUNK, MOE_COMBINE_CHUNK, FF_CHUNK
    nf = ff // fk
    nt = (2 * t) // bm + ne
    ns = nt * bm
    i32 = jnp.int32

    r = route.reshape(t, LANES)
    e1, e2 = r[:, 0].astype(i32), r[:, 1].astype(i32)
    g1, g2 = r[:, 2], r[:, 3]
    tok = jnp.arange(t, dtype=i32)
    active = jnp.ones((t,), bool) if ctx_active else (tok % ta) >= n_ctx

    eid = jnp.arange(ne, dtype=i32)[:, None]
    member = ((e1[None] == eid) | (e2[None] == eid)) & active[None]
    csum = jnp.cumsum(member.astype(i32), axis=1)
    cap = (csum[:, -1] + bm - 1) // bm * bm
    end = jnp.cumsum(cap)
    start = end - cap
    n_used = (end[-1] // bm).astype(i32)

    def slot_of(e_sel):
        rank = jnp.take_along_axis(csum, e_sel[None], axis=0)[0] - 1
        return jnp.where(active, start[e_sel] + rank, -1)

    pos1, pos2 = slot_of(e1), slot_of(e2)
    drop1, drop2 = jnp.where(pos1 >= 0, pos1, ns), jnp.where(pos2 >= 0, pos2, ns)
    src = jnp.full((ns,), -1, i32).at[drop1].set(tok, mode="drop").at[drop2].set(tok, mode="drop")
    gslot = jnp.zeros((ns,), F32).at[drop1].set(g1, mode="drop").at[drop2].set(g2, mode="drop")
    tile_ids = jnp.arange(nt, dtype=i32)
    tile_expert = jnp.minimum(jnp.sum(end[None] <= (tile_ids * bm)[:, None], axis=1), ne - 1).astype(i32)

    src_t = src.reshape(nt, bm)
    c_lo = src_t[:, 0] // gch
    c_hi = jnp.max(src_t, axis=1) // gch
    n_ch = jnp.where(tile_ids < n_used, c_hi - c_lo + 1, 0)
    ni_g = ne * (t // gch) + nt
    g_tile, g_local, g_valid = _expand_items(n_ch, ni_g)
    g_chunk = c_lo[g_tile] + g_local
    g_flag = g_valid * (_VALID + _FIRST * (g_local == 0) + _LAST * (g_local == n_ch[g_tile] - 1))

    xs = pl.pallas_call(
        _moe_gather_kernel,
        grid_spec=pltpu.PrefetchScalarGridSpec(
            num_scalar_prefetch=3,
            grid=(ni_g,),
            in_specs=[
                pl.BlockSpec((bm, 1), lambda k, tl, cn, fl: (tl[k], 0)),
                pl.BlockSpec((gch, d), lambda k, tl, cn, fl: (cn[k], 0)),
            ],
            out_specs=pl.BlockSpec((bm, d), lambda k, tl, cn, fl: (tl[k], 0)),
            scratch_shapes=[pltpu.VMEM((bm, d), F32)],
        ),
        out_shape=jax.ShapeDtypeStruct((ns, d), BF16),
        compiler_params=_cparams("arbitrary"),
        name="moe_gather",
    )(g_tile, g_chunk, g_flag.astype(i32), src.reshape(ns, 1), h.reshape(t, d))

    def last_used(i, nu):
        return jnp.minimum(i, nu[0] - 1)

    def f_eff(i, f, nu):
        return jnp.where(i < nu[0], f, nf - 1)

    y = pl.pallas_call(
        _moe_ffn_kernel,
        grid_spec=pltpu.PrefetchScalarGridSpec(
            num_scalar_prefetch=2,
            grid=(nt, nf),
            in_specs=[
                pl.BlockSpec((bm, d), lambda i, f, te, nu: (last_used(i, nu), 0)),
                pl.BlockSpec((None, d, fk), lambda i, f, te, nu: (te[i], 0, f_eff(i, f, nu))),
                pl.BlockSpec((None, d, fk), lambda i, f, te, nu: (te[i], 0, nf + f_eff(i, f, nu))),
                pl.BlockSpec((None, fk, d), lambda i, f, te, nu: (te[i], f_eff(i, f, nu), 0)),
                pl.BlockSpec((bm, 1), lambda i, f, te, nu: (last_used(i, nu), 0)),
            ],
            out_specs=pl.BlockSpec((bm, d), lambda i, f, te, nu: (last_used(i, nu), 0)),
            scratch_shapes=[pltpu.VMEM((bm, d), F32)],
        ),
        out_shape=jax.ShapeDtypeStruct((ns, d), BF16),
        compiler_params=_cparams("arbitrary", "arbitrary"),
        name="moe_ffn",
    )(tile_expert, n_used.reshape(1), xs, w13, w13, w2, gslot.reshape(ns, 1))

    nc = t // cch
    cs0 = jnp.concatenate([jnp.zeros((ne, 1), i32), csum], axis=1)[:, ::cch]
    before, through = cs0[:, :-1].T, cs0[:, 1:].T
    t_lo = (start[None] + before) // bm
    t_hi = (start[None] + through - 1) // bm
    n_tl = jnp.where(through > before, t_hi - t_lo + 1, 0).reshape(-1)
    ni_c = ne * nc + nt
    c_pair, c_local, c_valid = _expand_items(n_tl, ni_c)
    c_chunk = c_pair // ne
    c_tile = t_lo.reshape(-1)[c_pair] + c_local
    off = jnp.cumsum(n_tl)
    k_eff = jnp.minimum(jnp.arange(ni_c, dtype=i32), off[-1] - 1)
    chunk_first = (off - n_tl)[c_chunk * ne]
    chunk_last = off[c_chunk * ne + ne - 1] - 1
    c_flag = c_valid * (_VALID + _FIRST * (k_eff == chunk_first) + _LAST * (k_eff == chunk_last))

    cpb = ta // cch
    kern = functools.partial(_moe_combine_kernel, n_ctx=n_ctx, chunks_per_batch=cpb)
    out = pl.pallas_call(
        kern,
        grid_spec=pltpu.PrefetchScalarGridSpec(
            num_scalar_prefetch=3,
            grid=(ni_c,),
            in_specs=[
                pl.BlockSpec((cch, 1), lambda k, cn, tl, fl: (cn[k], 0)),
                pl.BlockSpec((cch, 1), lambda k, cn, tl, fl: (cn[k], 0)),
                pl.BlockSpec((bm, d), lambda k, cn, tl, fl: (tl[k], 0)),
                pl.BlockSpec((cch, d), lambda k, cn, tl, fl: (cn[k], 0)),
                pl.BlockSpec((None, None, None, 1, d), lambda k, cn, tl, fl: (cn[k] // cpb, 0, k_gate, 0, 0)),
                pl.BlockSpec((None, None, None, 1, d), lambda k, cn, tl, fl: (cn[k] // cpb, 1, k_gate, 0, 0)),
            ],
            out_specs=pl.BlockSpec((cch, d), lambda k, cn, tl, fl: (cn[k], 0)),
            scratch_shapes=[pltpu.VMEM((cch, d), F32)],
        ),
        out_shape=jax.ShapeDtypeStruct((t, d), F32),
        input_output_aliases={6: 0},
        compiler_params=_cparams("arbitrary"),
        name="moe_combine",
    )(c_chunk, c_tile, c_flag.astype(i32), pos1.reshape(t, 1), pos2.reshape(t, 1), y, xa.reshape(t, d), mods, mods)
    return out.reshape(b, ta, d)


def _rope_tables(seq, n_ctx, rot_dim):
    rows = seq // GRID_W
    quarter = rot_dim // 4
    inv_freq = ROPE_THETA ** (-jnp.arange(quarter, dtype=F32) / quarter)
    row = jnp.repeat(jnp.arange(rows, dtype=F32), GRID_W)
    col = jnp.tile(jnp.arange(GRID_W, dtype=F32), rows)
    ang = jnp.concatenate([row[:, None] * inv_freq, col[:, None] * inv_freq], axis=-1)
    cos, sin = jnp.cos(ang), jnp.sin(ang)
    reps = LANES // rot_dim
    c = jnp.tile(jnp.concatenate([cos, cos], axis=-1), (1, reps))
    s = jnp.tile(jnp.concatenate([-sin, sin], axis=-1), (1, reps))
    c = jnp.concatenate([jnp.ones((n_ctx, LANES), F32), c], axis=0)
    s = jnp.concatenate([jnp.zeros((n_ctx, LANES), F32), s], axis=0)
    return c, s


def kernel(x, c, ctx, c_ctx, ada_w, ada_b, norm_mix, norm_ffn, gqa_wqkv, gqa_q_gain, gqa_k_gain, gqa_wo, mla_wdown, mla_qa_gain, mla_kva_gain, mla_wuq, mla_wukv, mla_q_gain, mla_k_gain, mla_wo, win_wqkv, win_q_gain, win_k_gain, win_sink, win_wo, diff_wqkv, diff_q_gain, diff_k_gain, diff_lambda, diff_subln, diff_wo, ffn_w13, ffn_w2, moe_router, moe_w13, moe_w2):
    b, seq, d = x.shape
    n_ctx = ctx.shape[1]
    depth = ada_w.shape[0]
    ones = lambda n: jnp.ones((n,), F32)
    zeros = lambda n: jnp.zeros((n,), F32)

    crows = -(-(b + 1) // 8) * 8
    cond = jnp.zeros((crows, d), F32).at[:b].set(c).at[b].set(c_ctx)
    mods_all = adaln(cond, ada_w, ada_b)

    xa = jnp.concatenate([ctx, x], axis=1)

    for i in range(depth):
        need_ctx = i < depth - 1
        j = i // 4
        kind = i % 4
        ml = mods_all[i, :b].reshape(b, 6, 1, d)
        mc = jnp.broadcast_to(mods_all[i, b].reshape(1, 6, 1, d), (b, 6, 1, d))
        mods = jnp.stack([mc, ml], axis=1)

        prenorm = (norm_mix[i], mods, 0, 1, n_ctx)
        if kind == 0:
            hd = GQA_HEAD_DIM
            scale = hd ** -0.5 * LOG2E
            nq, nk = GQA_HEADS * hd, GQA_KV_HEADS * hd
            groups = [(g * hd, hd) for g in range(GQA_HEADS + GQA_KV_HEADS)]
            gain = jnp.concatenate([jnp.tile(gqa_q_gain[j] * scale, GQA_HEADS),
                                    jnp.tile(gqa_k_gain[j], GQA_KV_HEADS), ones(nk)])
            qkv = proj(xa, 0, gqa_wqkv[j].astype(BF16), groups, gain, _rope_tables(seq, n_ctx, hd),
                       range((nq + nk) // LANES), hd // 2, "gqa_qkv", prenorm)
            o = gqa_attention(qkv, n_ctx)
            w_o = gqa_wo[j]
        elif kind == 1:
            scale = (MLA_NOPE + MLA_ROPE) ** -0.5 * LOG2E
            hh = MLA_HEADS
            wd = mla_wdown[j]
            pad = jnp.zeros((d, LANES - MLA_ROPE), F32)
            wd2 = jnp.concatenate([wd[:, :MLA_Q_LORA], wd[:, MLA_Q_LORA + MLA_KV_LORA:], pad,
                                   wd[:, MLA_Q_LORA:MLA_Q_LORA + MLA_KV_LORA]], axis=1)
            kpe0 = MLA_Q_LORA
            ckv0 = MLA_Q_LORA + LANES
            groups = [(0, MLA_Q_LORA), (kpe0, MLA_ROPE), (ckv0, MLA_KV_LORA)]
            gain = jnp.concatenate([mla_qa_gain[j], mla_k_gain[j][MLA_NOPE:], zeros(LANES - MLA_ROPE),
                                    mla_kva_gain[j]])
            rope = _rope_tables(seq, n_ctx, MLA_ROPE)
            dn = proj(xa, 0, wd2.astype(BF16), groups, gain, rope, [kpe0 // LANES], MLA_ROPE // 2, "mla_down",
                      prenorm)
            wq = mla_wuq[j].reshape(MLA_Q_LORA, hh, MLA_NOPE + MLA_ROPE)
            wq_n = wq[:, :, :MLA_NOPE].reshape(MLA_Q_LORA, hh * MLA_NOPE)
            wq_p = jnp.pad(wq[:, :, MLA_NOPE:], ((0, 0), (0, 0), (0, LANES - MLA_ROPE))).reshape(MLA_Q_LORA, hh * LANES)
            wq2 = jnp.concatenate([wq_n, wq_p], axis=1)
            groups = ([(g * MLA_NOPE, MLA_NOPE) for g in range(hh)]
                      + [(hh * MLA_NOPE + g * LANES, MLA_ROPE) for g in range(hh)])
            qg = mla_q_gain[j] * scale
            gain = jnp.concatenate([jnp.tile(qg[:MLA_NOPE], hh),
                                    jnp.tile(jnp.concatenate([qg[MLA_NOPE:], zeros(LANES - MLA_ROPE)]), hh)])
            q = proj(dn, 0, wq2.astype(BF16), groups, gain, rope, range(hh, 2 * hh), MLA_ROPE // 2, "mla_q")
            groups = [(g * (MLA_NOPE + MLA_V), MLA_NOPE) for g in range(hh)]
            gain = jnp.tile(jnp.concatenate([mla_k_gain[j][:MLA_NOPE], ones(MLA_V)]), hh)
            kv = proj(dn, ckv0 // MLA_KV_LORA, mla_wukv[j].astype(BF16), groups, gain, rope, [], MLA_ROPE // 2, "mla_kv")
            o = mla_attention(q, kv, dn, kpe0 // LANES, n_ctx)
            w_o = mla_wo[j]
        elif kind == 2:
            hd = WIN_HEAD_DIM
            scale = hd ** -0.5 * LOG2E
            n_pairs = WIN_HEADS // WIN_KV_HEADS
            nq = WIN_HEADS * hd
            perm = np.array([(kv * n_pairs + g) * hd + t for g in range(n_pairs)
                             for kv in range(WIN_KV_HEADS) for t in range(hd)])
            wq = win_wqkv[j]
            w2_ = jnp.concatenate([wq[:, perm], wq[:, nq:]], axis=1)
            groups = [(g * hd, hd) for g in range(WIN_HEADS + WIN_KV_HEADS)]
            gain = jnp.concatenate([jnp.tile(win_q_gain[j] * scale, WIN_HEADS),
                                    jnp.tile(win_k_gain[j], WIN_KV_HEADS), ones(WIN_KV_HEADS * hd)])
            qkv = proj(xa, 0, w2_.astype(BF16), groups, gain, _rope_tables(seq, n_ctx, hd),
                       range(n_pairs + 1), hd // 2, "win_qkv", prenorm)
            o = win_attention(qkv, win_sink[j] * LOG2E, n_ctx)
            w_o = win_wo[j][perm, :]
        else:
            hd = DIFF_HEAD_DIM
            scale = hd ** -0.5 * LOG2E
            lam_init = 0.8 - 0.6 * math.exp(-0.3 * i)
            nqk = 2 * DIFF_HEADS
            groups = [(g * hd, hd) for g in range(2 * nqk)]
            gain = jnp.concatenate([jnp.tile(diff_q_gain[j] * scale, nqk), jnp.tile(diff_k_gain[j], nqk),
                                    ones(DIFF_HEADS * 2 * hd)])
            qkv = proj(xa, 0, diff_wqkv[j].astype(BF16), groups, gain, _rope_tables(seq, n_ctx, hd),
                       range(2 * DIFF_HEADS), hd // 2, "diff_qkv", prenorm)
            o = diff_attention(qkv, diff_lambda[j], diff_subln[j], lam_init, n_ctx)
            w_o = diff_wo[j]
        xa = out_proj_residual(o, w_o.astype(BF16), xa, mods, 2, n_ctx)

        jj = i // 2
        if i % 2 == 0:
            xa = ffn_residual(norm_ffn[i], ffn_w13[jj].astype(BF16), ffn_w2[jj].astype(BF16), xa, mods, 3, 4, 5,
                              n_ctx)
        else:
            h, route = norm_mod(xa, norm_ffn[i], mods, 3, 4, n_ctx, router=moe_router[jj])
            xa = moe_residual(h, route, moe_w13[jj].astype(BF16), moe_w2[jj].astype(BF16), xa, mods, 5,
                              n_ctx, need_ctx)
    return xa[:, n_ctx:]
```

```python
import functools
import math

import jax
import jax.numpy as jnp
import numpy as np
from jax import lax
from jax.experimental import pallas as pl
from jax.experimental.pallas import tpu as pltpu

GRID_W = 64
ROPE_THETA = 10000.0
EPS = 1e-6
NEG_INF = -1e30
LANES = 128
ROW_TILE = 256
FFN_ROW_TILE = 768
FF_CHUNK = 512
VMEM_LIMIT = 56 * 1024 * 1024

GQA_HEADS, GQA_KV_HEADS, GQA_HEAD_DIM = 8, 2, 128
MLA_HEADS, MLA_Q_LORA, MLA_KV_LORA, MLA_NOPE, MLA_ROPE, MLA_V = 8, 384, 256, 128, 64, 128
WIN_HEADS, WIN_KV_HEADS, WIN_HEAD_DIM, WINDOW = 16, 2, 64, 128
DIFF_HEADS, DIFF_HEAD_DIM = 8, 64
N_EXPERTS = 8

F32 = jnp.float32
BF16 = jnp.bfloat16


def _cparams(*sem):
    return pltpu.CompilerParams(dimension_semantics=sem, vmem_limit_bytes=VMEM_LIMIT)


def _dot(a, b):
    return jnp.dot(a, b, preferred_element_type=F32)


def _dot_nt(a, b):
    return lax.dot_general(a, b, (((1,), (1,)), ((), ())), preferred_element_type=F32)


def _split_bf16(v):
    hi = v.astype(BF16)
    lo = (v - hi.astype(F32)).astype(BF16)
    return hi, lo


def _adaln_kernel(c_ref, w_ref, b_ref, o_ref):
    c = c_ref[...]
    sc = c * (1.0 / (1.0 + jnp.exp(-c)))
    o_ref[0] = jnp.dot(sc, w_ref[0], preferred_element_type=F32,
                       precision=lax.Precision.HIGHEST) + b_ref[0]


def adaln(cond, ada_w, ada_b):
    depth, d, n = ada_w.shape
    rows = cond.shape[0]
    bn = 1536
    return pl.pallas_call(
        _adaln_kernel,
        grid=(depth, n // bn),
        in_specs=[
            pl.BlockSpec((rows, d), lambda l, j: (0, 0)),
            pl.BlockSpec((1, d, bn), lambda l, j: (l, 0, j)),
            pl.BlockSpec((1, 1, bn), lambda l, j: (l, 0, j)),
        ],
        out_specs=pl.BlockSpec((1, rows, bn), lambda l, j: (l, 0, j)),
        out_shape=jax.ShapeDtypeStruct((depth, rows, n), F32),
        compiler_params=_cparams("parallel", "parallel"),
        name="adaln",
    )(cond, ada_w, ada_b.reshape(depth, 1, n))


def _mod_spec(k, n_ctx_tiles, d):
    return pl.BlockSpec((None, None, None, 1, d),
                        lambda b, i, *_: (b, jnp.minimum(i // n_ctx_tiles, 1), k, 0, 0))


def _norm_mod(x, g, shift, scale):
    ms = jnp.mean(x * x, axis=-1, keepdims=True)
    return x * lax.rsqrt(ms + EPS) * g * (1.0 + scale) + shift


def _norm_mod_kernel(x_ref, g_ref, sh_ref, sc_ref, h_ref):
    h_ref[0] = _norm_mod(x_ref[0], g_ref[...], sh_ref[...], sc_ref[...]).astype(BF16)


def _norm_mod_router_kernel(x_ref, g_ref, sh_ref, sc_ref, r_ref, h_ref, cw_ref):
    h = _norm_mod(x_ref[0], g_ref[...], sh_ref[...], sc_ref[...])
    h_ref[0] = h.astype(BF16)
    logits = jnp.dot(h, r_ref[...], preferred_element_type=F32, precision=lax.Precision.HIGHEST)
    lane = lax.broadcasted_iota(jnp.int32, logits.shape, 1)
    logits = jnp.where(lane < N_EXPERTS, logits, -jnp.inf)
    m1 = jnp.max(logits, axis=-1, keepdims=True)
    i1 = jnp.min(jnp.where(logits == m1, lane, LANES), axis=-1, keepdims=True)
    rest = jnp.where(lane == i1, -jnp.inf, logits)
    m2 = jnp.max(rest, axis=-1, keepdims=True)
    i2 = jnp.min(jnp.where(rest == m2, lane, LANES), axis=-1, keepdims=True)
    e2 = jnp.exp(m2 - m1)
    den = 1.0 + e2
    cw_ref[0] = jnp.where(lane == 0, i1.astype(F32), jnp.where(lane == 1, i2.astype(F32),
                          jnp.where(lane == 2, 1.0 / den, jnp.where(lane == 3, e2 / den, 0.0))))


def norm_mod(xa, gain, mods, k_shift, k_scale, n_ctx, router=None):
    b, ta, d = xa.shape
    bm = ROW_TILE
    nct = n_ctx // bm
    in_specs = [
        pl.BlockSpec((1, bm, d), lambda b_, i: (b_, i, 0)),
        pl.BlockSpec((1, d), lambda b_, i: (0, 0)),
        _mod_spec(k_shift, nct, d),
        _mod_spec(k_scale, nct, d),
    ]
    h_spec = pl.BlockSpec((1, bm, d), lambda b_, i: (b_, i, 0))
    h_shape = jax.ShapeDtypeStruct((b, ta, d), BF16)
    args = [xa, gain.reshape(1, d), mods, mods]
    if router is None:
        return pl.pallas_call(
            _norm_mod_kernel, grid=(b, ta // bm), in_specs=in_specs, out_specs=h_spec,
            out_shape=h_shape, compiler_params=_cparams("parallel", "parallel"), name="norm_mod",
        )(*args)
    rpad = jnp.zeros((d, LANES), F32).at[:, :N_EXPERTS].set(router)
    return pl.pallas_call(
        _norm_mod_router_kernel, grid=(b, ta // bm),
        in_specs=in_specs + [pl.BlockSpec((d, LANES), lambda b_, i: (0, 0))],
        out_specs=[h_spec, pl.BlockSpec((1, bm, LANES), lambda b_, i: (b_, i, 0))],
        out_shape=[h_shape, jax.ShapeDtypeStruct((b, ta, LANES), F32)],
        compiler_params=_cparams("parallel", "parallel"), name="norm_mod_router",
    )(*args, rpad)


def _proj_kernel(*refs, rope_blocks, rope_half, prenorm):
    if prenorm:
        x_ref, g_ref, sh_ref, sc_ref = refs[:4]
        h = _norm_mod(x_ref[0], g_ref[...], sh_ref[...], sc_ref[...]).astype(BF16)
        refs = refs[4:]
    else:
        h = refs[0][0]
        refs = refs[1:]
    w_ref, e_ref, et_ref, igs_ref, u_ref, gain_ref, c_ref, s_ref, o_ref = refs
    y = _dot(h, w_ref[...])
    s_hi, s_lo = _split_bf16(y * y)
    ss = _dot(s_hi, e_ref[...]) + _dot(s_lo, e_ref[...])
    inv = lax.rsqrt(ss * igs_ref[...] + EPS)
    i_hi, i_lo = _split_bf16(inv)
    fac = _dot(i_hi, et_ref[...]) + _dot(i_lo, et_ref[...]) + u_ref[...]
    z = y * (fac * gain_ref[...])
    n = z.shape[1]
    cos = c_ref[...]
    sin = s_ref[...]
    if rope_half == LANES // 2:
        first = None
    else:
        lane = lax.broadcasted_iota(jnp.int32, cos.shape, 1)
        first = (lane % (2 * rope_half)) < rope_half
    for blk in range(n // LANES):
        zb = z[:, blk * LANES:(blk + 1) * LANES]
        if blk in rope_blocks:
            if first is None:
                rot = pltpu.roll(zb, LANES // 2, 1)
            else:
                rot = jnp.where(first, pltpu.roll(zb, LANES - rope_half, 1), pltpu.roll(zb, rope_half, 1))
            zb = zb * cos + rot * sin
        o_ref[0, :, blk * LANES:(blk + 1) * LANES] = zb.astype(BF16)


def proj(h, kin_block, w, groups, gain, rope_tabs, rope_blocks, rope_half, name, prenorm=None):
    b, ta, _ = h.shape
    k, n = w.shape
    bm = ROW_TILE
    e = np.zeros((n, LANES), np.float32)
    igs = np.zeros((1, LANES), np.float32)
    u = np.ones((1, n), np.float32)
    for gi, (start, size) in enumerate(groups):
        e[start:start + size, gi] = 1.0
        igs[0, gi] = 1.0 / size
        u[0, start:start + size] = 0.0
    cos, sin = rope_tabs
    kern = functools.partial(_proj_kernel, rope_blocks=frozenset(rope_blocks), rope_half=rope_half,
                             prenorm=prenorm is not None)
    lead_specs = [pl.BlockSpec((1, bm, k), lambda b_, i: (b_, i, kin_block))]
    lead_args = [h]
    if prenorm is not None:
        ngain, mods, k_shift, k_scale, n_ctx = prenorm
        nct = n_ctx // bm
        lead_specs += [pl.BlockSpec((1, k), lambda b_, i: (0, 0)), _mod_spec(k_shift, nct, k),
                       _mod_spec(k_scale, nct, k)]
        lead_args += [ngain.reshape(1, k), mods, mods]
    return pl.pallas_call(
        kern,
        grid=(b, ta // bm),
        in_specs=lead_specs + [
            pl.BlockSpec((k, n), lambda b_, i: (0, 0)),
            pl.BlockSpec((n, LANES), lambda b_, i: (0, 0)),
            pl.BlockSpec((LANES, n), lambda b_, i: (0, 0)),
            pl.BlockSpec((1, LANES), lambda b_, i: (0, 0)),
            pl.BlockSpec((1, n), lambda b_, i: (0, 0)),
            pl.BlockSpec((1, n), lambda b_, i: (0, 0)),
            pl.BlockSpec((bm, LANES), lambda b_, i: (i, 0)),
            pl.BlockSpec((bm, LANES), lambda b_, i: (i, 0)),
        ],
        out_specs=pl.BlockSpec((1, bm, n), lambda b_, i: (b_, i, 0)),
        out_shape=jax.ShapeDtypeStruct((b, ta, n), BF16),
        compiler_params=_cparams("parallel", "parallel"),
        name=name,
    )(*lead_args, w, jnp.asarray(e, BF16), jnp.asarray(e.T, BF16), jnp.asarray(igs), jnp.asarray(u),
      gain.reshape(1, n).astype(F32), cos, sin)


LOG2E = math.log2(math.e)


def _softmax_pv(s, v):
    m = jnp.max(s, axis=-1, keepdims=True)
    p = jnp.exp2(s - m)
    l = jnp.sum(p, axis=-1, keepdims=True)
    return _dot(p.astype(BF16), v) / l


def _gqa_attn_kernel(q_ref, k_ref, v_ref, o_ref, *, n_ctx, group, hd):
    i = pl.program_id(2)

    def attend(nk):
        k = k_ref[0, :nk, :]
        v = v_ref[0, :nk, :]
        for g in range(group):
            q = q_ref[0, :, g * hd:(g + 1) * hd]
            o = _softmax_pv(_dot_nt(q, k), v)
            o_ref[0, :, g * hd:(g + 1) * hd] = o.astype(BF16)

    nct = n_ctx // ROW_TILE

    @pl.when(i < nct)
    def _():
        attend(n_ctx)

    @pl.when(i >= nct)
    def _():
        attend(k_ref.shape[1])


def gqa_attention(qkv, n_ctx):
    b, ta, _ = qkv.shape
    hd, group, kvh = GQA_HEAD_DIM, GQA_HEADS // GQA_KV_HEADS, GQA_KV_HEADS
    bq = ROW_TILE
    qw = group * hd
    kb = GQA_HEADS
    vb = GQA_HEADS + kvh
    kern = functools.partial(_gqa_attn_kernel, n_ctx=n_ctx, group=group, hd=hd)
    return pl.pallas_call(
        kern,
        grid=(b, kvh, ta // bq),
        in_specs=[
            pl.BlockSpec((1, bq, qw), lambda b_, h, i: (b_, i, h)),
            pl.BlockSpec((1, ta, hd), lambda b_, h, i: (b_, 0, kb + h)),
            pl.BlockSpec((1, ta, hd), lambda b_, h, i: (b_, 0, vb + h)),
        ],
        out_specs=pl.BlockSpec((1, bq, qw), lambda b_, h, i: (b_, i, h)),
        out_shape=jax.ShapeDtypeStruct((b, ta, GQA_HEADS * hd), BF16),
        compiler_params=_cparams("parallel", "parallel", "arbitrary"),
        name="gqa_attn",
    )(qkv, qkv, qkv)


def _mla_attn_kernel(q_ref, kn_ref, kp_ref, v_ref, o_ref, kcat_ref, *, n_ctx):
    i = pl.program_id(2)

    @pl.when(i == 0)
    def _():
        kcat_ref[:, :LANES] = kn_ref[0]
        kcat_ref[:, LANES:] = kp_ref[0]

    def attend(nk):
        s = _dot_nt(q_ref[0], kcat_ref[:nk, :])
        o_ref[0] = _softmax_pv(s, v_ref[0, :nk, :]).astype(BF16)

    nct = n_ctx // ROW_TILE

    @pl.when(i < nct)
    def _():
        attend(n_ctx)

    @pl.when(i >= nct)
    def _():
        attend(kn_ref.shape[1])


def mla_attention(q, kv, dn, kpe_block, n_ctx):
    b, ta, _ = q.shape
    bq = ROW_TILE
    hh = MLA_HEADS
    kern = functools.partial(_mla_attn_kernel, n_ctx=n_ctx)
    return pl.pallas_call(
        kern,
        grid=(b, hh, ta // bq),
        in_specs=[
            pl.BlockSpec((1, bq, 2 * LANES), lambda b_, h, i: (b_, i, h)),
            pl.BlockSpec((1, ta, LANES), lambda b_, h, i: (b_, 0, 2 * h)),
            pl.BlockSpec((1, ta, LANES), lambda b_, h, i: (b_, 0, kpe_block)),
            pl.BlockSpec((1, ta, LANES), lambda b_, h, i: (b_, 0, 2 * h + 1)),
        ],
        out_specs=pl.BlockSpec((1, bq, LANES), lambda b_, h, i: (b_, i, h)),
        out_shape=jax.ShapeDtypeStruct((b, ta, hh * MLA_V), BF16),
        scratch_shapes=[pltpu.VMEM((ta, 2 * LANES), BF16)],
        compiler_params=_cparams("parallel", "parallel", "arbitrary"),
        name="mla_attn",
    )(q, kv, dn, kv)


def _half_masks(shape):
    lane = lax.broadcasted_iota(jnp.int32, shape, 1)
    return lane < (LANES // 2)


def _diff_attn_kernel(q_ref, k_ref, v_ref, lam_ref, sub_ref, o_ref, *, n_ctx, lam_init):
    i = pl.program_id(2)
    lp = lam_ref[...]
    lam = (jnp.exp(jnp.sum(lp[0:1] * lp[1:2], axis=-1, keepdims=True))
           - jnp.exp(jnp.sum(lp[2:3] * lp[3:4], axis=-1, keepdims=True)) + lam_init)

    def attend(nk):
        q = q_ref[0]
        lo = _half_masks(q.shape)
        zero = jnp.zeros_like(q)
        k = k_ref[0, :nk, :]
        v = v_ref[0, :nk, :]
        o0 = _softmax_pv(_dot_nt(jnp.where(lo, q, zero), k), v)
        o1 = _softmax_pv(_dot_nt(jnp.where(lo, zero, q), k), v)
        o = o0 - lam * o1
        ms = jnp.mean(o * o, axis=-1, keepdims=True)
        o_ref[0] = (o * lax.rsqrt(ms + EPS) * sub_ref[...] * (1.0 - lam_init)).astype(BF16)

    nct = n_ctx // ROW_TILE

    @pl.when(i < nct)
    def _():
        attend(n_ctx)

    @pl.when(i >= nct)
    def _():
        attend(k_ref.shape[1])


def diff_attention(qkv, lam_p, subln, lam_init, n_ctx):
    b, ta, _ = qkv.shape
    bq = ROW_TILE
    hh = DIFF_HEADS
    kern = functools.partial(_diff_attn_kernel, n_ctx=n_ctx, lam_init=lam_init)
    return pl.pallas_call(
        kern,
        grid=(b, hh, ta // bq),
        in_specs=[
            pl.BlockSpec((1, bq, LANES), lambda b_, h, i: (b_, i, h)),
            pl.BlockSpec((1, ta, LANES), lambda b_, h, i: (b_, 0, hh + h)),
            pl.BlockSpec((1, ta, LANES), lambda b_, h, i: (b_, 0, 2 * hh + h)),
            pl.BlockSpec((4, DIFF_HEAD_DIM), lambda b_, h, i: (0, 0)),
            pl.BlockSpec((1, LANES), lambda b_, h, i: (0, 0)),
        ],
        out_specs=pl.BlockSpec((1, bq, LANES), lambda b_, h, i: (b_, i, h)),
        out_shape=jax.ShapeDtypeStruct((b, ta, hh * 2 * DIFF_HEAD_DIM), BF16),
        compiler_params=_cparams("parallel", "parallel", "arbitrary"),
        name="diff_attn",
    )(qkv, qkv, qkv, lam_p.astype(F32), subln.reshape(1, LANES).astype(F32))


def _win_attn_kernel(sink_ref, q_ref, k_ref, v_ref, o_ref, *, n_ctx, seq, n_pairs):
    i = pl.program_id(1)
    bq = q_ref.shape[1]
    band = bq + 2 * WINDOW
    nct = n_ctx // bq

    def head_out(qm, kv_idx, pair, blocks):
        sk = sink_ref[kv_idx * n_pairs + pair]
        ss = []
        m = None
        for k, _, mask in blocks:
            s = _dot_nt(qm, k)
            if mask is not None:
                s = jnp.where(mask, s, NEG_INF)
            ss.append(s)
            bm_ = jnp.max(s, axis=-1, keepdims=True)
            m = bm_ if m is None else jnp.maximum(m, bm_)
        m = jnp.maximum(m, sk)
        l = jnp.exp2(sk - m)
        o = None
        for s, (_, v, _) in zip(ss, blocks):
            p = jnp.exp2(s - m)
            l = l + jnp.sum(p, axis=-1, keepdims=True)
            pv = _dot(p.astype(BF16), v)
            o = pv if o is None else o + pv
        return o / l

    def run(blocks):
        for pair in range(n_pairs):
            q = q_ref[0, :, pair * LANES:(pair + 1) * LANES]
            lo = _half_masks(q.shape)
            zero = jnp.zeros_like(q)
            o0 = head_out(jnp.where(lo, q, zero), 0, pair, blocks)
            o1 = head_out(jnp.where(lo, zero, q), 1, pair, blocks)
            o_ref[0, :, pair * LANES:(pair + 1) * LANES] = jnp.where(lo, o0, o1).astype(BF16)

    @pl.when(i < nct)
    def _():
        run([(k_ref[0, :n_ctx, :], v_ref[0, :n_ctx, :], None)])

    @pl.when(i >= nct)
    def _():
        q0 = (i - nct) * bq
        start = jnp.clip(q0 - WINDOW, 0, seq - band)
        start = pl.multiple_of(start, WINDOW)
        qpos = q0 + lax.broadcasted_iota(jnp.int32, (bq, band), 0)
        kpos = start + lax.broadcasted_iota(jnp.int32, (bq, band), 1)
        mask = jnp.abs(qpos - kpos) <= WINDOW
        kb = k_ref[0, pl.ds(n_ctx + start, band), :]
        vb = v_ref[0, pl.ds(n_ctx + start, band), :]
        run([(k_ref[0, :n_ctx, :], v_ref[0, :n_ctx, :], None), (kb, vb, mask)])


def win_attention(qkv, sink, n_ctx):
    b, ta, _ = qkv.shape
    bq = ROW_TILE
    n_pairs = WIN_HEADS // WIN_KV_HEADS
    qw = n_pairs * LANES
    kern = functools.partial(_win_attn_kernel, n_ctx=n_ctx, seq=ta - n_ctx, n_pairs=n_pairs)
    return pl.pallas_call(
        kern,
        grid_spec=pltpu.PrefetchScalarGridSpec(
            num_scalar_prefetch=1,
            grid=(b, ta // bq),
            in_specs=[
                pl.BlockSpec((1, bq, qw), lambda b_, i, s: (b_, i, 0)),
                pl.BlockSpec((1, ta, LANES), lambda b_, i, s: (b_, 0, n_pairs)),
                pl.BlockSpec((1, ta, LANES), lambda b_, i, s: (b_, 0, n_pairs + 1)),
            ],
            out_specs=pl.BlockSpec((1, bq, qw), lambda b_, i, s: (b_, i, 0)),
        ),
        out_shape=jax.ShapeDtypeStruct((b, ta, qw), BF16),
        compiler_params=_cparams("parallel", "arbitrary"),
        name="win_attn",
    )(sink.astype(F32), qkv, qkv, qkv)


def _out_proj_kernel(o_ref, w_ref, x_ref, g_ref, y_ref):
    y_ref[0] = x_ref[0] + g_ref[...] * _dot(o_ref[0], w_ref[...])


def out_proj_residual(o, w, xa, mods, k_gate, n_ctx):
    b, ta, d = xa.shape
    kdim = o.shape[2]
    bm = ROW_TILE
    return pl.pallas_call(
        _out_proj_kernel,
        grid=(b, ta // bm),
        in_specs=[
            pl.BlockSpec((1, bm, kdim), lambda b_, i: (b_, i, 0)),
            pl.BlockSpec((kdim, d), lambda b_, i: (0, 0)),
            pl.BlockSpec((1, bm, d), lambda b_, i: (b_, i, 0)),
            _mod_spec(k_gate, n_ctx // bm, d),
        ],
        out_specs=pl.BlockSpec((1, bm, d), lambda b_, i: (b_, i, 0)),
        out_shape=jax.ShapeDtypeStruct(xa.shape, F32),
        input_output_aliases={2: 0},
        compiler_params=_cparams("parallel", "parallel"),
        name="out_proj",
    )(o, w, xa, mods)


def _ffn_kernel(ng_ref, wg_ref, wu_ref, w2_ref, x_ref, mc_ref, ml_ref, y_ref, acc_ref, h_ref, *,
                n_ctx, k_shift, k_scale, k_gate):
    i = pl.program_id(1)
    f = pl.program_id(2)
    bm = acc_ref.shape[0]

    def row_mod(k):
        row = i * bm + lax.broadcasted_iota(jnp.int32, (bm, 1), 0)
        return jnp.where(row < n_ctx, mc_ref[k], ml_ref[k])

    @pl.when(f == 0)
    def _():
        acc_ref[...] = jnp.zeros_like(acc_ref)
        h_ref[...] = _norm_mod(x_ref[0], ng_ref[...], row_mod(k_shift), row_mod(k_scale)).astype(BF16)

    h = h_ref[...]
    g = _dot(h, wg_ref[...])
    u = _dot(h, wu_ref[...])
    act = (g * (1.0 / (1.0 + jnp.exp(-g))) * u).astype(BF16)
    acc_ref[...] += _dot(act, w2_ref[...])

    @pl.when(f == pl.num_programs(2) - 1)
    def _():
        y_ref[0] = x_ref[0] + row_mod(k_gate) * acc_ref[...]


def ffn_residual(norm_gain, w13, w2, xa, mods, k_shift, k_scale, k_gate, n_ctx):
    b, ta, d = xa.shape
    ff = w2.shape[0]
    bm, fk = FFN_ROW_TILE, FF_CHUNK
    nf = ff // fk
    kern = functools.partial(_ffn_kernel, n_ctx=n_ctx, k_shift=k_shift, k_scale=k_scale, k_gate=k_gate)
    in_specs = [
        pl.BlockSpec((1, d), lambda b_, i, f: (0, 0)),
        pl.BlockSpec((d, fk), lambda b_, i, f: (0, f)),
        pl.BlockSpec((d, fk), lambda b_, i, f: (0, nf + f)),
        pl.BlockSpec((fk, d), lambda b_, i, f: (f, 0)),
        pl.BlockSpec((1, bm, d), lambda b_, i, f: (b_, i, 0)),
        pl.BlockSpec((None, None, 6, 1, d), lambda b_, i, f: (b_, 0, 0, 0, 0)),
        pl.BlockSpec((None, None, 6, 1, d), lambda b_, i, f: (b_, 1, 0, 0, 0)),
    ]
    return pl.pallas_call(
        kern,
        grid=(b, ta // bm, nf),
        in_specs=in_specs,
        out_specs=pl.BlockSpec((1, bm, d), lambda b_, i, f: (b_, i, 0)),
        out_shape=jax.ShapeDtypeStruct(xa.shape, F32),
        scratch_shapes=[pltpu.VMEM((bm, d), F32), pltpu.VMEM((bm, d), BF16)],
        input_output_aliases={4: 0},
        compiler_params=_cparams("parallel", "parallel", "arbitrary"),
        name="ffn",
    )(norm_gain.reshape(1, d), w13, w13, w2, xa, mods, mods)


MOE_SLOT_TILE = 512
MOE_GATHER_CHUNK = 512
MOE_COMBINE_CHUNK = 768
_VALID, _FIRST, _LAST = 1, 2, 4


def _one_hot_bf16(cond):
    return jnp.where(cond, 1.0, 0.0).astype(BF16)


def _moe_gather_kernel(tile_ref, chunk_ref, flag_ref, src_ref, h_ref, o_ref, acc_ref):
    k = pl.program_id(0)
    fl = flag_ref[k]

    @pl.when((fl & _FIRST) != 0)
    def _():
        acc_ref[...] = jnp.zeros_like(acc_ref)

    @pl.when((fl & _VALID) != 0)
    def _():
        bm, ch = acc_ref.shape[0], h_ref.shape[0]
        tok = chunk_ref[k] * ch + lax.broadcasted_iota(jnp.int32, (bm, ch), 1)
        acc_ref[...] += _dot(_one_hot_bf16(src_ref[...] == tok), h_ref[...])

    @pl.when((fl & _LAST) != 0)
    def _():
        o_ref[...] = acc_ref[...].astype(BF16)


def _moe_ffn_kernel(te_ref, nu_ref, xs_ref, wg_ref, wu_ref, w2_ref, gs_ref, y_ref, acc_ref):
    i = pl.program_id(0)
    f = pl.program_id(1)

    @pl.when(i < nu_ref[0])
    def _():
        @pl.when(f == 0)
        def _():
            acc_ref[...] = jnp.zeros_like(acc_ref)

        xs = xs_ref[...]
        g = _dot(xs, wg_ref[...])
        u = _dot(xs, wu_ref[...])
        act = (g * (1.0 / (1.0 + jnp.exp(-g))) * u).astype(BF16)
        acc_ref[...] += _dot(act, w2_ref[...])

        @pl.when(f == pl.num_programs(1) - 1)
        def _():
            y_ref[...] = (acc_ref[...] * gs_ref[...]).astype(BF16)


def _moe_combine_kernel(chunk_ref, tile_ref, flag_ref, p1_ref, p2_ref, y_ref, x_ref, gc_ref, gl_ref,
                        o_ref, acc_ref, *, n_ctx, chunks_per_batch):
    k = pl.program_id(0)
    fl = flag_ref[k]
    ch, bm = acc_ref.shape[0], y_ref.shape[0]

    @pl.when((fl & _FIRST) != 0)
    def _():
        acc_ref[...] = jnp.zeros_like(acc_ref)

    @pl.when((fl & _VALID) != 0)
    def _():
        slot = tile_ref[k] * bm + lax.broadcasted_iota(jnp.int32, (ch, bm), 1)
        w = _one_hot_bf16((p1_ref[...] == slot) | (p2_ref[...] == slot))
        acc_ref[...] += _dot(w, y_ref[...])

    @pl.when((fl & _LAST) != 0)
    def _():
        row = (chunk_ref[k] % chunks_per_batch) * ch + lax.broadcasted_iota(jnp.int32, (ch, 1), 0)
        gate = jnp.where(row < n_ctx, gc_ref[...], gl_ref[...])
        o_ref[...] = x_ref[...] + gate * acc_ref[...]


def _expand_items(counts, n_items):
    off = jnp.cumsum(counts)
    total = off[-1]
    k = jnp.minimum(jnp.arange(n_items, dtype=jnp.int32), total - 1)
    grp = jnp.minimum(jnp.searchsorted(off, k, side="right").astype(jnp.int32), counts.shape[0] - 1)
    local = k - (off[grp] - counts[grp])
    valid = jnp.arange(n_items, dtype=jnp.int32) < total
    return grp, local, valid


def moe_residual(h, route, w13, w2, xa, mods, k_gate, n_ctx, ctx_active):
    b, ta, d = xa.shape
    t = b * ta
    ne, ff = w2.shape[0], w2.shape[1]
    bm, gch, cch, fk = MOE_SLOT_TILE, MOE_GATHER_CHUNK, MOE_COMBINE_CHUNK, FF_CHUNK
    nf = ff // fk
    nt = (2 * t) // bm + ne
    ns = nt * bm
    i32 = jnp.int32

    r = route.reshape(t, LANES)
    e1, e2 = r[:, 0].astype(i32), r[:, 1].astype(i32)
    g1, g2 = r[:, 2], r[:, 3]
    tok = jnp.arange(t, dtype=i32)
    active = jnp.ones((t,), bool) if ctx_active else (tok % ta) >= n_ctx

    eid = jnp.arange(ne, dtype=i32)[:, None]
    member = ((e1[None] == eid) | (e2[None] == eid)) & active[None]
    csum = jnp.cumsum(member.astype(i32), axis=1)
    cap = (csum[:, -1] + bm - 1) // bm * bm
    end = jnp.cumsum(cap)
    start = end - cap
    n_used = (end[-1] // bm).astype(i32)

    def slot_of(e_sel):
        rank = jnp.take_along_axis(csum, e_sel[None], axis=0)[0] - 1
        return jnp.where(active, start[e_sel] + rank, -1)

    pos1, pos2 = slot_of(e1), slot_of(e2)
    drop1, drop2 = jnp.where(pos1 >= 0, pos1, ns), jnp.where(pos2 >= 0, pos2, ns)
    src = jnp.full((ns,), -1, i32).at[drop1].set(tok, mode="drop").at[drop2].set(tok, mode="drop")
    gslot = jnp.zeros((ns,), F32).at[drop1].set(g1, mode="drop").at[drop2].set(g2, mode="drop")
    tile_ids = jnp.arange(nt, dtype=i32)
    tile_expert = jnp.minimum(jnp.sum(end[None] <= (tile_ids * bm)[:, None], axis=1), ne - 1).astype(i32)

    src_t = src.reshape(nt, bm)
    c_lo = src_t[:, 0] // gch
    c_hi = jnp.max(src_t, axis=1) // gch
    n_ch = jnp.where(tile_ids < n_used, c_hi - c_lo + 1, 0)
    ni_g = ne * (t // gch) + nt
    g_tile, g_local, g_valid = _expand_items(n_ch, ni_g)
    g_chunk = c_lo[g_tile] + g_local
    g_flag = g_valid * (_VALID + _FIRST * (g_local == 0) + _LAST * (g_local == n_ch[g_tile] - 1))

    xs = pl.pallas_call(
        _moe_gather_kernel,
        grid_spec=pltpu.PrefetchScalarGridSpec(
            num_scalar_prefetch=3,
            grid=(ni_g,),
            in_specs=[
                pl.BlockSpec((bm, 1), lambda k, tl, cn, fl: (tl[k], 0)),
                pl.BlockSpec((gch, d), lambda k, tl, cn, fl: (cn[k], 0)),
            ],
            out_specs=pl.BlockSpec((bm, d), lambda k, tl, cn, fl: (tl[k], 0)),
            scratch_shapes=[pltpu.VMEM((bm, d), F32)],
        ),
        out_shape=jax.ShapeDtypeStruct((ns, d), BF16),
        compiler_params=_cparams("arbitrary"),
        name="moe_gather",
    )(g_tile, g_chunk, g_flag.astype(i32), src.reshape(ns, 1), h.reshape(t, d))

    def last_used(i, nu):
        return jnp.minimum(i, nu[0] - 1)

    def f_eff(i, f, nu):
        return jnp.where(i < nu[0], f, nf - 1)

    y = pl.pallas_call(
        _moe_ffn_kernel,
        grid_spec=pltpu.PrefetchScalarGridSpec(
            num_scalar_prefetch=2,
            grid=(nt, nf),
            in_specs=[
                pl.BlockSpec((bm, d), lambda i, f, te, nu: (last_used(i, nu), 0)),
                pl.BlockSpec((None, d, fk), lambda i, f, te, nu: (te[i], 0, f_eff(i, f, nu))),
                pl.BlockSpec((None, d, fk), lambda i, f, te, nu: (te[i], 0, nf + f_eff(i, f, nu))),
                pl.BlockSpec((None, fk, d), lambda i, f, te, nu: (te[i], f_eff(i, f, nu), 0)),
                pl.BlockSpec((bm, 1), lambda i, f, te, nu: (last_used(i, nu), 0)),
            ],
            out_specs=pl.BlockSpec((bm, d), lambda i, f, te, nu: (last_used(i, nu), 0)),
            scratch_shapes=[pltpu.VMEM((bm, d), F32)],
        ),
        out_shape=jax.ShapeDtypeStruct((ns, d), BF16),
        compiler_params=_cparams("arbitrary", "arbitrary"),
        name="moe_ffn",
    )(tile_expert, n_used.reshape(1), xs, w13, w13, w2, gslot.reshape(ns, 1))

    nc = t // cch
    cs0 = jnp.concatenate([jnp.zeros((ne, 1), i32), csum], axis=1)[:, ::cch]
    before, through = cs0[:, :-1].T, cs0[:, 1:].T
    t_lo = (start[None] + before) // bm
    t_hi = (start[None] + through - 1) // bm
    n_tl = jnp.where(through > before, t_hi - t_lo + 1, 0).reshape(-1)
    ni_c = ne * nc + nt
    c_pair, c_local, c_valid = _expand_items(n_tl, ni_c)
    c_chunk = c_pair // ne
    c_tile = t_lo.reshape(-1)[c_pair] + c_local
    off = jnp.cumsum(n_tl)
    k_eff = jnp.minimum(jnp.arange(ni_c, dtype=i32), off[-1] - 1)
    chunk_first = (off - n_tl)[c_chunk * ne]
    chunk_last = off[c_chunk * ne + ne - 1] - 1
    c_flag = c_valid * (_VALID + _FIRST * (k_eff == chunk_first) + _LAST * (k_eff == chunk_last))

    cpb = ta // cch
    kern = functools.partial(_moe_combine_kernel, n_ctx=n_ctx, chunks_per_batch=cpb)
    out = pl.pallas_call(
        kern,
        grid_spec=pltpu.PrefetchScalarGridSpec(
            num_scalar_prefetch=3,
            grid=(ni_c,),
            in_specs=[
                pl.BlockSpec((cch, 1), lambda k, cn, tl, fl: (cn[k], 0)),
                pl.BlockSpec((cch, 1), lambda k, cn, tl, fl: (cn[k], 0)),
                pl.BlockSpec((bm, d), lambda k, cn, tl, fl: (tl[k], 0)),
                pl.BlockSpec((cch, d), lambda k, cn, tl, fl: (cn[k], 0)),
                pl.BlockSpec((None, None, None, 1, d), lambda k, cn, tl, fl: (cn[k] // cpb, 0, k_gate, 0, 0)),
                pl.BlockSpec((None, None, None, 1, d), lambda k, cn, tl, fl: (cn[k] // cpb, 1, k_gate, 0, 0)),
            ],
            out_specs=pl.BlockSpec((cch, d), lambda k, cn, tl, fl: (cn[k], 0)),
            scratch_shapes=[pltpu.VMEM((cch, d), F32)],
        ),
        out_shape=jax.ShapeDtypeStruct((t, d), F32),
        input_output_aliases={6: 0},
        compiler_params=_cparams("arbitrary"),
        name="moe_combine",
    )(c_chunk, c_tile, c_flag.astype(i32), pos1.reshape(t, 1), pos2.reshape(t, 1), y, xa.reshape(t, d), mods, mods)
    return out.reshape(b, ta, d)


def _rope_tables(seq, n_ctx, rot_dim):
    rows = seq // GRID_W
    quarter = rot_dim // 4
    inv_freq = ROPE_THETA ** (-jnp.arange(quarter, dtype=F32) / quarter)
    row = jnp.repeat(jnp.arange(rows, dtype=F32), GRID_W)
    col = jnp.tile(jnp.arange(GRID_W, dtype=F32), rows)
    ang = jnp.concatenate([row[:, None] * inv_freq, col[:, None] * inv_freq], axis=-1)
    cos, sin = jnp.cos(ang), jnp.sin(ang)
    reps = LANES // rot_dim
    c = jnp.tile(jnp.concatenate([cos, cos], axis=-1), (1, reps))
    s = jnp.tile(jnp.concatenate([-sin, sin], axis=-1), (1, reps))
    c = jnp.concatenate([jnp.ones((n_ctx, LANES), F32), c], axis=0)
    s = jnp.concatenate([jnp.zeros((n_ctx, LANES), F32), s], axis=0)
    return c, s


def kernel(x, c, ctx, c_ctx, ada_w, ada_b, norm_mix, norm_ffn, gqa_wqkv, gqa_q_gain, gqa_k_gain, gqa_wo, mla_wdown, mla_qa_gain, mla_kva_gain, mla_wuq, mla_wukv, mla_q_gain, mla_k_gain, mla_wo, win_wqkv, win_q_gain, win_k_gain, win_sink, win_wo, diff_wqkv, diff_q_gain, diff_k_gain, diff_lambda, diff_subln, diff_wo, ffn_w13, ffn_w2, moe_router, moe_w13, moe_w2):
    b, seq, d = x.shape
    n_ctx = ctx.shape[1]
    depth = ada_w.shape[0]
    ones = lambda n: jnp.ones((n,), F32)
    zeros = lambda n: jnp.zeros((n,), F32)

    crows = -(-(b + 1) // 8) * 8
    cond = jnp.zeros((crows, d), F32).at[:b].set(c).at[b].set(c_ctx)
    mods_all = adaln(cond, ada_w, ada_b)

    xa = jnp.concatenate([ctx, x], axis=1)

    for i in range(depth):
        need_ctx = i < depth - 1
        j = i // 4
        kind = i % 4
        ml = mods_all[i, :b].reshape(b, 6, 1, d)
        mc = jnp.broadcast_to(mods_all[i, b].reshape(1, 6, 1, d), (b, 6, 1, d))
        mods = jnp.stack([mc, ml], axis=1)

        prenorm = (norm_mix[i], mods, 0, 1, n_ctx)
        if kind == 0:
            hd = GQA_HEAD_DIM
            scale = hd ** -0.5 * LOG2E
            nq, nk = GQA_HEADS * hd, GQA_KV_HEADS * hd
            groups = [(g * hd, hd) for g in range(GQA_HEADS + GQA_KV_HEADS)]
            gain = jnp.concatenate([jnp.tile(gqa_q_gain[j] * scale, GQA_HEADS),
                                    jnp.tile(gqa_k_gain[j], GQA_KV_HEADS), ones(nk)])
            qkv = proj(xa, 0, gqa_wqkv[j].astype(BF16), groups, gain, _rope_tables(seq, n_ctx, hd),
                       range((nq + nk) // LANES), hd // 2, "gqa_qkv", prenorm)
            o = gqa_attention(qkv, n_ctx)
            w_o = gqa_wo[j]
        elif kind == 1:
            scale = (MLA_NOPE + MLA_ROPE) ** -0.5 * LOG2E
            hh = MLA_HEADS
            wd = mla_wdown[j]
            pad = jnp.zeros((d, LANES - MLA_ROPE), F32)
            wd2 = jnp.concatenate([wd[:, :MLA_Q_LORA], wd[:, MLA_Q_LORA + MLA_KV_LORA:], pad,
                                   wd[:, MLA_Q_LORA:MLA_Q_LORA + MLA_KV_LORA]], axis=1)
            kpe0 = MLA_Q_LORA
            ckv0 = MLA_Q_LORA + LANES
            groups = [(0, MLA_Q_LORA), (kpe0, MLA_ROPE), (ckv0, MLA_KV_LORA)]
            gain = jnp.concatenate([mla_qa_gain[j], mla_k_gain[j][MLA_NOPE:], zeros(LANES - MLA_ROPE),
                                    mla_kva_gain[j]])
            rope = _rope_tables(seq, n_ctx, MLA_ROPE)
            dn = proj(xa, 0, wd2.astype(BF16), groups, gain, rope, [kpe0 // LANES], MLA_ROPE // 2, "mla_down",
                      prenorm)
            wq = mla_wuq[j].reshape(MLA_Q_LORA, hh, MLA_NOPE + MLA_ROPE)
            wq2 = jnp.pad(wq, ((0, 0), (0, 0), (0, LANES - MLA_ROPE))).reshape(MLA_Q_LORA, hh * 2 * LANES)
            groups = ([(g * 2 * LANES, MLA_NOPE) for g in range(hh)]
                      + [(g * 2 * LANES + MLA_NOPE, MLA_ROPE) for g in range(hh)])
            qg = jnp.concatenate([mla_q_gain[j] * scale, zeros(LANES - MLA_ROPE)])
            gain = jnp.tile(qg, hh)
            q = proj(dn, 0, wq2.astype(BF16), groups, gain, rope, range(1, 2 * hh, 2), MLA_ROPE // 2, "mla_q")
            groups = [(g * (MLA_NOPE + MLA_V), MLA_NOPE) for g in range(hh)]
            gain = jnp.tile(jnp.concatenate([mla_k_gain[j][:MLA_NOPE], ones(MLA_V)]), hh)
            kv = proj(dn, ckv0 // MLA_KV_LORA, mla_wukv[j].astype(BF16), groups, gain, rope, [], MLA_ROPE // 2, "mla_kv")
            o = mla_attention(q, kv, dn, kpe0 // LANES, n_ctx)
            w_o = mla_wo[j]
        elif kind == 2:
            hd = WIN_HEAD_DIM
            scale = hd ** -0.5 * LOG2E
            n_pairs = WIN_HEADS // WIN_KV_HEADS
            nq = WIN_HEADS * hd
            perm = np.array([(kv * n_pairs + g) * hd + t for g in range(n_pairs)
                             for kv in range(WIN_KV_HEADS) for t in range(hd)])
            wq = win_wqkv[j]
            w2_ = jnp.concatenate([wq[:, perm], wq[:, nq:]], axis=1)
            groups = [(g * hd, hd) for g in range(WIN_HEADS + WIN_KV_HEADS)]
            gain = jnp.concatenate([jnp.tile(win_q_gain[j] * scale, WIN_HEADS),
                                    jnp.tile(win_k_gain[j], WIN_KV_HEADS), ones(WIN_KV_HEADS * hd)])
            qkv = proj(xa, 0, w2_.astype(BF16), groups, gain, _rope_tables(seq, n_ctx, hd),
                       range(n_pairs + 1), hd // 2, "win_qkv", prenorm)
            o = win_attention(qkv, win_sink[j] * LOG2E, n_ctx)
            w_o = win_wo[j][perm, :]
        else:
            hd = DIFF_HEAD_DIM
            scale = hd ** -0.5 * LOG2E
            lam_init = 0.8 - 0.6 * math.exp(-0.3 * i)
            nqk = 2 * DIFF_HEADS
            groups = [(g * hd, hd) for g in range(2 * nqk)]
            gain = jnp.concatenate([jnp.tile(diff_q_gain[j] * scale, nqk), jnp.tile(diff_k_gain[j], nqk),
                                    ones(DIFF_HEADS * 2 * hd)])
            qkv = proj(xa, 0, diff_wqkv[j].astype(BF16), groups, gain, _rope_tables(seq, n_ctx, hd),
                       range(2 * DIFF_HEADS), hd // 2, "diff_qkv", prenorm)
            o = diff_attention(qkv, diff_lambda[j], diff_subln[j], lam_init, n_ctx)
            w_o = diff_wo[j]
        xa = out_proj_residual(o, w_o.astype(BF16), xa, mods, 2, n_ctx)

        jj = i // 2
        if i % 2 == 0:
            xa = ffn_residual(norm_ffn[i], ffn_w13[jj].astype(BF16), ffn_w2[jj].astype(BF16), xa, mods, 3, 4, 5,
                              n_ctx)
        else:
            h, route = norm_mod(xa, norm_ffn[i], mods, 3, 4, n_ctx, router=moe_router[jj])
            xa = moe_residual(h, route, moe_w13[jj].astype(BF16), moe_w2[jj].astype(BF16), xa, mods, 5,
                              n_ctx, need_ctx)
    return xa[:, n_ctx:]
```

```python
import functools
import math

import jax
import jax.numpy as jnp
import numpy as np
from jax import lax
from jax.experimental import pallas as pl
from jax.experimental.pallas import tpu as pltpu

GRID_W = 64
ROPE_THETA = 10000.0
EPS = 1e-6
NEG_INF = -1e30
LANES = 128
MXU_WIDTH = 256
ROW_TILE = 256
FFN_ROW_TILE = 768
FF_CHUNK = 512
VMEM_LIMIT = 56 * 1024 * 1024

GQA_HEADS, GQA_KV_HEADS, GQA_HEAD_DIM = 8, 2, 128
MLA_HEADS, MLA_Q_LORA, MLA_KV_LORA, MLA_NOPE, MLA_ROPE, MLA_V = 8, 384, 256, 128, 64, 128
WIN_HEADS, WIN_KV_HEADS, WIN_HEAD_DIM, WINDOW = 16, 2, 64, 128
DIFF_HEADS, DIFF_HEAD_DIM = 8, 64
N_EXPERTS = 8

F32 = jnp.float32
BF16 = jnp.bfloat16


def _cparams(*sem):
    return pltpu.CompilerParams(dimension_semantics=sem, vmem_limit_bytes=VMEM_LIMIT)


def _dot(a, b):
    return jnp.dot(a, b, preferred_element_type=F32)


def _dot_nt(a, b):
    return lax.dot_general(a, b, (((1,), (1,)), ((), ())), preferred_element_type=F32)


def _split_bf16(v):
    hi = v.astype(BF16)
    lo = (v - hi.astype(F32)).astype(BF16)
    return hi, lo


def _adaln_kernel(c_ref, w_ref, b_ref, o_ref):
    c = c_ref[...]
    sc = c * (1.0 / (1.0 + jnp.exp(-c)))
    o_ref[0] = jnp.dot(sc, w_ref[0], preferred_element_type=F32,
                       precision=lax.Precision.HIGHEST) + b_ref[0]


def adaln(cond, ada_w, ada_b):
    depth, d, n = ada_w.shape
    rows = cond.shape[0]
    bn = 1536
    return pl.pallas_call(
        _adaln_kernel,
        grid=(depth, n // bn),
        in_specs=[
            pl.BlockSpec((rows, d), lambda l, j: (0, 0)),
            pl.BlockSpec((1, d, bn), lambda l, j: (l, 0, j)),
            pl.BlockSpec((1, 1, bn), lambda l, j: (l, 0, j)),
        ],
        out_specs=pl.BlockSpec((1, rows, bn), lambda l, j: (l, 0, j)),
        out_shape=jax.ShapeDtypeStruct((depth, rows, n), F32),
        compiler_params=_cparams("parallel", "parallel"),
        name="adaln",
    )(cond, ada_w, ada_b.reshape(depth, 1, n))


def _mod_spec(k, n_ctx_tiles, d):
    return pl.BlockSpec((None, None, None, 1, d),
                        lambda b, i, *_: (b, jnp.minimum(i // n_ctx_tiles, 1), k, 0, 0))


def _norm_mod(x, g, shift, scale):
    ms = jnp.mean(x * x, axis=-1, keepdims=True)
    return x * lax.rsqrt(ms + EPS) * g * (1.0 + scale) + shift


def _norm_mod_kernel(x_ref, g_ref, sh_ref, sc_ref, h_ref):
    h_ref[0] = _norm_mod(x_ref[0], g_ref[...], sh_ref[...], sc_ref[...]).astype(BF16)


def _norm_mod_router_kernel(x_ref, g_ref, sh_ref, sc_ref, r_ref, h_ref, cw_ref):
    h = _norm_mod(x_ref[0], g_ref[...], sh_ref[...], sc_ref[...])
    h_ref[0] = h.astype(BF16)
    logits = jnp.dot(h, r_ref[...], preferred_element_type=F32, precision=lax.Precision.HIGHEST)
    lane = lax.broadcasted_iota(jnp.int32, logits.shape, 1)
    logits = jnp.where(lane < N_EXPERTS, logits, -jnp.inf)
    m1 = jnp.max(logits, axis=-1, keepdims=True)
    i1 = jnp.min(jnp.where(logits == m1, lane, LANES), axis=-1, keepdims=True)
    rest = jnp.where(lane == i1, -jnp.inf, logits)
    m2 = jnp.max(rest, axis=-1, keepdims=True)
    i2 = jnp.min(jnp.where(rest == m2, lane, LANES), axis=-1, keepdims=True)
    e2 = jnp.exp(m2 - m1)
    den = 1.0 + e2
    cw_ref[0] = jnp.where(lane == 0, i1.astype(F32), jnp.where(lane == 1, i2.astype(F32),
                          jnp.where(lane == 2, 1.0 / den, jnp.where(lane == 3, e2 / den, 0.0))))


def norm_mod(xa, gain, mods, k_shift, k_scale, n_ctx, router=None):
    b, ta, d = xa.shape
    bm = ROW_TILE
    nct = n_ctx // bm
    in_specs = [
        pl.BlockSpec((1, bm, d), lambda b_, i: (b_, i, 0)),
        pl.BlockSpec((1, d), lambda b_, i: (0, 0)),
        _mod_spec(k_shift, nct, d),
        _mod_spec(k_scale, nct, d),
    ]
    h_spec = pl.BlockSpec((1, bm, d), lambda b_, i: (b_, i, 0))
    h_shape = jax.ShapeDtypeStruct((b, ta, d), BF16)
    args = [xa, gain.reshape(1, d), mods, mods]
    if router is None:
        return pl.pallas_call(
            _norm_mod_kernel, grid=(b, ta // bm), in_specs=in_specs, out_specs=h_spec,
            out_shape=h_shape, compiler_params=_cparams("parallel", "parallel"), name="norm_mod",
        )(*args)
    rpad = jnp.zeros((d, LANES), F32).at[:, :N_EXPERTS].set(router)
    return pl.pallas_call(
        _norm_mod_router_kernel, grid=(b, ta // bm),
        in_specs=in_specs + [pl.BlockSpec((d, LANES), lambda b_, i: (0, 0))],
        out_specs=[h_spec, pl.BlockSpec((1, bm, LANES), lambda b_, i: (b_, i, 0))],
        out_shape=[h_shape, jax.ShapeDtypeStruct((b, ta, LANES), F32)],
        compiler_params=_cparams("parallel", "parallel"), name="norm_mod_router",
    )(*args, rpad)


def _proj_kernel(*refs, rope_blocks, rope_half, prenorm, norm_blocks):
    if prenorm:
        x_ref, g_ref, sh_ref, sc_ref = refs[:4]
        h = _norm_mod(x_ref[0], g_ref[...], sh_ref[...], sc_ref[...]).astype(BF16)
        refs = refs[4:]
    else:
        h = refs[0][0]
        refs = refs[1:]
    y = _dot(h, refs[0][...])
    n = y.shape[1]
    s_hi, s_lo = _split_bf16(y * y)
    if norm_blocks is None:
        _, e_ref, et_ref, igs_ref, u_ref, gain_ref, c_ref, s_ref, o_ref = refs
        ss = _dot(s_hi, e_ref[...]) + _dot(s_lo, e_ref[...])
        inv = lax.rsqrt(ss * igs_ref[...] + EPS)
        i_hi, i_lo = _split_bf16(inv)
        fac = _dot(i_hi, et_ref[...]) + _dot(i_lo, et_ref[...]) + u_ref[...]
        z = y * (fac * gain_ref[...])
    else:
        _, j_ref, igs_ref, u_ref, gain_ref, c_ref, s_ref, o_ref = refs
        zs = []
        for blk in range(n // MXU_WIDTH):
            sl = slice(blk * MXU_WIDTH, (blk + 1) * MXU_WIDTH)
            if blk in norm_blocks:
                ss = _dot(s_hi[:, sl], j_ref[blk]) + _dot(s_lo[:, sl], j_ref[blk])
                u = u_ref[:, sl]
                fac = lax.rsqrt(ss * igs_ref[:, sl] + EPS) * (1.0 - u) + u
                zs.append(y[:, sl] * (fac * gain_ref[:, sl]))
            else:
                zs.append(y[:, sl] * gain_ref[:, sl])
        z = jnp.concatenate(zs, axis=1)
    cos = c_ref[...]
    sin = s_ref[...]
    if rope_half == LANES // 2:
        first = None
    else:
        lane = lax.broadcasted_iota(jnp.int32, cos.shape, 1)
        first = (lane % (2 * rope_half)) < rope_half
    for blk in range(n // LANES):
        zb = z[:, blk * LANES:(blk + 1) * LANES]
        if blk in rope_blocks:
            if first is None:
                rot = pltpu.roll(zb, LANES // 2, 1)
            else:
                rot = jnp.where(first, pltpu.roll(zb, LANES - rope_half, 1), pltpu.roll(zb, rope_half, 1))
            zb = zb * cos + rot * sin
        o_ref[0, :, blk * LANES:(blk + 1) * LANES] = zb.astype(BF16)


def proj(h, kin_block, w, groups, gain, rope_tabs, rope_blocks, rope_half, name, prenorm=None):
    b, ta, _ = h.shape
    k, n = w.shape
    bm = ROW_TILE
    mw = MXU_WIDTH
    u = np.ones((1, n), np.float32)
    for start, size in groups:
        u[0, start:start + size] = 0.0
    local = n % mw == 0 and all(start // mw == (start + size - 1) // mw for start, size in groups)
    if local:
        jm = np.zeros((n // mw, mw, mw), np.float32)
        igs = np.zeros((1, n), np.float32)
        for start, size in groups:
            blk, o = divmod(start, mw)
            jm[blk, o:o + size, o:o + size] = 1.0
            igs[0, start:start + size] = 1.0 / size
        norm_blocks = frozenset(start // mw for start, _ in groups)
        norm_args = [jnp.asarray(jm, BF16), jnp.asarray(igs)]
        norm_specs = [pl.BlockSpec((n // mw, mw, mw), lambda b_, i: (0, 0, 0)),
                      pl.BlockSpec((1, n), lambda b_, i: (0, 0))]
    else:
        e = np.zeros((n, LANES), np.float32)
        igs = np.zeros((1, LANES), np.float32)
        for gi, (start, size) in enumerate(groups):
            e[start:start + size, gi] = 1.0
            igs[0, gi] = 1.0 / size
        norm_blocks = None
        norm_args = [jnp.asarray(e, BF16), jnp.asarray(e.T, BF16), jnp.asarray(igs)]
        norm_specs = [pl.BlockSpec((n, LANES), lambda b_, i: (0, 0)),
                      pl.BlockSpec((LANES, n), lambda b_, i: (0, 0)),
                      pl.BlockSpec((1, LANES), lambda b_, i: (0, 0))]
    cos, sin = rope_tabs
    kern = functools.partial(_proj_kernel, rope_blocks=frozenset(rope_blocks), rope_half=rope_half,
                             prenorm=prenorm is not None, norm_blocks=norm_blocks)
    lead_specs = [pl.BlockSpec((1, bm, k), lambda b_, i: (b_, i, kin_block))]
    lead_args = [h]
    if prenorm is not None:
        ngain, mods, k_shift, k_scale, n_ctx = prenorm
        nct = n_ctx // bm
        lead_specs += [pl.BlockSpec((1, k), lambda b_, i: (0, 0)), _mod_spec(k_shift, nct, k),
                       _mod_spec(k_scale, nct, k)]
        lead_args += [ngain.reshape(1, k), mods, mods]
    return pl.pallas_call(
        kern,
        grid=(b, ta // bm),
        in_specs=lead_specs + [pl.BlockSpec((k, n), lambda b_, i: (0, 0))] + norm_specs + [
            pl.BlockSpec((1, n), lambda b_, i: (0, 0)),
            pl.BlockSpec((1, n), lambda b_, i: (0, 0)),
            pl.BlockSpec((bm, LANES), lambda b_, i: (i, 0)),
            pl.BlockSpec((bm, LANES), lambda b_, i: (i, 0)),
        ],
        out_specs=pl.BlockSpec((1, bm, n), lambda b_, i: (b_, i, 0)),
        out_shape=jax.ShapeDtypeStruct((b, ta, n), BF16),
        compiler_params=_cparams("parallel", "parallel"),
        name=name,
    )(*lead_args, w, *norm_args, jnp.asarray(u), gain.reshape(1, n).astype(F32), cos, sin)


LOG2E = math.log2(math.e)


def _softmax_pv(s, v):
    m = jnp.max(s, axis=-1, keepdims=True)
    p = jnp.exp2(s - m)
    l = jnp.sum(p, axis=-1, keepdims=True)
    return _dot(p.astype(BF16), v) / l


def _gqa_attn_kernel(q_ref, k_ref, v_ref, o_ref, *, n_ctx, group, hd):
    i = pl.program_id(2)

    def attend(nk):
        k = k_ref[0, :nk, :]
        v = v_ref[0, :nk, :]
        for g in range(group):
            q = q_ref[0, :, g * hd:(g + 1) * hd]
            o = _softmax_pv(_dot_nt(q, k), v)
            o_ref[0, :, g * hd:(g + 1) * hd] = o.astype(BF16)

    nct = n_ctx // ROW_TILE

    @pl.when(i < nct)
    def _():
        attend(n_ctx)

    @pl.when(i >= nct)
    def _():
        attend(k_ref.shape[1])


def gqa_attention(qkv, n_ctx):
    b, ta, _ = qkv.shape
    hd, group, kvh = GQA_HEAD_DIM, GQA_HEADS // GQA_KV_HEADS, GQA_KV_HEADS
    bq = ROW_TILE
    qw = group * hd
    kb = GQA_HEADS
    vb = GQA_HEADS + kvh
    kern = functools.partial(_gqa_attn_kernel, n_ctx=n_ctx, group=group, hd=hd)
    return pl.pallas_call(
        kern,
        grid=(b, kvh, ta // bq),
        in_specs=[
            pl.BlockSpec((1, bq, qw), lambda b_, h, i: (b_, i, h)),
            pl.BlockSpec((1, ta, hd), lambda b_, h, i: (b_, 0, kb + h)),
            pl.BlockSpec((1, ta, hd), lambda b_, h, i: (b_, 0, vb + h)),
        ],
        out_specs=pl.BlockSpec((1, bq, qw), lambda b_, h, i: (b_, i, h)),
        out_shape=jax.ShapeDtypeStruct((b, ta, GQA_HEADS * hd), BF16),
        compiler_params=_cparams("parallel", "parallel", "arbitrary"),
        name="gqa_attn",
    )(qkv, qkv, qkv)


def _mla_attn_kernel(q_ref, kn_ref, kp_ref, v_ref, o_ref, kcat_ref, *, n_ctx):
    i = pl.program_id(2)

    @pl.when(i == 0)
    def _():
        kcat_ref[:, :LANES] = kn_ref[0]
        kcat_ref[:, LANES:] = kp_ref[0]

    def attend(nk):
        s = _dot_nt(q_ref[0], kcat_ref[:nk, :])
        o_ref[0] = _softmax_pv(s, v_ref[0, :nk, :]).astype(BF16)

    nct = n_ctx // ROW_TILE

    @pl.when(i < nct)
    def _():
        attend(n_ctx)

    @pl.when(i >= nct)
    def _():
        attend(kn_ref.shape[1])


def mla_attention(q, kv, dn, kpe_block, n_ctx):
    b, ta, _ = q.shape
    bq = ROW_TILE
    hh = MLA_HEADS
    kern = functools.partial(_mla_attn_kernel, n_ctx=n_ctx)
    return pl.pallas_call(
        kern,
        grid=(b, hh, ta // bq),
        in_specs=[
            pl.BlockSpec((1, bq, 2 * LANES), lambda b_, h, i: (b_, i, h)),
            pl.BlockSpec((1, ta, LANES), lambda b_, h, i: (b_, 0, 2 * h)),
            pl.BlockSpec((1, ta, LANES), lambda b_, h, i: (b_, 0, kpe_block)),
            pl.BlockSpec((1, ta, LANES), lambda b_, h, i: (b_, 0, 2 * h + 1)),
        ],
        out_specs=pl.BlockSpec((1, bq, LANES), lambda b_, h, i: (b_, i, h)),
        out_shape=jax.ShapeDtypeStruct((b, ta, hh * MLA_V), BF16),
        scratch_shapes=[pltpu.VMEM((ta, 2 * LANES), BF16)],
        compiler_params=_cparams("parallel", "parallel", "arbitrary"),
        name="mla_attn",
    )(q, kv, dn, kv)


def _half_masks(shape):
    lane = lax.broadcasted_iota(jnp.int32, shape, 1)
    return lane < (LANES // 2)


def _diff_attn_kernel(q_ref, k_ref, v_ref, lam_ref, sub_ref, o_ref, *, n_ctx, lam_init):
    i = pl.program_id(2)
    lp = lam_ref[...]
    lam = (jnp.exp(jnp.sum(lp[0:1] * lp[1:2], axis=-1, keepdims=True))
           - jnp.exp(jnp.sum(lp[2:3] * lp[3:4], axis=-1, keepdims=True)) + lam_init)

    def attend(nk):
        q = q_ref[0]
        lo = _half_masks(q.shape)
        zero = jnp.zeros_like(q)
        k = k_ref[0, :nk, :]
        v = v_ref[0, :nk, :]
        def probs(s, weight):
            m = jnp.max(s, axis=-1, keepdims=True)
            e = jnp.exp2(s - m)
            return e * (weight / jnp.sum(e, axis=-1, keepdims=True))

        p = probs(_dot_nt(jnp.where(lo, q, zero), k), 1.0) - probs(_dot_nt(jnp.where(lo, zero, q), k), lam)
        o = _dot(p.astype(BF16), v)
        ms = jnp.mean(o * o, axis=-1, keepdims=True)
        o_ref[0] = (o * lax.rsqrt(ms + EPS) * sub_ref[...] * (1.0 - lam_init)).astype(BF16)

    nct = n_ctx // ROW_TILE

    @pl.when(i < nct)
    def _():
        attend(n_ctx)

    @pl.when(i >= nct)
    def _():
        attend(k_ref.shape[1])


def diff_attention(qkv, lam_p, subln, lam_init, n_ctx):
    b, ta, _ = qkv.shape
    bq = ROW_TILE
    hh = DIFF_HEADS
    kern = functools.partial(_diff_attn_kernel, n_ctx=n_ctx, lam_init=lam_init)
    return pl.pallas_call(
        kern,
        grid=(b, hh, ta // bq),
        in_specs=[
            pl.BlockSpec((1, bq, LANES), lambda b_, h, i: (b_, i, h)),
            pl.BlockSpec((1, ta, LANES), lambda b_, h, i: (b_, 0, hh + h)),
            pl.BlockSpec((1, ta, LANES), lambda b_, h, i: (b_, 0, 2 * hh + h)),
            pl.BlockSpec((4, DIFF_HEAD_DIM), lambda b_, h, i: (0, 0)),
            pl.BlockSpec((1, LANES), lambda b_, h, i: (0, 0)),
        ],
        out_specs=pl.BlockSpec((1, bq, LANES), lambda b_, h, i: (b_, i, h)),
        out_shape=jax.ShapeDtypeStruct((b, ta, hh * 2 * DIFF_HEAD_DIM), BF16),
        compiler_params=_cparams("parallel", "parallel", "arbitrary"),
        name="diff_attn",
    )(qkv, qkv, qkv, lam_p.astype(F32), subln.reshape(1, LANES).astype(F32))


def _win_attn_kernel(sink_ref, q_ref, k_ref, v_ref, o_ref, *, n_ctx, seq, n_pairs):
    i = pl.program_id(1)
    bq = q_ref.shape[1]
    band = bq + 2 * WINDOW
    nct = n_ctx // bq

    def head_out(qm, kv_idx, pair, blocks):
        sk = sink_ref[kv_idx * n_pairs + pair]
        ss = []
        m = None
        for k, _, mask in blocks:
            s = _dot_nt(qm, k)
            if mask is not None:
                s = jnp.where(mask, s, NEG_INF)
            ss.append(s)
            bm_ = jnp.max(s, axis=-1, keepdims=True)
            m = bm_ if m is None else jnp.maximum(m, bm_)
        m = jnp.maximum(m, sk)
        l = jnp.exp2(sk - m)
        o = None
        for s, (_, v, _) in zip(ss, blocks):
            p = jnp.exp2(s - m)
            l = l + jnp.sum(p, axis=-1, keepdims=True)
            pv = _dot(p.astype(BF16), v)
            o = pv if o is None else o + pv
        return o / l

    def run(blocks):
        for pair in range(n_pairs):
            q = q_ref[0, :, pair * LANES:(pair + 1) * LANES]
            lo = _half_masks(q.shape)
            zero = jnp.zeros_like(q)
            o0 = head_out(jnp.where(lo, q, zero), 0, pair, blocks)
            o1 = head_out(jnp.where(lo, zero, q), 1, pair, blocks)
            o_ref[0, :, pair * LANES:(pair + 1) * LANES] = jnp.where(lo, o0, o1).astype(BF16)

    @pl.when(i < nct)
    def _():
        run([(k_ref[0, :n_ctx, :], v_ref[0, :n_ctx, :], None)])

    @pl.when(i >= nct)
    def _():
        q0 = (i - nct) * bq
        start = jnp.clip(q0 - WINDOW, 0, seq - band)
        start = pl.multiple_of(start, WINDOW)
        qpos = q0 + lax.broadcasted_iota(jnp.int32, (bq, band), 0)
        kpos = start + lax.broadcasted_iota(jnp.int32, (bq, band), 1)
        mask = jnp.abs(qpos - kpos) <= WINDOW
        kb = k_ref[0, pl.ds(n_ctx + start, band), :]
        vb = v_ref[0, pl.ds(n_ctx + start, band), :]
        run([(k_ref[0, :n_ctx, :], v_ref[0, :n_ctx, :], None), (kb, vb, mask)])


def win_attention(qkv, sink, n_ctx):
    b, ta, _ = qkv.shape
    bq = ROW_TILE
    n_pairs = WIN_HEADS // WIN_KV_HEADS
    qw = n_pairs * LANES
    kern = functools.partial(_win_attn_kernel, n_ctx=n_ctx, seq=ta - n_ctx, n_pairs=n_pairs)
    return pl.pallas_call(
        kern,
        grid_spec=pltpu.PrefetchScalarGridSpec(
            num_scalar_prefetch=1,
            grid=(b, ta // bq),
            in_specs=[
                pl.BlockSpec((1, bq, qw), lambda b_, i, s: (b_, i, 0)),
                pl.BlockSpec((1, ta, LANES), lambda b_, i, s: (b_, 0, n_pairs)),
                pl.BlockSpec((1, ta, LANES), lambda b_, i, s: (b_, 0, n_pairs + 1)),
            ],
            out_specs=pl.BlockSpec((1, bq, qw), lambda b_, i, s: (b_, i, 0)),
        ),
        out_shape=jax.ShapeDtypeStruct((b, ta, qw), BF16),
        compiler_params=_cparams("parallel", "arbitrary"),
        name="win_attn",
    )(sink.astype(F32), qkv, qkv, qkv)


def _out_proj_kernel(o_ref, w_ref, x_ref, g_ref, y_ref):
    y_ref[0] = x_ref[0] + g_ref[...] * _dot(o_ref[0], w_ref[...])


def out_proj_residual(o, w, xa, mods, k_gate, n_ctx):
    b, ta, d = xa.shape
    kdim = o.shape[2]
    bm = ROW_TILE
    return pl.pallas_call(
        _out_proj_kernel,
        grid=(b, ta // bm),
        in_specs=[
            pl.BlockSpec((1, bm, kdim), lambda b_, i: (b_, i, 0)),
            pl.BlockSpec((kdim, d), lambda b_, i: (0, 0)),
            pl.BlockSpec((1, bm, d), lambda b_, i: (b_, i, 0)),
            _mod_spec(k_gate, n_ctx // bm, d),
        ],
        out_specs=pl.BlockSpec((1, bm, d), lambda b_, i: (b_, i, 0)),
        out_shape=jax.ShapeDtypeStruct(xa.shape, F32),
        input_output_aliases={2: 0},
        compiler_params=_cparams("parallel", "parallel"),
        name="out_proj",
    )(o, w, xa, mods)


def _ffn_kernel(ng_ref, wg_ref, wu_ref, w2_ref, x_ref, mc_ref, ml_ref, y_ref, acc_ref, h_ref, *,
                n_ctx, k_shift, k_scale, k_gate):
    i = pl.program_id(1)
    f = pl.program_id(2)
    bm = acc_ref.shape[0]

    def row_mod(k):
        row = i * bm + lax.broadcasted_iota(jnp.int32, (bm, 1), 0)
        return jnp.where(row < n_ctx, mc_ref[k], ml_ref[k])

    @pl.when(f == 0)
    def _():
        acc_ref[...] = jnp.zeros_like(acc_ref)
        h_ref[...] = _norm_mod(x_ref[0], ng_ref[...], row_mod(k_shift), row_mod(k_scale)).astype(BF16)

    h = h_ref[...]
    g = _dot(h, wg_ref[...])
    u = _dot(h, wu_ref[...])
    act = (g * (1.0 / (1.0 + jnp.exp(-g))) * u).astype(BF16)
    acc_ref[...] += _dot(act, w2_ref[...])

    @pl.when(f == pl.num_programs(2) - 1)
    def _():
        y_ref[0] = x_ref[0] + row_mod(k_gate) * acc_ref[...]


def ffn_residual(norm_gain, w13, w2, xa, mods, k_shift, k_scale, k_gate, n_ctx):
    b, ta, d = xa.shape
    ff = w2.shape[0]
    bm, fk = FFN_ROW_TILE, FF_CHUNK
    nf = ff // fk
    kern = functools.partial(_ffn_kernel, n_ctx=n_ctx, k_shift=k_shift, k_scale=k_scale, k_gate=k_gate)
    in_specs = [
        pl.BlockSpec((1, d), lambda b_, i, f: (0, 0)),
        pl.BlockSpec((d, fk), lambda b_, i, f: (0, f)),
        pl.BlockSpec((d, fk), lambda b_, i, f: (0, nf + f)),
        pl.BlockSpec((fk, d), lambda b_, i, f: (f, 0)),
        pl.BlockSpec((1, bm, d), lambda b_, i, f: (b_, i, 0)),
        pl.BlockSpec((None, None, 6, 1, d), lambda b_, i, f: (b_, 0, 0, 0, 0)),
        pl.BlockSpec((None, None, 6, 1, d), lambda b_, i, f: (b_, 1, 0, 0, 0)),
    ]
    return pl.pallas_call(
        kern,
        grid=(b, ta // bm, nf),
        in_specs=in_specs,
        out_specs=pl.BlockSpec((1, bm, d), lambda b_, i, f: (b_, i, 0)),
        out_shape=jax.ShapeDtypeStruct(xa.shape, F32),
        scratch_shapes=[pltpu.VMEM((bm, d), F32), pltpu.VMEM((bm, d), BF16)],
        input_output_aliases={4: 0},
        compiler_params=_cparams("parallel", "parallel", "arbitrary"),
        name="ffn",
    )(norm_gain.reshape(1, d), w13, w13, w2, xa, mods, mods)


MOE_SLOT_TILE = 512
MOE_GATHER_CHUNK = 512
MOE_COMBINE_CHUNK = 768
_VALID, _FIRST, _LAST = 1, 2, 4


def _one_hot_bf16(cond):
    return jnp.where(cond, 1.0, 0.0).astype(BF16)


def _moe_gather_kernel(tile_ref, chunk_ref, flag_ref, src_ref, h_ref, o_ref, acc_ref):
    k = pl.program_id(0)
    fl = flag_ref[k]

    @pl.when((fl & _FIRST) != 0)
    def _():
        acc_ref[...] = jnp.zeros_like(acc_ref)

    @pl.when((fl & _VALID) != 0)
    def _():
        bm, ch = acc_ref.shape[0], h_ref.shape[0]
        tok = chunk_ref[k] * ch + lax.broadcasted_iota(jnp.int32, (bm, ch), 1)
        acc_ref[...] += _dot(_one_hot_bf16(src_ref[...] == tok), h_ref[...])

    @pl.when((fl & _LAST) != 0)
    def _():
        o_ref[...] = acc_ref[...].astype(BF16)


def _moe_ffn_kernel(te_ref, nu_ref, xs_ref, wg_ref, wu_ref, w2_ref, gs_ref, y_ref, acc_ref):
    i = pl.program_id(0)
    f = pl.program_id(1)

    @pl.when(i < nu_ref[0])
    def _():
        @pl.when(f == 0)
        def _():
            acc_ref[...] = jnp.zeros_like(acc_ref)

        xs = xs_ref[...]
        g = _dot(xs, wg_ref[...])
        u = _dot(xs, wu_ref[...])
        act = (g * (1.0 / (1.0 + jnp.exp(-g))) * u).astype(BF16)
        acc_ref[...] += _dot(act, w2_ref[...])

        @pl.when(f == pl.num_programs(1) - 1)
        def _():
            y_ref[...] = (acc_ref[...] * gs_ref[...]).astype(BF16)


def _moe_combine_kernel(chunk_ref, tile_ref, flag_ref, p1_ref, p2_ref, y_ref, x_ref, gc_ref, gl_ref,
                        o_ref, acc_ref, *, n_ctx, chunks_per_batch):
    k = pl.program_id(0)
    fl = flag_ref[k]
    ch, bm = acc_ref.shape[0], y_ref.shape[0]

    @pl.when((fl & _FIRST) != 0)
    def _():
        acc_ref[...] = jnp.zeros_like(acc_ref)

    @pl.when((fl & _VALID) != 0)
    def _():
        slot = tile_ref[k] * bm + lax.broadcasted_iota(jnp.int32, (ch, bm), 1)
        w = _one_hot_bf16((p1_ref[...] == slot) | (p2_ref[...] == slot))
        acc_ref[...] += _dot(w, y_ref[...])

    @pl.when((fl & _LAST) != 0)
    def _():
        row = (chunk_ref[k] % chunks_per_batch) * ch + lax.broadcasted_iota(jnp.int32, (ch, 1), 0)
        gate = jnp.where(row < n_ctx, gc_ref[...], gl_ref[...])
        o_ref[...] = x_ref[...] + gate * acc_ref[...]


def _expand_items(counts, n_items):
    off = jnp.cumsum(counts)
    total = off[-1]
    k = jnp.minimum(jnp.arange(n_items, dtype=jnp.int32), total - 1)
    grp = jnp.minimum(jnp.searchsorted(off, k, side="right").astype(jnp.int32), counts.shape[0] - 1)
    local = k - (off[grp] - counts[grp])
    valid = jnp.arange(n_items, dtype=jnp.int32) < total
    return grp, local, valid


def moe_residual(h, route, w13, w2, xa, mods, k_gate, n_ctx, ctx_active):
    b, ta, d = xa.shape
    t = b * ta
    ne, ff = w2.shape[0], w2.shape[1]
    bm, gch, cch, fk = MOE_SLOT_TILE, MOE_GATHER_CHUNK, MOE_COMBINE_CHUNK, FF_CHUNK
    nf = ff // fk
    nt = (2 * t) // bm + ne
    ns = nt * bm
    i32 = jnp.int32

    r = route.reshape(t, LANES)
    e1, e2 = r[:, 0].astype(i32), r[:, 1].astype(i32)
    g1, g2 = r[:, 2], r[:, 3]
    tok = jnp.arange(t, dtype=i32)
    active = jnp.ones((t,), bool) if ctx_active else (tok % ta) >= n_ctx

    eid = jnp.arange(ne, dtype=i32)[:, None]
    member = ((e1[None] == eid) | (e2[None] == eid)) & active[None]
    csum = jnp.cumsum(member.astype(i32), axis=1)
    cap = (csum[:, -1] + bm - 1) // bm * bm
    end = jnp.cumsum(cap)
    start = end - cap
    n_used = (end[-1] // bm).astype(i32)

    def slot_of(e_sel):
        rank = jnp.take_along_axis(csum, e_sel[None], axis=0)[0] - 1
        return jnp.where(active, start[e_sel] + rank, -1)

    pos1, pos2 = slot_of(e1), slot_of(e2)
    drop1, drop2 = jnp.where(pos1 >= 0, pos1, ns), jnp.where(pos2 >= 0, pos2, ns)
    src = jnp.full((ns,), -1, i32).at[drop1].set(tok, mode="drop").at[drop2].set(tok, mode="drop")
    gslot = jnp.zeros((ns,), F32).at[drop1].set(g1, mode="drop").at[drop2].set(g2, mode="drop")
    tile_ids = jnp.arange(nt, dtype=i32)
    tile_expert = jnp.minimum(jnp.sum(end[None] <= (tile_ids * bm)[:, None], axis=1), ne - 1).astype(i32)

    src_t = src.reshape(nt, bm)
    c_lo = src_t[:, 0] // gch
    c_hi = jnp.max(src_t, axis=1) // gch
    n_ch = jnp.where(tile_ids < n_used, c_hi - c_lo + 1, 0)
    ni_g = ne * (t // gch) + nt
    g_tile, g_local, g_valid = _expand_items(n_ch, ni_g)
    g_chunk = c_lo[g_tile] + g_local
    g_flag = g_valid * (_VALID + _FIRST * (g_local == 0) + _LAST * (g_local == n_ch[g_tile] - 1))

    xs = pl.pallas_call(
        _moe_gather_kernel,
        grid_spec=pltpu.PrefetchScalarGridSpec(
            num_scalar_prefetch=3,
            grid=(ni_g,),
            in_specs=[
                pl.BlockSpec((bm, 1), lambda k, tl, cn, fl: (tl[k], 0)),
                pl.BlockSpec((gch, d), lambda k, tl, cn, fl: (cn[k], 0)),
            ],
            out_specs=pl.BlockSpec((bm, d), lambda k, tl, cn, fl: (tl[k], 0)),
            scratch_shapes=[pltpu.VMEM((bm, d), F32)],
        ),
        out_shape=jax.ShapeDtypeStruct((ns, d), BF16),
        compiler_params=_cparams("arbitrary"),
        name="moe_gather",
    )(g_tile, g_chunk, g_flag.astype(i32), src.reshape(ns, 1), h.reshape(t, d))

    def last_used(i, nu):
        return jnp.minimum(i, nu[0] - 1)

    def f_eff(i, f, nu):
        return jnp.where(i < nu[0], f, nf - 1)

    y = pl.pallas_call(
        _moe_ffn_kernel,
        grid_spec=pltpu.PrefetchScalarGridSpec(
            num_scalar_prefetch=2,
            grid=(nt, nf),
            in_specs=[
                pl.BlockSpec((bm, d), lambda i, f, te, nu: (last_used(i, nu), 0)),
                pl.BlockSpec((None, d, fk), lambda i, f, te, nu: (te[i], 0, f_eff(i, f, nu))),
                pl.BlockSpec((None, d, fk), lambda i, f, te, nu: (te[i], 0, nf + f_eff(i, f, nu))),
                pl.BlockSpec((None, fk, d), lambda i, f, te, nu: (te[i], f_eff(i, f, nu), 0)),
                pl.BlockSpec((bm, 1), lambda i, f, te, nu: (last_used(i, nu), 0)),
            ],
            out_specs=pl.BlockSpec((bm, d), lambda i, f, te, nu: (last_used(i, nu), 0)),
            scratch_shapes=[pltpu.VMEM((bm, d), F32)],
        ),
        out_shape=jax.ShapeDtypeStruct((ns, d), BF16),
        compiler_params=_cparams("arbitrary", "arbitrary"),
        name="moe_ffn",
    )(tile_expert, n_used.reshape(1), xs, w13, w13, w2, gslot.reshape(ns, 1))

    nc = t // cch
    cs0 = jnp.concatenate([jnp.zeros((ne, 1), i32), csum], axis=1)[:, ::cch]
    before, through = cs0[:, :-1].T, cs0[:, 1:].T
    t_lo = (start[None] + before) // bm
    t_hi = (start[None] + through - 1) // bm
    n_tl = jnp.where(through > before, t_hi - t_lo + 1, 0).reshape(-1)
    ni_c = ne * nc + nt
    c_pair, c_local, c_valid = _expand_items(n_tl, ni_c)
    c_chunk = c_pair // ne
    c_tile = t_lo.reshape(-1)[c_pair] + c_local
    off = jnp.cumsum(n_tl)
    k_eff = jnp.minimum(jnp.arange(ni_c, dtype=i32), off[-1] - 1)
    chunk_first = (off - n_tl)[c_chunk * ne]
    chunk_last = off[c_chunk * ne + ne - 1] - 1
    c_flag = c_valid * (_VALID + _FIRST * (k_eff == chunk_first) + _LAST * (k_eff == chunk_last))

    cpb = ta // cch
    kern = functools.partial(_moe_combine_kernel, n_ctx=n_ctx, chunks_per_batch=cpb)
    out = pl.pallas_call(
        kern,
        grid_spec=pltpu.PrefetchScalarGridSpec(
            num_scalar_prefetch=3,
            grid=(ni_c,),
            in_specs=[
                pl.BlockSpec((cch, 1), lambda k, cn, tl, fl: (cn[k], 0)),
                pl.BlockSpec((cch, 1), lambda k, cn, tl, fl: (cn[k], 0)),
                pl.BlockSpec((bm, d), lambda k, cn, tl, fl: (tl[k], 0)),
                pl.BlockSpec((cch, d), lambda k, cn, tl, fl: (cn[k], 0)),
                pl.BlockSpec((None, None, None, 1, d), lambda k, cn, tl, fl: (cn[k] // cpb, 0, k_gate, 0, 0)),
                pl.BlockSpec((None, None, None, 1, d), lambda k, cn, tl, fl: (cn[k] // cpb, 1, k_gate, 0, 0)),
            ],
            out_specs=pl.BlockSpec((cch, d), lambda k, cn, tl, fl: (cn[k], 0)),
            scratch_shapes=[pltpu.VMEM((cch, d), F32)],
        ),
        out_shape=jax.ShapeDtypeStruct((t, d), F32),
        input_output_aliases={6: 0},
        compiler_params=_cparams("arbitrary"),
        name="moe_combine",
    )(c_chunk, c_tile, c_flag.astype(i32), pos1.reshape(t, 1), pos2.reshape(t, 1), y, xa.reshape(t, d), mods, mods)
    return out.reshape(b, ta, d)


def _rope_tables(seq, n_ctx, rot_dim):
    rows = seq // GRID_W
    quarter = rot_dim // 4
    inv_freq = ROPE_THETA ** (-jnp.arange(quarter, dtype=F32) / quarter)
    row = jnp.repeat(jnp.arange(rows, dtype=F32), GRID_W)
    col = jnp.tile(jnp.arange(GRID_W, dtype=F32), rows)
    ang = jnp.concatenate([row[:, None] * inv_freq, col[:, None] * inv_freq], axis=-1)
    cos, sin = jnp.cos(ang), jnp.sin(ang)
    reps = LANES // rot_dim
    c = jnp.tile(jnp.concatenate([cos, cos], axis=-1), (1, reps))
    s = jnp.tile(jnp.concatenate([-sin, sin], axis=-1), (1, reps))
    c = jnp.concatenate([jnp.ones((n_ctx, LANES), F32), c], axis=0)
    s = jnp.concatenate([jnp.zeros((n_ctx, LANES), F32), s], axis=0)
    return c, s


def kernel(x, c, ctx, c_ctx, ada_w, ada_b, norm_mix, norm_ffn, gqa_wqkv, gqa_q_gain, gqa_k_gain, gqa_wo, mla_wdown, mla_qa_gain, mla_kva_gain, mla_wuq, mla_wukv, mla_q_gain, mla_k_gain, mla_wo, win_wqkv, win_q_gain, win_k_gain, win_sink, win_wo, diff_wqkv, diff_q_gain, diff_k_gain, diff_lambda, diff_subln, diff_wo, ffn_w13, ffn_w2, moe_router, moe_w13, moe_w2):
    b, seq, d = x.shape
    n_ctx = ctx.shape[1]
    depth = ada_w.shape[0]
    ones = lambda n: jnp.ones((n,), F32)
    zeros = lambda n: jnp.zeros((n,), F32)

    crows = -(-(b + 1) // 8) * 8
    cond = jnp.zeros((crows, d), F32).at[:b].set(c).at[b].set(c_ctx)
    mods_all = adaln(cond, ada_w, ada_b)

    xa = jnp.concatenate([ctx, x], axis=1)

    for i in range(depth):
        need_ctx = i < depth - 1
        j = i // 4
        kind = i % 4
        ml = mods_all[i, :b].reshape(b, 6, 1, d)
        mc = jnp.broadcast_to(mods_all[i, b].reshape(1, 6, 1, d), (b, 6, 1, d))
        mods = jnp.stack([mc, ml], axis=1)

        prenorm = (norm_mix[i], mods, 0, 1, n_ctx)
        if kind == 0:
            hd = GQA_HEAD_DIM
            scale = hd ** -0.5 * LOG2E
            nq, nk = GQA_HEADS * hd, GQA_KV_HEADS * hd
            groups = [(g * hd, hd) for g in range(GQA_HEADS + GQA_KV_HEADS)]
            gain = jnp.concatenate([jnp.tile(gqa_q_gain[j] * scale, GQA_HEADS),
                                    jnp.tile(gqa_k_gain[j], GQA_KV_HEADS), ones(nk)])
            qkv = proj(xa, 0, gqa_wqkv[j].astype(BF16), groups, gain, _rope_tables(seq, n_ctx, hd),
                       range((nq + nk) // LANES), hd // 2, "gqa_qkv", prenorm)
            o = gqa_attention(qkv, n_ctx)
            w_o = gqa_wo[j]
        elif kind == 1:
            scale = (MLA_NOPE + MLA_ROPE) ** -0.5 * LOG2E
            hh = MLA_HEADS
            wd = mla_wdown[j]
            pad = jnp.zeros((d, LANES - MLA_ROPE), F32)
            wd2 = jnp.concatenate([wd[:, :MLA_Q_LORA], wd[:, MLA_Q_LORA + MLA_KV_LORA:], pad,
                                   wd[:, MLA_Q_LORA:MLA_Q_LORA + MLA_KV_LORA]], axis=1)
            kpe0 = MLA_Q_LORA
            ckv0 = MLA_Q_LORA + LANES
            groups = [(0, MLA_Q_LORA), (kpe0, MLA_ROPE), (ckv0, MLA_KV_LORA)]
            gain = jnp.concatenate([mla_qa_gain[j], mla_k_gain[j][MLA_NOPE:], zeros(LANES - MLA_ROPE),
                                    mla_kva_gain[j]])
            rope = _rope_tables(seq, n_ctx, MLA_ROPE)
            dn = proj(xa, 0, wd2.astype(BF16), groups, gain, rope, [kpe0 // LANES], MLA_ROPE // 2, "mla_down",
                      prenorm)
            wq = mla_wuq[j].reshape(MLA_Q_LORA, hh, MLA_NOPE + MLA_ROPE)
            wq2 = jnp.pad(wq, ((0, 0), (0, 0), (0, LANES - MLA_ROPE))).reshape(MLA_Q_LORA, hh * 2 * LANES)
            groups = ([(g * 2 * LANES, MLA_NOPE) for g in range(hh)]
                      + [(g * 2 * LANES + MLA_NOPE, MLA_ROPE) for g in range(hh)])
            qg = jnp.concatenate([mla_q_gain[j] * scale, zeros(LANES - MLA_ROPE)])
            gain = jnp.tile(qg, hh)
            q = proj(dn, 0, wq2.astype(BF16), groups, gain, rope, range(1, 2 * hh, 2), MLA_ROPE // 2, "mla_q")
            groups = [(g * (MLA_NOPE + MLA_V), MLA_NOPE) for g in range(hh)]
            gain = jnp.tile(jnp.concatenate([mla_k_gain[j][:MLA_NOPE], ones(MLA_V)]), hh)
            kv = proj(dn, ckv0 // MLA_KV_LORA, mla_wukv[j].astype(BF16), groups, gain, rope, [], MLA_ROPE // 2, "mla_kv")
            o = mla_attention(q, kv, dn, kpe0 // LANES, n_ctx)
            w_o = mla_wo[j]
        elif kind == 2:
            hd = WIN_HEAD_DIM
            scale = hd ** -0.5 * LOG2E
            n_pairs = WIN_HEADS // WIN_KV_HEADS
            nq = WIN_HEADS * hd
            perm = np.array([(kv * n_pairs + g) * hd + t for g in range(n_pairs)
                             for kv in range(WIN_KV_HEADS) for t in range(hd)])
            wq = win_wqkv[j]
            w2_ = jnp.concatenate([wq[:, perm], wq[:, nq:]], axis=1)
            groups = [(g * hd, hd) for g in range(WIN_HEADS + WIN_KV_HEADS)]
            gain = jnp.concatenate([jnp.tile(win_q_gain[j] * scale, WIN_HEADS),
                                    jnp.tile(win_k_gain[j], WIN_KV_HEADS), ones(WIN_KV_HEADS * hd)])
            qkv = proj(xa, 0, w2_.astype(BF16), groups, gain, _rope_tables(seq, n_ctx, hd),
                       range(n_pairs + 1), hd // 2, "win_qkv", prenorm)
            o = win_attention(qkv, win_sink[j] * LOG2E, n_ctx)
            w_o = win_wo[j][perm, :]
        else:
            hd = DIFF_HEAD_DIM
            scale = hd ** -0.5 * LOG2E
            lam_init = 0.8 - 0.6 * math.exp(-0.3 * i)
            nqk = 2 * DIFF_HEADS
            groups = [(g * hd, hd) for g in range(2 * nqk)]
            gain = jnp.concatenate([jnp.tile(diff_q_gain[j] * scale, nqk), jnp.tile(diff_k_gain[j], nqk),
                                    ones(DIFF_HEADS * 2 * hd)])
            qkv = proj(xa, 0, diff_wqkv[j].astype(BF16), groups, gain, _rope_tables(seq, n_ctx, hd),
                       range(2 * DIFF_HEADS), hd // 2, "diff_qkv", prenorm)
            o = diff_attention(qkv, diff_lambda[j], diff_subln[j], lam_init, n_ctx)
            w_o = diff_wo[j]
        xa = out_proj_residual(o, w_o.astype(BF16), xa, mods, 2, n_ctx)

        jj = i // 2
        if i % 2 == 0:
            xa = ffn_residual(norm_ffn[i], ffn_w13[jj].astype(BF16), ffn_w2[jj].astype(BF16), xa, mods, 3, 4, 5,
                              n_ctx)
        else:
            h, route = norm_mod(xa, norm_ffn[i], mods, 3, 4, n_ctx, router=moe_router[jj])
            xa = moe_residual(h, route, moe_w13[jj].astype(BF16), moe_w2[jj].astype(BF16), xa, mods, 5,
                              n_ctx, need_ctx)
    return xa[:, n_ctx:]
```

```python
import functools
import math

import jax
import jax.numpy as jnp
import numpy as np
from jax import lax
from jax.experimental import pallas as pl
from jax.experimental.pallas import tpu as pltpu

GRID_W = 64
ROPE_THETA = 10000.0
EPS = 1e-6
NEG_INF = -1e30
LANES = 128
MXU_WIDTH = 256
ROW_TILE = 256
FFN_ROW_TILE = 768
FF_CHUNK = 512
VMEM_LIMIT = 56 * 1024 * 1024

GQA_HEADS, GQA_KV_HEADS, GQA_HEAD_DIM = 8, 2, 128
MLA_HEADS, MLA_Q_LORA, MLA_KV_LORA, MLA_NOPE, MLA_ROPE, MLA_V = 8, 384, 256, 128, 64, 128
WIN_HEADS, WIN_KV_HEADS, WIN_HEAD_DIM, WINDOW = 16, 2, 64, 128
DIFF_HEADS, DIFF_HEAD_DIM = 8, 64
N_EXPERTS = 8

F32 = jnp.float32
BF16 = jnp.bfloat16


def _cparams(*sem):
    return pltpu.CompilerParams(dimension_semantics=sem, vmem_limit_bytes=VMEM_LIMIT)


def _dot(a, b):
    return jnp.dot(a, b, preferred_element_type=F32)


def _dot_nt(a, b):
    return lax.dot_general(a, b, (((1,), (1,)), ((), ())), preferred_element_type=F32)


def _split_bf16(v):
    hi = v.astype(BF16)
    lo = (v - hi.astype(F32)).astype(BF16)
    return hi, lo


def _adaln_kernel(c_ref, w_ref, b_ref, o_ref):
    c = c_ref[...]
    sc = c * (1.0 / (1.0 + jnp.exp(-c)))
    o_ref[0] = jnp.dot(sc, w_ref[0], preferred_element_type=F32,
                       precision=lax.Precision.HIGHEST) + b_ref[0]


def adaln(cond, ada_w, ada_b):
    depth, d, n = ada_w.shape
    rows = cond.shape[0]
    bn = 1536
    return pl.pallas_call(
        _adaln_kernel,
        grid=(depth, n // bn),
        in_specs=[
            pl.BlockSpec((rows, d), lambda l, j: (0, 0)),
            pl.BlockSpec((1, d, bn), lambda l, j: (l, 0, j)),
            pl.BlockSpec((1, 1, bn), lambda l, j: (l, 0, j)),
        ],
        out_specs=pl.BlockSpec((1, rows, bn), lambda l, j: (l, 0, j)),
        out_shape=jax.ShapeDtypeStruct((depth, rows, n), F32),
        compiler_params=_cparams("parallel", "parallel"),
        name="adaln",
    )(cond, ada_w, ada_b.reshape(depth, 1, n))


def _mod_spec(k, n_ctx_tiles, d):
    return pl.BlockSpec((None, None, None, 1, d),
                        lambda b, i, *_: (b, jnp.minimum(i // n_ctx_tiles, 1), k, 0, 0))


def _norm_mod(x, g, shift, scale):
    ms = jnp.mean(x * x, axis=-1, keepdims=True)
    return x * lax.rsqrt(ms + EPS) * g * (1.0 + scale) + shift


def _norm_mod_router_kernel(x_ref, g_ref, sh_ref, sc_ref, r_ref, h_ref, cw_ref):
    d = x_ref.shape[2]
    h = _norm_mod(x_ref[0], g_ref[...], sh_ref[...], sc_ref[...])
    h_ref[0, :, :d] = h.astype(BF16)
    logits = jnp.dot(h, r_ref[...], preferred_element_type=F32, precision=lax.Precision.HIGHEST)
    lane = lax.broadcasted_iota(jnp.int32, logits.shape, 1)
    logits = jnp.where(lane < N_EXPERTS, logits, -jnp.inf)
    m1 = jnp.max(logits, axis=-1, keepdims=True)
    i1 = jnp.min(jnp.where(logits == m1, lane, LANES), axis=-1, keepdims=True)
    rest = jnp.where(lane == i1, -jnp.inf, logits)
    m2 = jnp.max(rest, axis=-1, keepdims=True)
    i2 = jnp.min(jnp.where(rest == m2, lane, LANES), axis=-1, keepdims=True)
    e2 = jnp.exp(m2 - m1)
    den = 1.0 + e2
    g1, g2 = 1.0 / den, e2 / den
    cw_ref[0] = jnp.where(lane == 0, i1.astype(F32), jnp.where(lane == 1, i2.astype(F32),
                          jnp.where(lane == 2, g1, jnp.where(lane == 3, g2, 0.0))))
    g1_hi, g2_hi = g1.astype(BF16).astype(F32), g2.astype(BF16).astype(F32)
    ext = jnp.where(lane == ROUTE_G1, g1_hi, jnp.where(lane == ROUTE_G1 + 1, g1 - g1_hi,
          jnp.where(lane == ROUTE_G2, g2_hi, jnp.where(lane == ROUTE_G2 + 1, g2 - g2_hi,
          jnp.where(lane == ROUTE_E1, i1.astype(F32), jnp.where(lane == ROUTE_E2, i2.astype(F32), 0.0))))))
    h_ref[0, :, d:] = ext.astype(BF16)


ROUTE_G1, ROUTE_G2, ROUTE_E1, ROUTE_E2 = 0, 2, 4, 5


def norm_mod_router(xa, gain, mods, k_shift, k_scale, n_ctx, router):
    b, ta, d = xa.shape
    bm = ROW_TILE
    nct = n_ctx // bm
    rpad = jnp.zeros((d, LANES), F32).at[:, :N_EXPERTS].set(router)
    return pl.pallas_call(
        _norm_mod_router_kernel, grid=(b, ta // bm),
        in_specs=[
            pl.BlockSpec((1, bm, d), lambda b_, i: (b_, i, 0)),
            pl.BlockSpec((1, d), lambda b_, i: (0, 0)),
            _mod_spec(k_shift, nct, d),
            _mod_spec(k_scale, nct, d),
            pl.BlockSpec((d, LANES), lambda b_, i: (0, 0)),
        ],
        out_specs=[pl.BlockSpec((1, bm, d + LANES), lambda b_, i: (b_, i, 0)),
                   pl.BlockSpec((1, bm, LANES), lambda b_, i: (b_, i, 0))],
        out_shape=[jax.ShapeDtypeStruct((b, ta, d + LANES), BF16), jax.ShapeDtypeStruct((b, ta, LANES), F32)],
        compiler_params=_cparams("parallel", "parallel"), name="norm_mod_router",
    )(xa, gain.reshape(1, d), mods, mods, rpad)


def _proj_kernel(*refs, rope_blocks, rope_half, prenorm, norm_blocks):
    if prenorm:
        x_ref, g_ref, sh_ref, sc_ref = refs[:4]
        h = _norm_mod(x_ref[0], g_ref[...], sh_ref[...], sc_ref[...]).astype(BF16)
        refs = refs[4:]
    else:
        h = refs[0][0]
        refs = refs[1:]
    y = _dot(h, refs[0][...])
    n = y.shape[1]
    s_hi, s_lo = _split_bf16(y * y)
    if norm_blocks is None:
        _, e_ref, et_ref, igs_ref, u_ref, gain_ref, c_ref, s_ref, o_ref = refs
        ss = _dot(s_hi, e_ref[...]) + _dot(s_lo, e_ref[...])
        inv = lax.rsqrt(ss * igs_ref[...] + EPS)
        i_hi, i_lo = _split_bf16(inv)
        fac = _dot(i_hi, et_ref[...]) + _dot(i_lo, et_ref[...]) + u_ref[...]
        z = y * (fac * gain_ref[...])
    else:
        _, j_ref, igs_ref, u_ref, gain_ref, c_ref, s_ref, o_ref = refs
        zs = []
        for blk in range(n // MXU_WIDTH):
            sl = slice(blk * MXU_WIDTH, (blk + 1) * MXU_WIDTH)
            if blk in norm_blocks:
                ss = _dot(s_hi[:, sl], j_ref[blk]) + _dot(s_lo[:, sl], j_ref[blk])
                u = u_ref[:, sl]
                fac = lax.rsqrt(ss * igs_ref[:, sl] + EPS) * (1.0 - u) + u
                zs.append(y[:, sl] * (fac * gain_ref[:, sl]))
            else:
                zs.append(y[:, sl] * gain_ref[:, sl])
        z = jnp.concatenate(zs, axis=1)
    cos = c_ref[...]
    sin = s_ref[...]
    if rope_half == LANES // 2:
        first = None
    else:
        lane = lax.broadcasted_iota(jnp.int32, cos.shape, 1)
        first = (lane % (2 * rope_half)) < rope_half
    for blk in range(n // LANES):
        zb = z[:, blk * LANES:(blk + 1) * LANES]
        if blk in rope_blocks:
            if first is None:
                rot = pltpu.roll(zb, LANES // 2, 1)
            else:
                rot = jnp.where(first, pltpu.roll(zb, LANES - rope_half, 1), pltpu.roll(zb, rope_half, 1))
            zb = zb * cos + rot * sin
        o_ref[0, :, blk * LANES:(blk + 1) * LANES] = zb.astype(BF16)


def proj(h, kin_block, w, groups, gain, rope_tabs, rope_blocks, rope_half, name, prenorm=None):
    b, ta, _ = h.shape
    k, n = w.shape
    bm = ROW_TILE
    mw = MXU_WIDTH
    u = np.ones((1, n), np.float32)
    for start, size in groups:
        u[0, start:start + size] = 0.0
    local = n % mw == 0 and all(start // mw == (start + size - 1) // mw for start, size in groups)
    if local:
        jm = np.zeros((n // mw, mw, mw), np.float32)
        igs = np.zeros((1, n), np.float32)
        for start, size in groups:
            blk, o = divmod(start, mw)
            jm[blk, o:o + size, o:o + size] = 1.0
            igs[0, start:start + size] = 1.0 / size
        norm_blocks = frozenset(start // mw for start, _ in groups)
        norm_args = [jnp.asarray(jm, BF16), jnp.asarray(igs)]
        norm_specs = [pl.BlockSpec((n // mw, mw, mw), lambda b_, i: (0, 0, 0)),
                      pl.BlockSpec((1, n), lambda b_, i: (0, 0))]
    else:
        e = np.zeros((n, LANES), np.float32)
        igs = np.zeros((1, LANES), np.float32)
        for gi, (start, size) in enumerate(groups):
            e[start:start + size, gi] = 1.0
            igs[0, gi] = 1.0 / size
        norm_blocks = None
        norm_args = [jnp.asarray(e, BF16), jnp.asarray(e.T, BF16), jnp.asarray(igs)]
        norm_specs = [pl.BlockSpec((n, LANES), lambda b_, i: (0, 0)),
                      pl.BlockSpec((LANES, n), lambda b_, i: (0, 0)),
                      pl.BlockSpec((1, LANES), lambda b_, i: (0, 0))]
    cos, sin = rope_tabs
    kern = functools.partial(_proj_kernel, rope_blocks=frozenset(rope_blocks), rope_half=rope_half,
                             prenorm=prenorm is not None, norm_blocks=norm_blocks)
    lead_specs = [pl.BlockSpec((1, bm, k), lambda b_, i: (b_, i, kin_block))]
    lead_args = [h]
    if prenorm is not None:
        ngain, mods, k_shift, k_scale, n_ctx = prenorm
        nct = n_ctx // bm
        lead_specs += [pl.BlockSpec((1, k), lambda b_, i: (0, 0)), _mod_spec(k_shift, nct, k),
                       _mod_spec(k_scale, nct, k)]
        lead_args += [ngain.reshape(1, k), mods, mods]
    return pl.pallas_call(
        kern,
        grid=(b, ta // bm),
        in_specs=lead_specs + [pl.BlockSpec((k, n), lambda b_, i: (0, 0))] + norm_specs + [
            pl.BlockSpec((1, n), lambda b_, i: (0, 0)),
            pl.BlockSpec((1, n), lambda b_, i: (0, 0)),
            pl.BlockSpec((bm, LANES), lambda b_, i: (i, 0)),
            pl.BlockSpec((bm, LANES), lambda b_, i: (i, 0)),
        ],
        out_specs=pl.BlockSpec((1, bm, n), lambda b_, i: (b_, i, 0)),
        out_shape=jax.ShapeDtypeStruct((b, ta, n), BF16),
        compiler_params=_cparams("parallel", "parallel"),
        name=name,
    )(*lead_args, w, *norm_args, jnp.asarray(u), gain.reshape(1, n).astype(F32), cos, sin)


LOG2E = math.log2(math.e)


def _softmax_pv(s, v):
    m = jnp.max(s, axis=-1, keepdims=True)
    p = jnp.exp2(s - m)
    l = jnp.sum(p, axis=-1, keepdims=True)
    return _dot(p.astype(BF16), v) / l


def _gqa_attn_kernel(q_ref, k_ref, v_ref, o_ref, *, n_ctx, group, hd):
    i = pl.program_id(2)

    def attend(nk):
        k = k_ref[0, :nk, :]
        v = v_ref[0, :nk, :]
        for g in range(group):
            q = q_ref[0, :, g * hd:(g + 1) * hd]
            o = _softmax_pv(_dot_nt(q, k), v)
            o_ref[0, :, g * hd:(g + 1) * hd] = o.astype(BF16)

    nct = n_ctx // ROW_TILE

    @pl.when(i < nct)
    def _():
        attend(n_ctx)

    @pl.when(i >= nct)
    def _():
        attend(k_ref.shape[1])


def gqa_attention(qkv, n_ctx):
    b, ta, _ = qkv.shape
    hd, group, kvh = GQA_HEAD_DIM, GQA_HEADS // GQA_KV_HEADS, GQA_KV_HEADS
    bq = ROW_TILE
    qw = group * hd
    kb = GQA_HEADS
    vb = GQA_HEADS + kvh
    kern = functools.partial(_gqa_attn_kernel, n_ctx=n_ctx, group=group, hd=hd)
    return pl.pallas_call(
        kern,
        grid=(b, kvh, ta // bq),
        in_specs=[
            pl.BlockSpec((1, bq, qw), lambda b_, h, i: (b_, i, h)),
            pl.BlockSpec((1, ta, hd), lambda b_, h, i: (b_, 0, kb + h)),
            pl.BlockSpec((1, ta, hd), lambda b_, h, i: (b_, 0, vb + h)),
        ],
        out_specs=pl.BlockSpec((1, bq, qw), lambda b_, h, i: (b_, i, h)),
        out_shape=jax.ShapeDtypeStruct((b, ta, GQA_HEADS * hd), BF16),
        compiler_params=_cparams("parallel", "parallel", "arbitrary"),
        name="gqa_attn",
    )(qkv, qkv, qkv)


def _mla_attn_kernel(q_ref, kn_ref, kp_ref, v_ref, o_ref, kcat_ref, *, n_ctx):
    i = pl.program_id(2)

    @pl.when(i == 0)
    def _():
        kcat_ref[:, :LANES] = kn_ref[0]
        kcat_ref[:, LANES:] = kp_ref[0]

    def attend(nk):
        s = _dot_nt(q_ref[0], kcat_ref[:nk, :])
        o_ref[0] = _softmax_pv(s, v_ref[0, :nk, :]).astype(BF16)

    nct = n_ctx // ROW_TILE

    @pl.when(i < nct)
    def _():
        attend(n_ctx)

    @pl.when(i >= nct)
    def _():
        attend(kn_ref.shape[1])


def mla_attention(q, kv, dn, kpe_block, n_ctx):
    b, ta, _ = q.shape
    bq = ROW_TILE
    hh = MLA_HEADS
    kern = functools.partial(_mla_attn_kernel, n_ctx=n_ctx)
    return pl.pallas_call(
        kern,
        grid=(b, hh, ta // bq),
        in_specs=[
            pl.BlockSpec((1, bq, 2 * LANES), lambda b_, h, i: (b_, i, h)),
            pl.BlockSpec((1, ta, LANES), lambda b_, h, i: (b_, 0, 2 * h)),
            pl.BlockSpec((1, ta, LANES), lambda b_, h, i: (b_, 0, kpe_block)),
            pl.BlockSpec((1, ta, LANES), lambda b_, h, i: (b_, 0, 2 * h + 1)),
        ],
        out_specs=pl.BlockSpec((1, bq, LANES), lambda b_, h, i: (b_, i, h)),
        out_shape=jax.ShapeDtypeStruct((b, ta, hh * MLA_V), BF16),
        scratch_shapes=[pltpu.VMEM((ta, 2 * LANES), BF16)],
        compiler_params=_cparams("parallel", "parallel", "arbitrary"),
        name="mla_attn",
    )(q, kv, dn, kv)


def _half_masks(shape):
    lane = lax.broadcasted_iota(jnp.int32, shape, 1)
    return lane < (LANES // 2)


def _diff_attn_kernel(q_ref, k_ref, v_ref, lam_ref, sub_ref, o_ref, *, n_ctx, lam_init):
    i = pl.program_id(2)
    lp = lam_ref[...]
    lam = (jnp.exp(jnp.sum(lp[0:1] * lp[1:2], axis=-1, keepdims=True))
           - jnp.exp(jnp.sum(lp[2:3] * lp[3:4], axis=-1, keepdims=True)) + lam_init)

    def attend(nk):
        q = q_ref[0]
        lo = _half_masks(q.shape)
        zero = jnp.zeros_like(q)
        k = k_ref[0, :nk, :]
        v = v_ref[0, :nk, :]
        def probs(s, weight):
            m = jnp.max(s, axis=-1, keepdims=True)
            e = jnp.exp2(s - m)
            return e * (weight / jnp.sum(e, axis=-1, keepdims=True))

        p = probs(_dot_nt(jnp.where(lo, q, zero), k), 1.0) - probs(_dot_nt(jnp.where(lo, zero, q), k), lam)
        o = _dot(p.astype(BF16), v)
        ms = jnp.mean(o * o, axis=-1, keepdims=True)
        o_ref[0] = (o * lax.rsqrt(ms + EPS) * sub_ref[...] * (1.0 - lam_init)).astype(BF16)

    nct = n_ctx // ROW_TILE

    @pl.when(i < nct)
    def _():
        attend(n_ctx)

    @pl.when(i >= nct)
    def _():
        attend(k_ref.shape[1])


def diff_attention(qkv, lam_p, subln, lam_init, n_ctx):
    b, ta, _ = qkv.shape
    bq = ROW_TILE
    hh = DIFF_HEADS
    kern = functools.partial(_diff_attn_kernel, n_ctx=n_ctx, lam_init=lam_init)
    return pl.pallas_call(
        kern,
        grid=(b, hh, ta // bq),
        in_specs=[
            pl.BlockSpec((1, bq, LANES), lambda b_, h, i: (b_, i, h)),
            pl.BlockSpec((1, ta, LANES), lambda b_, h, i: (b_, 0, hh + h)),
            pl.BlockSpec((1, ta, LANES), lambda b_, h, i: (b_, 0, 2 * hh + h)),
            pl.BlockSpec((4, DIFF_HEAD_DIM), lambda b_, h, i: (0, 0)),
            pl.BlockSpec((1, LANES), lambda b_, h, i: (0, 0)),
        ],
        out_specs=pl.BlockSpec((1, bq, LANES), lambda b_, h, i: (b_, i, h)),
        out_shape=jax.ShapeDtypeStruct((b, ta, hh * 2 * DIFF_HEAD_DIM), BF16),
        compiler_params=_cparams("parallel", "parallel", "arbitrary"),
        name="diff_attn",
    )(qkv, qkv, qkv, lam_p.astype(F32), subln.reshape(1, LANES).astype(F32))


def _win_attn_kernel(sink_ref, q_ref, k_ref, v_ref, o_ref, *, n_ctx, seq, n_pairs):
    i = pl.program_id(1)
    bq = q_ref.shape[1]
    band = bq + 2 * WINDOW
    nct = n_ctx // bq

    def head_out(qm, kv_idx, pair, blocks):
        sk = sink_ref[kv_idx * n_pairs + pair]
        ss = []
        m = None
        for k, _, mask in blocks:
            s = _dot_nt(qm, k)
            if mask is not None:
                s = jnp.where(mask, s, NEG_INF)
            ss.append(s)
            bm_ = jnp.max(s, axis=-1, keepdims=True)
            m = bm_ if m is None else jnp.maximum(m, bm_)
        m = jnp.maximum(m, sk)
        l = jnp.exp2(sk - m)
        o = None
        for s, (_, v, _) in zip(ss, blocks):
            p = jnp.exp2(s - m)
            l = l + jnp.sum(p, axis=-1, keepdims=True)
            pv = _dot(p.astype(BF16), v)
            o = pv if o is None else o + pv
        return o / l

    def run(blocks):
        for pair in range(n_pairs):
            q = q_ref[0, :, pair * LANES:(pair + 1) * LANES]
            lo = _half_masks(q.shape)
            zero = jnp.zeros_like(q)
            o0 = head_out(jnp.where(lo, q, zero), 0, pair, blocks)
            o1 = head_out(jnp.where(lo, zero, q), 1, pair, blocks)
            o_ref[0, :, pair * LANES:(pair + 1) * LANES] = jnp.where(lo, o0, o1).astype(BF16)

    @pl.when(i < nct)
    def _():
        run([(k_ref[0, :n_ctx, :], v_ref[0, :n_ctx, :], None)])

    @pl.when(i >= nct)
    def _():
        q0 = (i - nct) * bq
        start = jnp.clip(q0 - WINDOW, 0, seq - band)
        start = pl.multiple_of(start, WINDOW)
        qpos = q0 + lax.broadcasted_iota(jnp.int32, (bq, band), 0)
        kpos = start + lax.broadcasted_iota(jnp.int32, (bq, band), 1)
        mask = jnp.abs(qpos - kpos) <= WINDOW
        kb = k_ref[0, pl.ds(n_ctx + start, band), :]
        vb = v_ref[0, pl.ds(n_ctx + start, band), :]
        run([(k_ref[0, :n_ctx, :], v_ref[0, :n_ctx, :], None), (kb, vb, mask)])


def win_attention(qkv, sink, n_ctx):
    b, ta, _ = qkv.shape
    bq = ROW_TILE
    n_pairs = WIN_HEADS // WIN_KV_HEADS
    qw = n_pairs * LANES
    kern = functools.partial(_win_attn_kernel, n_ctx=n_ctx, seq=ta - n_ctx, n_pairs=n_pairs)
    return pl.pallas_call(
        kern,
        grid_spec=pltpu.PrefetchScalarGridSpec(
            num_scalar_prefetch=1,
            grid=(b, ta // bq),
            in_specs=[
                pl.BlockSpec((1, bq, qw), lambda b_, i, s: (b_, i, 0)),
                pl.BlockSpec((1, ta, LANES), lambda b_, i, s: (b_, 0, n_pairs)),
                pl.BlockSpec((1, ta, LANES), lambda b_, i, s: (b_, 0, n_pairs + 1)),
            ],
            out_specs=pl.BlockSpec((1, bq, qw), lambda b_, i, s: (b_, i, 0)),
        ),
        out_shape=jax.ShapeDtypeStruct((b, ta, qw), BF16),
        compiler_params=_cparams("parallel", "arbitrary"),
        name="win_attn",
    )(sink.astype(F32), qkv, qkv, qkv)


def _out_proj_kernel(o_ref, w_ref, x_ref, g_ref, y_ref):
    y_ref[0] = x_ref[0] + g_ref[...] * _dot(o_ref[0], w_ref[...])


def out_proj_residual(o, w, xa, mods, k_gate, n_ctx):
    b, ta, d = xa.shape
    kdim = o.shape[2]
    bm = ROW_TILE
    return pl.pallas_call(
        _out_proj_kernel,
        grid=(b, ta // bm),
        in_specs=[
            pl.BlockSpec((1, bm, kdim), lambda b_, i: (b_, i, 0)),
            pl.BlockSpec((kdim, d), lambda b_, i: (0, 0)),
            pl.BlockSpec((1, bm, d), lambda b_, i: (b_, i, 0)),
            _mod_spec(k_gate, n_ctx // bm, d),
        ],
        out_specs=pl.BlockSpec((1, bm, d), lambda b_, i: (b_, i, 0)),
        out_shape=jax.ShapeDtypeStruct(xa.shape, F32),
        input_output_aliases={2: 0},
        compiler_params=_cparams("parallel", "parallel"),
        name="out_proj",
    )(o, w, xa, mods)


def _ffn_kernel(ng_ref, wg_ref, wu_ref, w2_ref, x_ref, mc_ref, ml_ref, y_ref, acc_ref, h_ref, *,
                n_ctx, k_shift, k_scale, k_gate):
    i = pl.program_id(1)
    f = pl.program_id(2)
    bm = acc_ref.shape[0]

    def row_mod(k):
        row = i * bm + lax.broadcasted_iota(jnp.int32, (bm, 1), 0)
        return jnp.where(row < n_ctx, mc_ref[k], ml_ref[k])

    @pl.when(f == 0)
    def _():
        acc_ref[...] = jnp.zeros_like(acc_ref)
        h_ref[...] = _norm_mod(x_ref[0], ng_ref[...], row_mod(k_shift), row_mod(k_scale)).astype(BF16)

    h = h_ref[...]
    g = _dot(h, wg_ref[...])
    u = _dot(h, wu_ref[...])
    act = (g * (1.0 / (1.0 + jnp.exp(-g))) * u).astype(BF16)
    acc_ref[...] += _dot(act, w2_ref[...])

    @pl.when(f == pl.num_programs(2) - 1)
    def _():
        y_ref[0] = x_ref[0] + row_mod(k_gate) * acc_ref[...]


def ffn_residual(norm_gain, w13, w2, xa, mods, k_shift, k_scale, k_gate, n_ctx):
    b, ta, d = xa.shape
    ff = w2.shape[0]
    bm, fk = FFN_ROW_TILE, FF_CHUNK
    nf = ff // fk
    kern = functools.partial(_ffn_kernel, n_ctx=n_ctx, k_shift=k_shift, k_scale=k_scale, k_gate=k_gate)
    in_specs = [
        pl.BlockSpec((1, d), lambda b_, i, f: (0, 0)),
        pl.BlockSpec((d, fk), lambda b_, i, f: (0, f)),
        pl.BlockSpec((d, fk), lambda b_, i, f: (0, nf + f)),
        pl.BlockSpec((fk, d), lambda b_, i, f: (f, 0)),
        pl.BlockSpec((1, bm, d), lambda b_, i, f: (b_, i, 0)),
        pl.BlockSpec((None, None, 6, 1, d), lambda b_, i, f: (b_, 0, 0, 0, 0)),
        pl.BlockSpec((None, None, 6, 1, d), lambda b_, i, f: (b_, 1, 0, 0, 0)),
    ]
    return pl.pallas_call(
        kern,
        grid=(b, ta // bm, nf),
        in_specs=in_specs,
        out_specs=pl.BlockSpec((1, bm, d), lambda b_, i, f: (b_, i, 0)),
        out_shape=jax.ShapeDtypeStruct(xa.shape, F32),
        scratch_shapes=[pltpu.VMEM((bm, d), F32), pltpu.VMEM((bm, d), BF16)],
        input_output_aliases={4: 0},
        compiler_params=_cparams("parallel", "parallel", "arbitrary"),
        name="ffn",
    )(norm_gain.reshape(1, d), w13, w13, w2, xa, mods, mods)


MOE_SLOT_TILE = 512
MOE_GATHER_CHUNK = 512
MOE_COMBINE_CHUNK = 768
_VALID, _FIRST, _LAST = 1, 2, 4


def _one_hot_bf16(cond):
    return jnp.where(cond, 1.0, 0.0).astype(BF16)


def _moe_gather_kernel(tile_ref, chunk_ref, flag_ref, p1_ref, p2_ref, h_ref, o_ref, acc_ref):
    k = pl.program_id(0)
    fl = flag_ref[k]

    @pl.when((fl & _FIRST) != 0)
    def _():
        acc_ref[...] = jnp.zeros_like(acc_ref)

    @pl.when((fl & _VALID) != 0)
    def _():
        bm, ch = acc_ref.shape[0], h_ref.shape[0]
        slot = tile_ref[k] * bm + lax.broadcasted_iota(jnp.int32, (bm, ch), 0)
        acc_ref[...] += _dot(_one_hot_bf16((p1_ref[...] == slot) | (p2_ref[...] == slot)), h_ref[...])

    @pl.when((fl & _LAST) != 0)
    def _():
        o_ref[...] = acc_ref[...].astype(BF16)


def _moe_ffn_kernel(te_ref, nu_ref, xs_ref, wg_ref, wu_ref, w2_ref, y_ref, acc_ref):
    i = pl.program_id(0)
    f = pl.program_id(1)
    d = acc_ref.shape[1]

    @pl.when(i < nu_ref[0])
    def _():
        @pl.when(f == 0)
        def _():
            acc_ref[...] = jnp.zeros_like(acc_ref)

        xs = xs_ref[:, :d]
        g = _dot(xs, wg_ref[...])
        u = _dot(xs, wu_ref[...])
        act = (g * (1.0 / (1.0 + jnp.exp(-g))) * u).astype(BF16)
        acc_ref[...] += _dot(act, w2_ref[...])

        @pl.when(f == pl.num_programs(1) - 1)
        def _():
            rt = xs_ref[:, d:].astype(F32)
            first = rt[:, ROUTE_E1:ROUTE_E1 + 1] == te_ref[i].astype(F32)
            gate = jnp.where(first, rt[:, ROUTE_G1:ROUTE_G1 + 1] + rt[:, ROUTE_G1 + 1:ROUTE_G1 + 2],
                             rt[:, ROUTE_G2:ROUTE_G2 + 1] + rt[:, ROUTE_G2 + 1:ROUTE_G2 + 2])
            y_ref[...] = (acc_ref[...] * gate).astype(BF16)


def _moe_combine_kernel(chunk_ref, tile_ref, flag_ref, p1_ref, p2_ref, y_ref, x_ref, gc_ref, gl_ref,
                        o_ref, acc_ref, *, n_ctx, chunks_per_batch):
    k = pl.program_id(0)
    fl = flag_ref[k]
    ch, bm = acc_ref.shape[0], y_ref.shape[0]

    @pl.when((fl & _FIRST) != 0)
    def _():
        acc_ref[...] = jnp.zeros_like(acc_ref)

    @pl.when((fl & _VALID) != 0)
    def _():
        slot = tile_ref[k] * bm + lax.broadcasted_iota(jnp.int32, (ch, bm), 1)
        w = _one_hot_bf16((p1_ref[...] == slot) | (p2_ref[...] == slot))
        acc_ref[...] += _dot(w, y_ref[...])

    @pl.when((fl & _LAST) != 0)
    def _():
        row = (chunk_ref[k] % chunks_per_batch) * ch + lax.broadcasted_iota(jnp.int32, (ch, 1), 0)
        gate = jnp.where(row < n_ctx, gc_ref[...], gl_ref[...])
        o_ref[...] = x_ref[...] + gate * acc_ref[...]


def _expand_items(counts, n_items):
    off = jnp.cumsum(counts)
    total = off[-1]
    k = jnp.minimum(jnp.arange(n_items, dtype=jnp.int32), total - 1)
    grp = jnp.minimum(jnp.searchsorted(off, k, side="right").astype(jnp.int32), counts.shape[0] - 1)
    local = k - (off[grp] - counts[grp])
    valid = jnp.arange(n_items, dtype=jnp.int32) < total
    return grp, local, valid


def moe_residual(h, route, w13, w2, xa, mods, k_gate, n_ctx, ctx_active):
    b, ta, d = xa.shape
    t = b * ta
    ne, ff = w2.shape[0], w2.shape[1]
    bm, gch, cch, fk = MOE_SLOT_TILE, MOE_GATHER_CHUNK, MOE_COMBINE_CHUNK, FF_CHUNK
    nf = ff // fk
    nt = (2 * t) // bm + ne
    ns = nt * bm
    i32 = jnp.int32

    dx = h.shape[2]
    r = route.reshape(t, LANES)
    e1, e2 = r[:, 0].astype(i32), r[:, 1].astype(i32)
    tok = jnp.arange(t, dtype=i32)
    active = jnp.ones((t,), bool) if ctx_active else (tok % ta) >= n_ctx

    eid = jnp.arange(ne, dtype=i32)[:, None]
    member = ((e1[None] == eid) | (e2[None] == eid)) & active[None]
    csum = jnp.cumsum(member.astype(i32), axis=1)
    cap = (csum[:, -1] + bm - 1) // bm * bm
    end = jnp.cumsum(cap)
    start = end - cap
    n_used = (end[-1] // bm).astype(i32)

    def slot_of(e_sel):
        rank = jnp.take_along_axis(csum, e_sel[None], axis=0)[0] - 1
        return jnp.where(active, start[e_sel] + rank, -1)

    pos1, pos2 = slot_of(e1), slot_of(e2)
    tile_ids = jnp.arange(nt, dtype=i32)
    tile_expert = jnp.minimum(jnp.sum(end[None] <= (tile_ids * bm)[:, None], axis=1), ne - 1).astype(i32)

    r0 = tile_ids * bm - start[tile_expert]
    r1 = jnp.minimum(r0 + bm, csum[tile_expert, -1]) - 1
    rows = csum[tile_expert]
    find = jax.vmap(lambda row, v: jnp.searchsorted(row, v, side="left"))
    c_lo = jnp.minimum(find(rows, r0 + 1), t - 1).astype(i32) // gch
    c_hi = jnp.minimum(find(rows, r1 + 1), t - 1).astype(i32) // gch
    n_ch = jnp.where(tile_ids < n_used, c_hi - c_lo + 1, 0)
    ni_g = ne * (t // gch) + nt
    g_tile, g_local, g_valid = _expand_items(n_ch, ni_g)
    g_chunk = c_lo[g_tile] + g_local
    g_flag = g_valid * (_VALID + _FIRST * (g_local == 0) + _LAST * (g_local == n_ch[g_tile] - 1))

    xs = pl.pallas_call(
        _moe_gather_kernel,
        grid_spec=pltpu.PrefetchScalarGridSpec(
            num_scalar_prefetch=3,
            grid=(ni_g,),
            in_specs=[
                pl.BlockSpec((None, 1, gch), lambda k, tl, cn, fl: (cn[k], 0, 0)),
                pl.BlockSpec((None, 1, gch), lambda k, tl, cn, fl: (cn[k], 0, 0)),
                pl.BlockSpec((gch, dx), lambda k, tl, cn, fl: (cn[k], 0)),
            ],
            out_specs=pl.BlockSpec((bm, dx), lambda k, tl, cn, fl: (tl[k], 0)),
            scratch_shapes=[pltpu.VMEM((bm, dx), F32)],
        ),
        out_shape=jax.ShapeDtypeStruct((ns, dx), BF16),
        compiler_params=_cparams("arbitrary"),
        name="moe_gather",
    )(g_tile, g_chunk, g_flag.astype(i32), pos1.reshape(t // gch, 1, gch), pos2.reshape(t // gch, 1, gch),
      h.reshape(t, dx))

    def last_used(i, nu):
        return jnp.minimum(i, nu[0] - 1)

    def f_eff(i, f, nu):
        return jnp.where(i < nu[0], f, nf - 1)

    y = pl.pallas_call(
        _moe_ffn_kernel,
        grid_spec=pltpu.PrefetchScalarGridSpec(
            num_scalar_prefetch=2,
            grid=(nt, nf),
            in_specs=[
                pl.BlockSpec((bm, dx), lambda i, f, te, nu: (last_used(i, nu), 0)),
                pl.BlockSpec((None, d, fk), lambda i, f, te, nu: (te[i], 0, f_eff(i, f, nu))),
                pl.BlockSpec((None, d, fk), lambda i, f, te, nu: (te[i], 0, nf + f_eff(i, f, nu))),
                pl.BlockSpec((None, fk, d), lambda i, f, te, nu: (te[i], f_eff(i, f, nu), 0)),
            ],
            out_specs=pl.BlockSpec((bm, d), lambda i, f, te, nu: (last_used(i, nu), 0)),
            scratch_shapes=[pltpu.VMEM((bm, d), F32)],
        ),
        out_shape=jax.ShapeDtypeStruct((ns, d), BF16),
        compiler_params=_cparams("arbitrary", "arbitrary"),
        name="moe_ffn",
    )(tile_expert, n_used.reshape(1), xs, w13, w13, w2)

    nc = t // cch
    cs0 = jnp.concatenate([jnp.zeros((ne, 1), i32), csum], axis=1)[:, ::cch]
    before, through = cs0[:, :-1].T, cs0[:, 1:].T
    t_lo = (start[None] + before) // bm
    t_hi = (start[None] + through - 1) // bm
    n_tl = jnp.where(through > before, t_hi - t_lo + 1, 0).reshape(-1)
    ni_c = ne * nc + nt
    c_pair, c_local, c_valid = _expand_items(n_tl, ni_c)
    c_chunk = c_pair // ne
    c_tile = t_lo.reshape(-1)[c_pair] + c_local
    off = jnp.cumsum(n_tl)
    k_eff = jnp.minimum(jnp.arange(ni_c, dtype=i32), off[-1] - 1)
    chunk_first = (off - n_tl)[c_chunk * ne]
    chunk_last = off[c_chunk * ne + ne - 1] - 1
    c_flag = c_valid * (_VALID + _FIRST * (k_eff == chunk_first) + _LAST * (k_eff == chunk_last))

    cpb = ta // cch
    kern = functools.partial(_moe_combine_kernel, n_ctx=n_ctx, chunks_per_batch=cpb)
    out = pl.pallas_call(
        kern,
        grid_spec=pltpu.PrefetchScalarGridSpec(
            num_scalar_prefetch=3,
            grid=(ni_c,),
            in_specs=[
                pl.BlockSpec((cch, 1), lambda k, cn, tl, fl: (cn[k], 0)),
                pl.BlockSpec((cch, 1), lambda k, cn, tl, fl: (cn[k], 0)),
                pl.BlockSpec((bm, d), lambda k, cn, tl, fl: (tl[k], 0)),
                pl.BlockSpec((cch, d), lambda k, cn, tl, fl: (cn[k], 0)),
                pl.BlockSpec((None, None, None, 1, d), lambda k, cn, tl, fl: (cn[k] // cpb, 0, k_gate, 0, 0)),
                pl.BlockSpec((None, None, None, 1, d), lambda k, cn, tl, fl: (cn[k] // cpb, 1, k_gate, 0, 0)),
            ],
            out_specs=pl.BlockSpec((cch, d), lambda k, cn, tl, fl: (cn[k], 0)),
            scratch_shapes=[pltpu.VMEM((cch, d), F32)],
        ),
        out_shape=jax.ShapeDtypeStruct((t, d), F32),
        input_output_aliases={6: 0},
        compiler_params=_cparams("arbitrary"),
        name="moe_combine",
    )(c_chunk, c_tile, c_flag.astype(i32), pos1.reshape(t, 1), pos2.reshape(t, 1), y, xa.reshape(t, d), mods, mods)
    return out.reshape(b, ta, d)


def _rope_tables(seq, n_ctx, rot_dim):
    rows = seq // GRID_W
    quarter = rot_dim // 4
    inv_freq = ROPE_THETA ** (-jnp.arange(quarter, dtype=F32) / quarter)
    row = jnp.repeat(jnp.arange(rows, dtype=F32), GRID_W)
    col = jnp.tile(jnp.arange(GRID_W, dtype=F32), rows)
    ang = jnp.concatenate([row[:, None] * inv_freq, col[:, None] * inv_freq], axis=-1)
    cos, sin = jnp.cos(ang), jnp.sin(ang)
    reps = LANES // rot_dim
    c = jnp.tile(jnp.concatenate([cos, cos], axis=-1), (1, reps))
    s = jnp.tile(jnp.concatenate([-sin, sin], axis=-1), (1, reps))
    c = jnp.concatenate([jnp.ones((n_ctx, LANES), F32), c], axis=0)
    s = jnp.concatenate([jnp.zeros((n_ctx, LANES), F32), s], axis=0)
    return c, s


def kernel(x, c, ctx, c_ctx, ada_w, ada_b, norm_mix, norm_ffn, gqa_wqkv, gqa_q_gain, gqa_k_gain, gqa_wo, mla_wdown, mla_qa_gain, mla_kva_gain, mla_wuq, mla_wukv, mla_q_gain, mla_k_gain, mla_wo, win_wqkv, win_q_gain, win_k_gain, win_sink, win_wo, diff_wqkv, diff_q_gain, diff_k_gain, diff_lambda, diff_subln, diff_wo, ffn_w13, ffn_w2, moe_router, moe_w13, moe_w2):
    b, seq, d = x.shape
    n_ctx = ctx.shape[1]
    depth = ada_w.shape[0]
    ones = lambda n: jnp.ones((n,), F32)
    zeros = lambda n: jnp.zeros((n,), F32)

    crows = -(-(b + 1) // 8) * 8
    cond = jnp.zeros((crows, d), F32).at[:b].set(c).at[b].set(c_ctx)
    mods_all = adaln(cond, ada_w, ada_b)

    xa = jnp.concatenate([ctx, x], axis=1)

    for i in range(depth):
        need_ctx = i < depth - 1
        j = i // 4
        kind = i % 4
        ml = mods_all[i, :b].reshape(b, 6, 1, d)
        mc = jnp.broadcast_to(mods_all[i, b].reshape(1, 6, 1, d), (b, 6, 1, d))
        mods = jnp.stack([mc, ml], axis=1)

        prenorm = (norm_mix[i], mods, 0, 1, n_ctx)
        if kind == 0:
            hd = GQA_HEAD_DIM
            scale = hd ** -0.5 * LOG2E
            nq, nk = GQA_HEADS * hd, GQA_KV_HEADS * hd
            groups = [(g * hd, hd) for g in range(GQA_HEADS + GQA_KV_HEADS)]
            gain = jnp.concatenate([jnp.tile(gqa_q_gain[j] * scale, GQA_HEADS),
                                    jnp.tile(gqa_k_gain[j], GQA_KV_HEADS), ones(nk)])
            qkv = proj(xa, 0, gqa_wqkv[j].astype(BF16), groups, gain, _rope_tables(seq, n_ctx, hd),
                       range((nq + nk) // LANES), hd // 2, "gqa_qkv", prenorm)
            o = gqa_attention(qkv, n_ctx)
            w_o = gqa_wo[j]
        elif kind == 1:
            scale = (MLA_NOPE + MLA_ROPE) ** -0.5 * LOG2E
            hh = MLA_HEADS
            wd = mla_wdown[j]
            pad = jnp.zeros((d, LANES - MLA_ROPE), F32)
            wd2 = jnp.concatenate([wd[:, :MLA_Q_LORA], wd[:, MLA_Q_LORA + MLA_KV_LORA:], pad,
                                   wd[:, MLA_Q_LORA:MLA_Q_LORA + MLA_KV_LORA]], axis=1)
            kpe0 = MLA_Q_LORA
            ckv0 = MLA_Q_LORA + LANES
            groups = [(0, MLA_Q_LORA), (kpe0, MLA_ROPE), (ckv0, MLA_KV_LORA)]
            gain = jnp.concatenate([mla_qa_gain[j], mla_k_gain[j][MLA_NOPE:], zeros(LANES - MLA_ROPE),
                                    mla_kva_gain[j]])
            rope = _rope_tables(seq, n_ctx, MLA_ROPE)
            dn = proj(xa, 0, wd2.astype(BF16), groups, gain, rope, [kpe0 // LANES], MLA_ROPE // 2, "mla_down",
                      prenorm)
            wq = mla_wuq[j].reshape(MLA_Q_LORA, hh, MLA_NOPE + MLA_ROPE)
            wq2 = jnp.pad(wq, ((0, 0), (0, 0), (0, LANES - MLA_ROPE))).reshape(MLA_Q_LORA, hh * 2 * LANES)
            groups = ([(g * 2 * LANES, MLA_NOPE) for g in range(hh)]
                      + [(g * 2 * LANES + MLA_NOPE, MLA_ROPE) for g in range(hh)])
            qg = jnp.concatenate([mla_q_gain[j] * scale, zeros(LANES - MLA_ROPE)])
            gain = jnp.tile(qg, hh)
            q = proj(dn, 0, wq2.astype(BF16), groups, gain, rope, range(1, 2 * hh, 2), MLA_ROPE // 2, "mla_q")
            groups = [(g * (MLA_NOPE + MLA_V), MLA_NOPE) for g in range(hh)]
            gain = jnp.tile(jnp.concatenate([mla_k_gain[j][:MLA_NOPE], ones(MLA_V)]), hh)
            kv = proj(dn, ckv0 // MLA_KV_LORA, mla_wukv[j].astype(BF16), groups, gain, rope, [], MLA_ROPE // 2, "mla_kv")
            o = mla_attention(q, kv, dn, kpe0 // LANES, n_ctx)
            w_o = mla_wo[j]
        elif kind == 2:
            hd = WIN_HEAD_DIM
            scale = hd ** -0.5 * LOG2E
            n_pairs = WIN_HEADS // WIN_KV_HEADS
            nq = WIN_HEADS * hd
            perm = np.array([(kv * n_pairs + g) * hd + t for g in range(n_pairs)
                             for kv in range(WIN_KV_HEADS) for t in range(hd)])
            wq = win_wqkv[j]
            w2_ = jnp.concatenate([wq[:, perm], wq[:, nq:]], axis=1)
            groups = [(g * hd, hd) for g in range(WIN_HEADS + WIN_KV_HEADS)]
            gain = jnp.concatenate([jnp.tile(win_q_gain[j] * scale, WIN_HEADS),
                                    jnp.tile(win_k_gain[j], WIN_KV_HEADS), ones(WIN_KV_HEADS * hd)])
            qkv = proj(xa, 0, w2_.astype(BF16), groups, gain, _rope_tables(seq, n_ctx, hd),
                       range(n_pairs + 1), hd // 2, "win_qkv", prenorm)
            o = win_attention(qkv, win_sink[j] * LOG2E, n_ctx)
            w_o = win_wo[j][perm, :]
        else:
            hd = DIFF_HEAD_DIM
            scale = hd ** -0.5 * LOG2E
            lam_init = 0.8 - 0.6 * math.exp(-0.3 * i)
            nqk = 2 * DIFF_HEADS
            groups = [(g * hd, hd) for g in range(2 * nqk)]
            gain = jnp.concatenate([jnp.tile(diff_q_gain[j] * scale, nqk), jnp.tile(diff_k_gain[j], nqk),
                                    ones(DIFF_HEADS * 2 * hd)])
            qkv = proj(xa, 0, diff_wqkv[j].astype(BF16), groups, gain, _rope_tables(seq, n_ctx, hd),
                       range(2 * DIFF_HEADS), hd // 2, "diff_qkv", prenorm)
            o = diff_attention(qkv, diff_lambda[j], diff_subln[j], lam_init, n_ctx)
            w_o = diff_wo[j]
        xa = out_proj_residual(o, w_o.astype(BF16), xa, mods, 2, n_ctx)

        jj = i // 2
        if i % 2 == 0:
            xa = ffn_residual(norm_ffn[i], ffn_w13[jj].astype(BF16), ffn_w2[jj].astype(BF16), xa, mods, 3, 4, 5,
                              n_ctx)
        else:
            h, route = norm_mod_router(xa, norm_ffn[i], mods, 3, 4, n_ctx, moe_router[jj])
            xa = moe_residual(h, route, moe_w13[jj].astype(BF16), moe_w2[jj].astype(BF16), xa, mods, 5,
                              n_ctx, need_ctx)
    return xa[:, n_ctx:]
```

```python
import functools
import math

import jax
import jax.numpy as jnp
import numpy as np
from jax import lax
from jax.experimental import pallas as pl
from jax.experimental.pallas import tpu as pltpu

GRID_W = 64
ROPE_THETA = 10000.0
EPS = 1e-6
NEG_INF = -1e30
LANES = 128
MXU_WIDTH = 256
ROW_TILE = 256
FFN_ROW_TILE = 768
FF_CHUNK = 896
VMEM_LIMIT = 56 * 1024 * 1024

GQA_HEADS, GQA_KV_HEADS, GQA_HEAD_DIM = 8, 2, 128
MLA_HEADS, MLA_Q_LORA, MLA_KV_LORA, MLA_NOPE, MLA_ROPE, MLA_V = 8, 384, 256, 128, 64, 128
WIN_HEADS, WIN_KV_HEADS, WIN_HEAD_DIM, WINDOW = 16, 2, 64, 128
DIFF_HEADS, DIFF_HEAD_DIM = 8, 64
N_EXPERTS = 8

F32 = jnp.float32
BF16 = jnp.bfloat16


def _cparams(*sem):
    return pltpu.CompilerParams(dimension_semantics=sem, vmem_limit_bytes=VMEM_LIMIT)


def _dot(a, b):
    return jnp.dot(a, b, preferred_element_type=F32)


def _dot_nt(a, b):
    return lax.dot_general(a, b, (((1,), (1,)), ((), ())), preferred_element_type=F32)


def _split_bf16(v):
    hi = v.astype(BF16)
    lo = (v - hi.astype(F32)).astype(BF16)
    return hi, lo


def _adaln_kernel(c_ref, w_ref, b_ref, o_ref):
    c = c_ref[...]
    sc = c * (1.0 / (1.0 + jnp.exp(-c)))
    o_ref[0] = jnp.dot(sc, w_ref[0], preferred_element_type=F32,
                       precision=lax.Precision.HIGHEST) + b_ref[0]


def adaln(cond, ada_w, ada_b):
    depth, d, n = ada_w.shape
    rows = cond.shape[0]
    bn = 1536
    return pl.pallas_call(
        _adaln_kernel,
        grid=(depth, n // bn),
        in_specs=[
            pl.BlockSpec((rows, d), lambda l, j: (0, 0)),
            pl.BlockSpec((1, d, bn), lambda l, j: (l, 0, j)),
            pl.BlockSpec((1, 1, bn), lambda l, j: (l, 0, j)),
        ],
        out_specs=pl.BlockSpec((1, rows, bn), lambda l, j: (l, 0, j)),
        out_shape=jax.ShapeDtypeStruct((depth, rows, n), F32),
        compiler_params=_cparams("parallel", "parallel"),
        name="adaln",
    )(cond, ada_w, ada_b.reshape(depth, 1, n))


def _mod_spec(k, n_ctx_tiles, d):
    return pl.BlockSpec((None, None, None, 1, d),
                        lambda b, i, *_: (b, jnp.minimum(i // n_ctx_tiles, 1), k, 0, 0))


def _norm_mod(x, g, shift, scale):
    ms = jnp.mean(x * x, axis=-1, keepdims=True)
    return x * lax.rsqrt(ms + EPS) * g * (1.0 + scale) + shift


def _norm_mod_router_kernel(x_ref, g_ref, sh_ref, sc_ref, r_ref, h_ref, cw_ref):
    d = x_ref.shape[2]
    h = _norm_mod(x_ref[0], g_ref[...], sh_ref[...], sc_ref[...])
    h_ref[0, :, :d] = h.astype(BF16)
    logits = jnp.dot(h, r_ref[...], preferred_element_type=F32, precision=lax.Precision.HIGHEST)
    lane = lax.broadcasted_iota(jnp.int32, logits.shape, 1)
    logits = jnp.where(lane < N_EXPERTS, logits, -jnp.inf)
    m1 = jnp.max(logits, axis=-1, keepdims=True)
    i1 = jnp.min(jnp.where(logits == m1, lane, LANES), axis=-1, keepdims=True)
    rest = jnp.where(lane == i1, -jnp.inf, logits)
    m2 = jnp.max(rest, axis=-1, keepdims=True)
    i2 = jnp.min(jnp.where(rest == m2, lane, LANES), axis=-1, keepdims=True)
    e2 = jnp.exp(m2 - m1)
    den = 1.0 + e2
    g1, g2 = 1.0 / den, e2 / den
    cw_ref[0] = jnp.where(lane == 0, i1.astype(F32), jnp.where(lane == 1, i2.astype(F32),
                          jnp.where(lane == 2, g1, jnp.where(lane == 3, g2, 0.0))))
    g1_hi, g2_hi = g1.astype(BF16).astype(F32), g2.astype(BF16).astype(F32)
    ext = jnp.where(lane == ROUTE_G1, g1_hi, jnp.where(lane == ROUTE_G1 + 1, g1 - g1_hi,
          jnp.where(lane == ROUTE_G2, g2_hi, jnp.where(lane == ROUTE_G2 + 1, g2 - g2_hi,
          jnp.where(lane == ROUTE_E1, i1.astype(F32), jnp.where(lane == ROUTE_E2, i2.astype(F32), 0.0))))))
    h_ref[0, :, d:] = ext.astype(BF16)


ROUTE_G1, ROUTE_G2, ROUTE_E1, ROUTE_E2 = 0, 2, 4, 5


def norm_mod_router(xa, gain, mods, k_shift, k_scale, n_ctx, router):
    b, ta, d = xa.shape
    bm = ROW_TILE
    nct = n_ctx // bm
    rpad = jnp.zeros((d, LANES), F32).at[:, :N_EXPERTS].set(router)
    return pl.pallas_call(
        _norm_mod_router_kernel, grid=(b, ta // bm),
        in_specs=[
            pl.BlockSpec((1, bm, d), lambda b_, i: (b_, i, 0)),
            pl.BlockSpec((1, d), lambda b_, i: (0, 0)),
            _mod_spec(k_shift, nct, d),
            _mod_spec(k_scale, nct, d),
            pl.BlockSpec((d, LANES), lambda b_, i: (0, 0)),
        ],
        out_specs=[pl.BlockSpec((1, bm, d + LANES), lambda b_, i: (b_, i, 0)),
                   pl.BlockSpec((1, bm, LANES), lambda b_, i: (b_, i, 0))],
        out_shape=[jax.ShapeDtypeStruct((b, ta, d + LANES), BF16), jax.ShapeDtypeStruct((b, ta, LANES), F32)],
        compiler_params=_cparams("parallel", "parallel"), name="norm_mod_router",
    )(xa, gain.reshape(1, d), mods, mods, rpad)


def _proj_kernel(*refs, rope_blocks, rope_half, prenorm, norm_blocks):
    if prenorm:
        x_ref, g_ref, sh_ref, sc_ref = refs[:4]
        h = _norm_mod(x_ref[0], g_ref[...], sh_ref[...], sc_ref[...]).astype(BF16)
        refs = refs[4:]
    else:
        h = refs[0][0]
        refs = refs[1:]
    y = _dot(h, refs[0][...])
    n = y.shape[1]
    s_hi, s_lo = _split_bf16(y * y)
    if norm_blocks is None:
        _, e_ref, et_ref, igs_ref, u_ref, gain_ref, c_ref, s_ref, o_ref = refs
        ss = _dot(s_hi, e_ref[...]) + _dot(s_lo, e_ref[...])
        inv = lax.rsqrt(ss * igs_ref[...] + EPS)
        i_hi, i_lo = _split_bf16(inv)
        fac = _dot(i_hi, et_ref[...]) + _dot(i_lo, et_ref[...]) + u_ref[...]
        z = y * (fac * gain_ref[...])
    else:
        _, j_ref, igs_ref, u_ref, gain_ref, c_ref, s_ref, o_ref = refs
        zs = []
        for blk in range(n // MXU_WIDTH):
            sl = slice(blk * MXU_WIDTH, (blk + 1) * MXU_WIDTH)
            if blk in norm_blocks:
                ss = _dot(s_hi[:, sl], j_ref[blk]) + _dot(s_lo[:, sl], j_ref[blk])
                u = u_ref[:, sl]
                fac = lax.rsqrt(ss * igs_ref[:, sl] + EPS) * (1.0 - u) + u
                zs.append(y[:, sl] * (fac * gain_ref[:, sl]))
            else:
                zs.append(y[:, sl] * gain_ref[:, sl])
        z = jnp.concatenate(zs, axis=1)
    cos = c_ref[...]
    sin = s_ref[...]
    if rope_half == LANES // 2:
        first = None
    else:
        lane = lax.broadcasted_iota(jnp.int32, cos.shape, 1)
        first = (lane % (2 * rope_half)) < rope_half
    for blk in range(n // LANES):
        zb = z[:, blk * LANES:(blk + 1) * LANES]
        if blk in rope_blocks:
            if first is None:
                rot = pltpu.roll(zb, LANES // 2, 1)
            else:
                rot = jnp.where(first, pltpu.roll(zb, LANES - rope_half, 1), pltpu.roll(zb, rope_half, 1))
            zb = zb * cos + rot * sin
        o_ref[0, :, blk * LANES:(blk + 1) * LANES] = zb.astype(BF16)


def proj(h, kin_block, w, groups, gain, rope_tabs, rope_blocks, rope_half, name, prenorm=None):
    b, ta, _ = h.shape
    k, n = w.shape
    bm = ROW_TILE
    mw = MXU_WIDTH
    u = np.ones((1, n), np.float32)
    for start, size in groups:
        u[0, start:start + size] = 0.0
    local = n % mw == 0 and all(start // mw == (start + size - 1) // mw for start, size in groups)
    if local:
        jm = np.zeros((n // mw, mw, mw), np.float32)
        igs = np.zeros((1, n), np.float32)
        for start, size in groups:
            blk, o = divmod(start, mw)
            jm[blk, o:o + size, o:o + size] = 1.0
            igs[0, start:start + size] = 1.0 / size
        norm_blocks = frozenset(start // mw for start, _ in groups)
        norm_args = [jnp.asarray(jm, BF16), jnp.asarray(igs)]
        norm_specs = [pl.BlockSpec((n // mw, mw, mw), lambda b_, i: (0, 0, 0)),
                      pl.BlockSpec((1, n), lambda b_, i: (0, 0))]
    else:
        e = np.zeros((n, LANES), np.float32)
        igs = np.zeros((1, LANES), np.float32)
        for gi, (start, size) in enumerate(groups):
            e[start:start + size, gi] = 1.0
            igs[0, gi] = 1.0 / size
        norm_blocks = None
        norm_args = [jnp.asarray(e, BF16), jnp.asarray(e.T, BF16), jnp.asarray(igs)]
        norm_specs = [pl.BlockSpec((n, LANES), lambda b_, i: (0, 0)),
                      pl.BlockSpec((LANES, n), lambda b_, i: (0, 0)),
                      pl.BlockSpec((1, LANES), lambda b_, i: (0, 0))]
    cos, sin = rope_tabs
    kern = functools.partial(_proj_kernel, rope_blocks=frozenset(rope_blocks), rope_half=rope_half,
                             prenorm=prenorm is not None, norm_blocks=norm_blocks)
    lead_specs = [pl.BlockSpec((1, bm, k), lambda b_, i: (b_, i, kin_block))]
    lead_args = [h]
    if prenorm is not None:
        ngain, mods, k_shift, k_scale, n_ctx = prenorm
        nct = n_ctx // bm
        lead_specs += [pl.BlockSpec((1, k), lambda b_, i: (0, 0)), _mod_spec(k_shift, nct, k),
                       _mod_spec(k_scale, nct, k)]
        lead_args += [ngain.reshape(1, k), mods, mods]
    return pl.pallas_call(
        kern,
        grid=(b, ta // bm),
        in_specs=lead_specs + [pl.BlockSpec((k, n), lambda b_, i: (0, 0))] + norm_specs + [
            pl.BlockSpec((1, n), lambda b_, i: (0, 0)),
            pl.BlockSpec((1, n), lambda b_, i: (0, 0)),
            pl.BlockSpec((bm, LANES), lambda b_, i: (i, 0)),
            pl.BlockSpec((bm, LANES), lambda b_, i: (i, 0)),
        ],
        out_specs=pl.BlockSpec((1, bm, n), lambda b_, i: (b_, i, 0)),
        out_shape=jax.ShapeDtypeStruct((b, ta, n), BF16),
        compiler_params=_cparams("parallel", "parallel"),
        name=name,
    )(*lead_args, w, *norm_args, jnp.asarray(u), gain.reshape(1, n).astype(F32), cos, sin)


LOG2E = math.log2(math.e)


def _softmax_pv(s, v):
    m = jnp.max(s, axis=-1, keepdims=True)
    p = jnp.exp2(s - m)
    l = jnp.sum(p, axis=-1, keepdims=True)
    return _dot(p.astype(BF16), v) / l


def _gqa_attn_kernel(q_ref, k_ref, v_ref, o_ref, *, n_ctx, group, hd):
    i = pl.program_id(2)

    def attend(nk):
        k = k_ref[0, :nk, :]
        v = v_ref[0, :nk, :]
        for g in range(group):
            q = q_ref[0, :, g * hd:(g + 1) * hd]
            o = _softmax_pv(_dot_nt(q, k), v)
            o_ref[0, :, g * hd:(g + 1) * hd] = o.astype(BF16)

    nct = n_ctx // ROW_TILE

    @pl.when(i < nct)
    def _():
        attend(n_ctx)

    @pl.when(i >= nct)
    def _():
        attend(k_ref.shape[1])


def gqa_attention(qkv, n_ctx):
    b, ta, _ = qkv.shape
    hd, group, kvh = GQA_HEAD_DIM, GQA_HEADS // GQA_KV_HEADS, GQA_KV_HEADS
    bq = ROW_TILE
    qw = group * hd
    kb = GQA_HEADS
    vb = GQA_HEADS + kvh
    kern = functools.partial(_gqa_attn_kernel, n_ctx=n_ctx, group=group, hd=hd)
    return pl.pallas_call(
        kern,
        grid=(b, kvh, ta // bq),
        in_specs=[
            pl.BlockSpec((1, bq, qw), lambda b_, h, i: (b_, i, h)),
            pl.BlockSpec((1, ta, hd), lambda b_, h, i: (b_, 0, kb + h)),
            pl.BlockSpec((1, ta, hd), lambda b_, h, i: (b_, 0, vb + h)),
        ],
        out_specs=pl.BlockSpec((1, bq, qw), lambda b_, h, i: (b_, i, h)),
        out_shape=jax.ShapeDtypeStruct((b, ta, GQA_HEADS * hd), BF16),
        compiler_params=_cparams("parallel", "parallel", "arbitrary"),
        name="gqa_attn",
    )(qkv, qkv, qkv)


def _mla_attn_kernel(q_ref, kn_ref, kp_ref, v_ref, o_ref, kcat_ref, *, n_ctx):
    i = pl.program_id(2)

    @pl.when(i == 0)
    def _():
        kcat_ref[:, :LANES] = kn_ref[0]
        kcat_ref[:, LANES:] = kp_ref[0]

    def attend(nk):
        s = _dot_nt(q_ref[0], kcat_ref[:nk, :])
        o_ref[0] = _softmax_pv(s, v_ref[0, :nk, :]).astype(BF16)

    nct = n_ctx // ROW_TILE

    @pl.when(i < nct)
    def _():
        attend(n_ctx)

    @pl.when(i >= nct)
    def _():
        attend(kn_ref.shape[1])


def mla_attention(q, kv, dn, kpe_block, n_ctx):
    b, ta, _ = q.shape
    bq = ROW_TILE
    hh = MLA_HEADS
    kern = functools.partial(_mla_attn_kernel, n_ctx=n_ctx)
    return pl.pallas_call(
        kern,
        grid=(b, hh, ta // bq),
        in_specs=[
            pl.BlockSpec((1, bq, 2 * LANES), lambda b_, h, i: (b_, i, h)),
            pl.BlockSpec((1, ta, LANES), lambda b_, h, i: (b_, 0, 2 * h)),
            pl.BlockSpec((1, ta, LANES), lambda b_, h, i: (b_, 0, kpe_block)),
            pl.BlockSpec((1, ta, LANES), lambda b_, h, i: (b_, 0, 2 * h + 1)),
        ],
        out_specs=pl.BlockSpec((1, bq, LANES), lambda b_, h, i: (b_, i, h)),
        out_shape=jax.ShapeDtypeStruct((b, ta, hh * MLA_V), BF16),
        scratch_shapes=[pltpu.VMEM((ta, 2 * LANES), BF16)],
        compiler_params=_cparams("parallel", "parallel", "arbitrary"),
        name="mla_attn",
    )(q, kv, dn, kv)


def _half_masks(shape):
    lane = lax.broadcasted_iota(jnp.int32, shape, 1)
    return lane < (LANES // 2)


def _diff_attn_kernel(q_ref, k_ref, v_ref, lam_ref, sub_ref, o_ref, *, n_ctx, lam_init):
    i = pl.program_id(2)
    lp = lam_ref[...]
    lam = (jnp.exp(jnp.sum(lp[0:1] * lp[1:2], axis=-1, keepdims=True))
           - jnp.exp(jnp.sum(lp[2:3] * lp[3:4], axis=-1, keepdims=True)) + lam_init)

    def attend(nk):
        q = q_ref[0]
        lo = _half_masks(q.shape)
        zero = jnp.zeros_like(q)
        k = k_ref[0, :nk, :]
        v = v_ref[0, :nk, :]
        def probs(s, weight):
            m = jnp.max(s, axis=-1, keepdims=True)
            e = jnp.exp2(s - m)
            return e * (weight / jnp.sum(e, axis=-1, keepdims=True))

        p = probs(_dot_nt(jnp.where(lo, q, zero), k), 1.0) - probs(_dot_nt(jnp.where(lo, zero, q), k), lam)
        o = _dot(p.astype(BF16), v)
        ms = jnp.mean(o * o, axis=-1, keepdims=True)
        o_ref[0] = (o * lax.rsqrt(ms + EPS) * sub_ref[...] * (1.0 - lam_init)).astype(BF16)

    nct = n_ctx // ROW_TILE

    @pl.when(i < nct)
    def _():
        attend(n_ctx)

    @pl.when(i >= nct)
    def _():
        attend(k_ref.shape[1])


def diff_attention(qkv, lam_p, subln, lam_init, n_ctx):
    b, ta, _ = qkv.shape
    bq = ROW_TILE
    hh = DIFF_HEADS
    kern = functools.partial(_diff_attn_kernel, n_ctx=n_ctx, lam_init=lam_init)
    return pl.pallas_call(
        kern,
        grid=(b, hh, ta // bq),
        in_specs=[
            pl.BlockSpec((1, bq, LANES), lambda b_, h, i: (b_, i, h)),
            pl.BlockSpec((1, ta, LANES), lambda b_, h, i: (b_, 0, hh + h)),
            pl.BlockSpec((1, ta, LANES), lambda b_, h, i: (b_, 0, 2 * hh + h)),
            pl.BlockSpec((4, DIFF_HEAD_DIM), lambda b_, h, i: (0, 0)),
            pl.BlockSpec((1, LANES), lambda b_, h, i: (0, 0)),
        ],
        out_specs=pl.BlockSpec((1, bq, LANES), lambda b_, h, i: (b_, i, h)),
        out_shape=jax.ShapeDtypeStruct((b, ta, hh * 2 * DIFF_HEAD_DIM), BF16),
        compiler_params=_cparams("parallel", "parallel", "arbitrary"),
        name="diff_attn",
    )(qkv, qkv, qkv, lam_p.astype(F32), subln.reshape(1, LANES).astype(F32))


def _win_attn_kernel(sink_ref, q_ref, k_ref, v_ref, o_ref, *, n_ctx, seq, n_pairs):
    i = pl.program_id(1)
    bq = q_ref.shape[1]
    band = bq + 2 * WINDOW
    nct = n_ctx // bq

    def head_out(qm, kv_idx, pair, blocks):
        sk = sink_ref[kv_idx * n_pairs + pair]
        ss = []
        m = None
        for k, _, mask in blocks:
            s = _dot_nt(qm, k)
            if mask is not None:
                s = jnp.where(mask, s, NEG_INF)
            ss.append(s)
            bm_ = jnp.max(s, axis=-1, keepdims=True)
            m = bm_ if m is None else jnp.maximum(m, bm_)
        m = jnp.maximum(m, sk)
        l = jnp.exp2(sk - m)
        o = None
        for s, (_, v, _) in zip(ss, blocks):
            p = jnp.exp2(s - m)
            l = l + jnp.sum(p, axis=-1, keepdims=True)
            pv = _dot(p.astype(BF16), v)
            o = pv if o is None else o + pv
        return o / l

    def run(blocks):
        for pair in range(n_pairs):
            q = q_ref[0, :, pair * LANES:(pair + 1) * LANES]
            lo = _half_masks(q.shape)
            zero = jnp.zeros_like(q)
            o0 = head_out(jnp.where(lo, q, zero), 0, pair, blocks)
            o1 = head_out(jnp.where(lo, zero, q), 1, pair, blocks)
            o_ref[0, :, pair * LANES:(pair + 1) * LANES] = jnp.where(lo, o0, o1).astype(BF16)

    @pl.when(i < nct)
    def _():
        run([(k_ref[0, :n_ctx, :], v_ref[0, :n_ctx, :], None)])

    @pl.when(i >= nct)
    def _():
        q0 = (i - nct) * bq
        start = jnp.clip(q0 - WINDOW, 0, seq - band)
        start = pl.multiple_of(start, WINDOW)
        qpos = q0 + lax.broadcasted_iota(jnp.int32, (bq, band), 0)
        kpos = start + lax.broadcasted_iota(jnp.int32, (bq, band), 1)
        mask = jnp.abs(qpos - kpos) <= WINDOW
        kb = k_ref[0, pl.ds(n_ctx + start, band), :]
        vb = v_ref[0, pl.ds(n_ctx + start, band), :]
        run([(k_ref[0, :n_ctx, :], v_ref[0, :n_ctx, :], None), (kb, vb, mask)])


def win_attention(qkv, sink, n_ctx):
    b, ta, _ = qkv.shape
    bq = ROW_TILE
    n_pairs = WIN_HEADS // WIN_KV_HEADS
    qw = n_pairs * LANES
    kern = functools.partial(_win_attn_kernel, n_ctx=n_ctx, seq=ta - n_ctx, n_pairs=n_pairs)
    return pl.pallas_call(
        kern,
        grid_spec=pltpu.PrefetchScalarGridSpec(
            num_scalar_prefetch=1,
            grid=(b, ta // bq),
            in_specs=[
                pl.BlockSpec((1, bq, qw), lambda b_, i, s: (b_, i, 0)),
                pl.BlockSpec((1, ta, LANES), lambda b_, i, s: (b_, 0, n_pairs)),
                pl.BlockSpec((1, ta, LANES), lambda b_, i, s: (b_, 0, n_pairs + 1)),
            ],
            out_specs=pl.BlockSpec((1, bq, qw), lambda b_, i, s: (b_, i, 0)),
        ),
        out_shape=jax.ShapeDtypeStruct((b, ta, qw), BF16),
        compiler_params=_cparams("parallel", "arbitrary"),
        name="win_attn",
    )(sink.astype(F32), qkv, qkv, qkv)


def _out_proj_kernel(o_ref, w_ref, x_ref, g_ref, y_ref):
    y_ref[0] = x_ref[0] + g_ref[...] * _dot(o_ref[0], w_ref[...])


def out_proj_residual(o, w, xa, mods, k_gate, n_ctx):
    b, ta, d = xa.shape
    kdim = o.shape[2]
    bm = ROW_TILE
    return pl.pallas_call(
        _out_proj_kernel,
        grid=(b, ta // bm),
        in_specs=[
            pl.BlockSpec((1, bm, kdim), lambda b_, i: (b_, i, 0)),
            pl.BlockSpec((kdim, d), lambda b_, i: (0, 0)),
            pl.BlockSpec((1, bm, d), lambda b_, i: (b_, i, 0)),
            _mod_spec(k_gate, n_ctx // bm, d),
        ],
        out_specs=pl.BlockSpec((1, bm, d), lambda b_, i: (b_, i, 0)),
        out_shape=jax.ShapeDtypeStruct(xa.shape, F32),
        input_output_aliases={2: 0},
        compiler_params=_cparams("parallel", "parallel"),
        name="out_proj",
    )(o, w, xa, mods)


def _ffn_kernel(ng_ref, wg_ref, wu_ref, w2_ref, x_ref, mc_ref, ml_ref, y_ref, acc_ref, h_ref, *,
                n_ctx, k_shift, k_scale, k_gate):
    i = pl.program_id(1)
    f = pl.program_id(2)
    bm = acc_ref.shape[0]

    def row_mod(k):
        row = i * bm + lax.broadcasted_iota(jnp.int32, (bm, 1), 0)
        return jnp.where(row < n_ctx, mc_ref[k], ml_ref[k])

    @pl.when(f == 0)
    def _():
        acc_ref[...] = jnp.zeros_like(acc_ref)
        h_ref[...] = _norm_mod(x_ref[0], ng_ref[...], row_mod(k_shift), row_mod(k_scale)).astype(BF16)

    h = h_ref[...]
    g = _dot(h, wg_ref[...])
    u = _dot(h, wu_ref[...])
    act = (g * (1.0 / (1.0 + jnp.exp(-g))) * u).astype(BF16)
    acc_ref[...] += _dot(act, w2_ref[...])

    @pl.when(f == pl.num_programs(2) - 1)
    def _():
        y_ref[0] = x_ref[0] + row_mod(k_gate) * acc_ref[...]


def ffn_residual(norm_gain, w13, w2, xa, mods, k_shift, k_scale, k_gate, n_ctx):
    b, ta, d = xa.shape
    ff = w2.shape[0]
    bm, fk = FFN_ROW_TILE, FF_CHUNK
    nf = ff // fk
    kern = functools.partial(_ffn_kernel, n_ctx=n_ctx, k_shift=k_shift, k_scale=k_scale, k_gate=k_gate)
    in_specs = [
        pl.BlockSpec((1, d), lambda b_, i, f: (0, 0)),
        pl.BlockSpec((d, fk), lambda b_, i, f: (0, f)),
        pl.BlockSpec((d, fk), lambda b_, i, f: (0, nf + f)),
        pl.BlockSpec((fk, d), lambda b_, i, f: (f, 0)),
        pl.BlockSpec((1, bm, d), lambda b_, i, f: (b_, i, 0)),
        pl.BlockSpec((None, None, 6, 1, d), lambda b_, i, f: (b_, 0, 0, 0, 0)),
        pl.BlockSpec((None, None, 6, 1, d), lambda b_, i, f: (b_, 1, 0, 0, 0)),
    ]
    return pl.pallas_call(
        kern,
        grid=(b, ta // bm, nf),
        in_specs=in_specs,
        out_specs=pl.BlockSpec((1, bm, d), lambda b_, i, f: (b_, i, 0)),
        out_shape=jax.ShapeDtypeStruct(xa.shape, F32),
        scratch_shapes=[pltpu.VMEM((bm, d), F32), pltpu.VMEM((bm, d), BF16)],
        input_output_aliases={4: 0},
        compiler_params=_cparams("parallel", "parallel", "arbitrary"),
        name="ffn",
    )(norm_gain.reshape(1, d), w13, w13, w2, xa, mods, mods)


MOE_SLOT_TILE = 512
MOE_GATHER_CHUNK = 512
MOE_COMBINE_CHUNK = 768
_VALID, _FIRST, _LAST = 1, 2, 4


def _one_hot_bf16(cond):
    return jnp.where(cond, 1.0, 0.0).astype(BF16)


def _moe_gather_kernel(tile_ref, chunk_ref, flag_ref, p1_ref, p2_ref, h_ref, o_ref, acc_ref):
    k = pl.program_id(0)
    fl = flag_ref[k]

    @pl.when((fl & _FIRST) != 0)
    def _():
        acc_ref[...] = jnp.zeros_like(acc_ref)

    @pl.when((fl & _VALID) != 0)
    def _():
        bm, ch = acc_ref.shape[0], h_ref.shape[0]
        slot = tile_ref[k] * bm + lax.broadcasted_iota(jnp.int32, (bm, ch), 0)
        acc_ref[...] += _dot(_one_hot_bf16((p1_ref[...] == slot) | (p2_ref[...] == slot)), h_ref[...])

    @pl.when((fl & _LAST) != 0)
    def _():
        o_ref[...] = acc_ref[...].astype(BF16)


def _moe_ffn_kernel(te_ref, nu_ref, xs_ref, wg_ref, wu_ref, w2_ref, y_ref, acc_ref):
    i = pl.program_id(0)
    f = pl.program_id(1)
    d = acc_ref.shape[1]

    @pl.when(i < nu_ref[0])
    def _():
        @pl.when(f == 0)
        def _():
            acc_ref[...] = jnp.zeros_like(acc_ref)

        xs = xs_ref[:, :d]
        g = _dot(xs, wg_ref[...])
        u = _dot(xs, wu_ref[...])
        act = (g * (1.0 / (1.0 + jnp.exp(-g))) * u).astype(BF16)
        acc_ref[...] += _dot(act, w2_ref[...])

        @pl.when(f == pl.num_programs(1) - 1)
        def _():
            rt = xs_ref[:, d:].astype(F32)
            first = rt[:, ROUTE_E1:ROUTE_E1 + 1] == te_ref[i].astype(F32)
            gate = jnp.where(first, rt[:, ROUTE_G1:ROUTE_G1 + 1] + rt[:, ROUTE_G1 + 1:ROUTE_G1 + 2],
                             rt[:, ROUTE_G2:ROUTE_G2 + 1] + rt[:, ROUTE_G2 + 1:ROUTE_G2 + 2])
            y_ref[...] = (acc_ref[...] * gate).astype(BF16)


def _moe_combine_kernel(chunk_ref, tile_ref, flag_ref, p1_ref, p2_ref, y_ref, x_ref, gc_ref, gl_ref,
                        o_ref, acc_ref, *, n_ctx, chunks_per_batch):
    k = pl.program_id(0)
    fl = flag_ref[k]
    ch, bm = acc_ref.shape[0], y_ref.shape[0]

    @pl.when((fl & _FIRST) != 0)
    def _():
        acc_ref[...] = jnp.zeros_like(acc_ref)

    @pl.when((fl & _VALID) != 0)
    def _():
        slot = tile_ref[k] * bm + lax.broadcasted_iota(jnp.int32, (ch, bm), 1)
        w = _one_hot_bf16((p1_ref[...] == slot) | (p2_ref[...] == slot))
        acc_ref[...] += _dot(w, y_ref[...])

    @pl.when((fl & _LAST) != 0)
    def _():
        row = (chunk_ref[k] % chunks_per_batch) * ch + lax.broadcasted_iota(jnp.int32, (ch, 1), 0)
        gate = jnp.where(row < n_ctx, gc_ref[...], gl_ref[...])
        o_ref[...] = x_ref[...] + gate * acc_ref[...]


def _expand_items(counts, n_items):
    off = jnp.cumsum(counts)
    total = off[-1]
    k = jnp.minimum(jnp.arange(n_items, dtype=jnp.int32), total - 1)
    grp = jnp.minimum(jnp.sum(off[None, :] <= k[:, None], axis=1).astype(jnp.int32), counts.shape[0] - 1)
    local = k - (off[grp] - counts[grp])
    valid = jnp.arange(n_items, dtype=jnp.int32) < total
    return grp, local, valid


def moe_residual(h, route, w13, w2, xa, mods, k_gate, n_ctx, ctx_active):
    b, ta, d = xa.shape
    t = b * ta
    ne, ff = w2.shape[0], w2.shape[1]
    bm, gch, cch, fk = MOE_SLOT_TILE, MOE_GATHER_CHUNK, MOE_COMBINE_CHUNK, FF_CHUNK
    nf = ff // fk
    nt = (2 * t) // bm + ne
    ns = nt * bm
    i32 = jnp.int32

    dx = h.shape[2]
    r = route.reshape(t, LANES)
    e1, e2 = r[:, 0].astype(i32), r[:, 1].astype(i32)
    tok = jnp.arange(t, dtype=i32)
    active = jnp.ones((t,), bool) if ctx_active else (tok % ta) >= n_ctx

    eid = jnp.arange(ne, dtype=i32)[:, None]
    member = ((e1[None] == eid) | (e2[None] == eid)) & active[None]
    csum = jnp.cumsum(member.astype(i32), axis=1)
    cap = (csum[:, -1] + bm - 1) // bm * bm
    end = jnp.cumsum(cap)
    start = end - cap
    n_used = (end[-1] // bm).astype(i32)

    def slot_of(e_sel):
        rank = jnp.take_along_axis(csum, e_sel[None], axis=0)[0] - 1
        return jnp.where(active, start[e_sel] + rank, -1)

    pos1, pos2 = slot_of(e1), slot_of(e2)
    tile_ids = jnp.arange(nt, dtype=i32)
    tile_expert = jnp.minimum(jnp.sum(end[None] <= (tile_ids * bm)[:, None], axis=1), ne - 1).astype(i32)

    r0 = tile_ids * bm - start[tile_expert]
    r1 = jnp.minimum(r0 + bm, csum[tile_expert, -1]) - 1
    ngc = t // gch
    through_g = csum[:, gch - 1::gch][tile_expert]
    c_lo = jnp.minimum(jnp.sum(through_g <= r0[:, None], axis=1), ngc - 1).astype(i32)
    c_hi = jnp.minimum(jnp.sum(through_g <= r1[:, None], axis=1), ngc - 1).astype(i32)
    n_ch = jnp.where(tile_ids < n_used, c_hi - c_lo + 1, 0)
    ni_g = ne * (t // gch) + nt
    g_tile, g_local, g_valid = _expand_items(n_ch, ni_g)
    g_chunk = c_lo[g_tile] + g_local
    g_flag = g_valid * (_VALID + _FIRST * (g_local == 0) + _LAST * (g_local == n_ch[g_tile] - 1))

    xs = pl.pallas_call(
        _moe_gather_kernel,
        grid_spec=pltpu.PrefetchScalarGridSpec(
            num_scalar_prefetch=3,
            grid=(ni_g,),
            in_specs=[
                pl.BlockSpec((None, 1, gch), lambda k, tl, cn, fl: (cn[k], 0, 0)),
                pl.BlockSpec((None, 1, gch), lambda k, tl, cn, fl: (cn[k], 0, 0)),
                pl.BlockSpec((gch, dx), lambda k, tl, cn, fl: (cn[k], 0)),
            ],
            out_specs=pl.BlockSpec((bm, dx), lambda k, tl, cn, fl: (tl[k], 0)),
            scratch_shapes=[pltpu.VMEM((bm, dx), F32)],
        ),
        out_shape=jax.ShapeDtypeStruct((ns, dx), BF16),
        compiler_params=_cparams("arbitrary"),
        name="moe_gather",
    )(g_tile, g_chunk, g_flag.astype(i32), pos1.reshape(t // gch, 1, gch), pos2.reshape(t // gch, 1, gch),
      h.reshape(t, dx))

    def last_used(i, nu):
        return jnp.minimum(i, nu[0] - 1)

    def f_eff(i, f, nu):
        return jnp.where(i < nu[0], f, nf - 1)

    y = pl.pallas_call(
        _moe_ffn_kernel,
        grid_spec=pltpu.PrefetchScalarGridSpec(
            num_scalar_prefetch=2,
            grid=(nt, nf),
            in_specs=[
                pl.BlockSpec((bm, dx), lambda i, f, te, nu: (last_used(i, nu), 0)),
                pl.BlockSpec((None, d, fk), lambda i, f, te, nu: (te[i], 0, f_eff(i, f, nu))),
                pl.BlockSpec((None, d, fk), lambda i, f, te, nu: (te[i], 0, nf + f_eff(i, f, nu))),
                pl.BlockSpec((None, fk, d), lambda i, f, te, nu: (te[i], f_eff(i, f, nu), 0)),
            ],
            out_specs=pl.BlockSpec((bm, d), lambda i, f, te, nu: (last_used(i, nu), 0)),
            scratch_shapes=[pltpu.VMEM((bm, d), F32)],
        ),
        out_shape=jax.ShapeDtypeStruct((ns, d), BF16),
        compiler_params=_cparams("arbitrary", "arbitrary"),
        name="moe_ffn",
    )(tile_expert, n_used.reshape(1), xs, w13, w13, w2)

    nc = t // cch
    cs0 = jnp.concatenate([jnp.zeros((ne, 1), i32), csum], axis=1)[:, ::cch]
    before, through = cs0[:, :-1].T, cs0[:, 1:].T
    t_lo = (start[None] + before) // bm
    t_hi = (start[None] + through - 1) // bm
    n_tl = jnp.where(through > before, t_hi - t_lo + 1, 0).reshape(-1)
    ni_c = ne * nc + nt
    c_pair, c_local, c_valid = _expand_items(n_tl, ni_c)
    c_chunk = c_pair // ne
    c_tile = t_lo.reshape(-1)[c_pair] + c_local
    off = jnp.cumsum(n_tl)
    k_eff = jnp.minimum(jnp.arange(ni_c, dtype=i32), off[-1] - 1)
    chunk_first = (off - n_tl)[c_chunk * ne]
    chunk_last = off[c_chunk * ne + ne - 1] - 1
    c_flag = c_valid * (_VALID + _FIRST * (k_eff == chunk_first) + _LAST * (k_eff == chunk_last))

    cpb = ta // cch
    kern = functools.partial(_moe_combine_kernel, n_ctx=n_ctx, chunks_per_batch=cpb)
    out = pl.pallas_call(
        kern,
        grid_spec=pltpu.PrefetchScalarGridSpec(
            num_scalar_prefetch=3,
            grid=(ni_c,),
            in_specs=[
                pl.BlockSpec((cch, 1), lambda k, cn, tl, fl: (cn[k], 0)),
                pl.BlockSpec((cch, 1), lambda k, cn, tl, fl: (cn[k], 0)),
                pl.BlockSpec((bm, d), lambda k, cn, tl, fl: (tl[k], 0)),
                pl.BlockSpec((cch, d), lambda k, cn, tl, fl: (cn[k], 0)),
                pl.BlockSpec((None, None, None, 1, d), lambda k, cn, tl, fl: (cn[k] // cpb, 0, k_gate, 0, 0)),
                pl.BlockSpec((None, None, None, 1, d), lambda k, cn, tl, fl: (cn[k] // cpb, 1, k_gate, 0, 0)),
            ],
            out_specs=pl.BlockSpec((cch, d), lambda k, cn, tl, fl: (cn[k], 0)),
            scratch_shapes=[pltpu.VMEM((cch, d), F32)],
        ),
        out_shape=jax.ShapeDtypeStruct((t, d), F32),
        input_output_aliases={6: 0},
        compiler_params=_cparams("arbitrary"),
        name="moe_combine",
    )(c_chunk, c_tile, c_flag.astype(i32), pos1.reshape(t, 1), pos2.reshape(t, 1), y, xa.reshape(t, d), mods, mods)
    return out.reshape(b, ta, d)


def _rope_tables(seq, n_ctx, rot_dim):
    rows = seq // GRID_W
    quarter = rot_dim // 4
    inv_freq = ROPE_THETA ** (-jnp.arange(quarter, dtype=F32) / quarter)
    row = jnp.repeat(jnp.arange(rows, dtype=F32), GRID_W)
    col = jnp.tile(jnp.arange(GRID_W, dtype=F32), rows)
    ang = jnp.concatenate([row[:, None] * inv_freq, col[:, None] * inv_freq], axis=-1)
    cos, sin = jnp.cos(ang), jnp.sin(ang)
    reps = LANES // rot_dim
    c = jnp.tile(jnp.concatenate([cos, cos], axis=-1), (1, reps))
    s = jnp.tile(jnp.concatenate([-sin, sin], axis=-1), (1, reps))
    c = jnp.concatenate([jnp.ones((n_ctx, LANES), F32), c], axis=0)
    s = jnp.concatenate([jnp.zeros((n_ctx, LANES), F32), s], axis=0)
    return c, s


def kernel(x, c, ctx, c_ctx, ada_w, ada_b, norm_mix, norm_ffn, gqa_wqkv, gqa_q_gain, gqa_k_gain, gqa_wo, mla_wdown, mla_qa_gain, mla_kva_gain, mla_wuq, mla_wukv, mla_q_gain, mla_k_gain, mla_wo, win_wqkv, win_q_gain, win_k_gain, win_sink, win_wo, diff_wqkv, diff_q_gain, diff_k_gain, diff_lambda, diff_subln, diff_wo, ffn_w13, ffn_w2, moe_router, moe_w13, moe_w2):
    b, seq, d = x.shape
    n_ctx = ctx.shape[1]
    depth = ada_w.shape[0]
    ones = lambda n: jnp.ones((n,), F32)
    zeros = lambda n: jnp.zeros((n,), F32)

    crows = -(-(b + 1) // 8) * 8
    cond = jnp.zeros((crows, d), F32).at[:b].set(c).at[b].set(c_ctx)
    mods_all = adaln(cond, ada_w, ada_b)

    xa = jnp.concatenate([ctx, x], axis=1)

    for i in range(depth):
        need_ctx = i < depth - 1
        j = i // 4
        kind = i % 4
        ml = mods_all[i, :b].reshape(b, 6, 1, d)
        mc = jnp.broadcast_to(mods_all[i, b].reshape(1, 6, 1, d), (b, 6, 1, d))
        mods = jnp.stack([mc, ml], axis=1)

        prenorm = (norm_mix[i], mods, 0, 1, n_ctx)
        if kind == 0:
            hd = GQA_HEAD_DIM
            scale = hd ** -0.5 * LOG2E
            nq, nk = GQA_HEADS * hd, GQA_KV_HEADS * hd
            groups = [(g * hd, hd) for g in range(GQA_HEADS + GQA_KV_HEADS)]
            gain = jnp.concatenate([jnp.tile(gqa_q_gain[j] * scale, GQA_HEADS),
                                    jnp.tile(gqa_k_gain[j], GQA_KV_HEADS), ones(nk)])
            qkv = proj(xa, 0, gqa_wqkv[j].astype(BF16), groups, gain, _rope_tables(seq, n_ctx, hd),
                       range((nq + nk) // LANES), hd // 2, "gqa_qkv", prenorm)
            o = gqa_attention(qkv, n_ctx)
            w_o = gqa_wo[j]
        elif kind == 1:
            scale = (MLA_NOPE + MLA_ROPE) ** -0.5 * LOG2E
            hh = MLA_HEADS
            wd = mla_wdown[j]
            pad = jnp.zeros((d, LANES - MLA_ROPE), F32)
            wd2 = jnp.concatenate([wd[:, :MLA_Q_LORA], wd[:, MLA_Q_LORA + MLA_KV_LORA:], pad,
                                   wd[:, MLA_Q_LORA:MLA_Q_LORA + MLA_KV_LORA]], axis=1)
            kpe0 = MLA_Q_LORA
            ckv0 = MLA_Q_LORA + LANES
            groups = [(0, MLA_Q_LORA), (kpe0, MLA_ROPE), (ckv0, MLA_KV_LORA)]
            gain = jnp.concatenate([mla_qa_gain[j], mla_k_gain[j][MLA_NOPE:], zeros(LANES - MLA_ROPE),
                                    mla_kva_gain[j]])
            rope = _rope_tables(seq, n_ctx, MLA_ROPE)
            dn = proj(xa, 0, wd2.astype(BF16), groups, gain, rope, [kpe0 // LANES], MLA_ROPE // 2, "mla_down",
                      prenorm)
            wq = mla_wuq[j].reshape(MLA_Q_LORA, hh, MLA_NOPE + MLA_ROPE)
            wq2 = jnp.pad(wq, ((0, 0), (0, 0), (0, LANES - MLA_ROPE))).reshape(MLA_Q_LORA, hh * 2 * LANES)
            groups = ([(g * 2 * LANES, MLA_NOPE) for g in range(hh)]
                      + [(g * 2 * LANES + MLA_NOPE, MLA_ROPE) for g in range(hh)])
            qg = jnp.concatenate([mla_q_gain[j] * scale, zeros(LANES - MLA_ROPE)])
            gain = jnp.tile(qg, hh)
            q = proj(dn, 0, wq2.astype(BF16), groups, gain, rope, range(1, 2 * hh, 2), MLA_ROPE // 2, "mla_q")
            groups = [(g * (MLA_NOPE + MLA_V), MLA_NOPE) for g in range(hh)]
            gain = jnp.tile(jnp.concatenate([mla_k_gain[j][:MLA_NOPE], ones(MLA_V)]), hh)
            kv = proj(dn, ckv0 // MLA_KV_LORA, mla_wukv[j].astype(BF16), groups, gain, rope, [], MLA_ROPE // 2, "mla_kv")
            o = mla_attention(q, kv, dn, kpe0 // LANES, n_ctx)
            w_o = mla_wo[j]
        elif kind == 2:
            hd = WIN_HEAD_DIM
            scale = hd ** -0.5 * LOG2E
            n_pairs = WIN_HEADS // WIN_KV_HEADS
            nq = WIN_HEADS * hd
            perm = np.array([(kv * n_pairs + g) * hd + t for g in range(n_pairs)
                             for kv in range(WIN_KV_HEADS) for t in range(hd)])
            wq = win_wqkv[j]
            w2_ = jnp.concatenate([wq[:, perm], wq[:, nq:]], axis=1)
            groups = [(g * hd, hd) for g in range(WIN_HEADS + WIN_KV_HEADS)]
            gain = jnp.concatenate([jnp.tile(win_q_gain[j] * scale, WIN_HEADS),
                                    jnp.tile(win_k_gain[j], WIN_KV_HEADS), ones(WIN_KV_HEADS * hd)])
            qkv = proj(xa, 0, w2_.astype(BF16), groups, gain, _rope_tables(seq, n_ctx, hd),
                       range(n_pairs + 1), hd // 2, "win_qkv", prenorm)
            o = win_attention(qkv, win_sink[j] * LOG2E, n_ctx)
            w_o = win_wo[j][perm, :]
        else:
            hd = DIFF_HEAD_DIM
            scale = hd ** -0.5 * LOG2E
            lam_init = 0.8 - 0.6 * math.exp(-0.3 * i)
            nqk = 2 * DIFF_HEADS
            groups = [(g * hd, hd) for g in range(2 * nqk)]
            gain = jnp.concatenate([jnp.tile(diff_q_gain[j] * scale, nqk), jnp.tile(diff_k_gain[j], nqk),
                                    ones(DIFF_HEADS * 2 * hd)])
            qkv = proj(xa, 0, diff_wqkv[j].astype(BF16), groups, gain, _rope_tables(seq, n_ctx, hd),
                       range(2 * DIFF_HEADS), hd // 2, "diff_qkv", prenorm)
            o = diff_attention(qkv, diff_lambda[j], diff_subln[j], lam_init, n_ctx)
            w_o = diff_wo[j]
        xa = out_proj_residual(o, w_o.astype(BF16), xa, mods, 2, n_ctx)

        jj = i // 2
        if i % 2 == 0:
            xa = ffn_residual(norm_ffn[i], ffn_w13[jj].astype(BF16), ffn_w2[jj].astype(BF16), xa, mods, 3, 4, 5,
                              n_ctx)
        else:
            h, route = norm_mod_router(xa, norm_ffn[i], mods, 3, 4, n_ctx, moe_router[jj])
            xa = moe_residual(h, route, moe_w13[jj].astype(BF16), moe_w2[jj].astype(BF16), xa, mods, 5,
                              n_ctx, need_ctx)
    return xa[:, n_ctx:]
```

```python
import functools
import math

import jax
import jax.numpy as jnp
import numpy as np
from jax import lax
from jax.experimental import pallas as pl
from jax.experimental.pallas import tpu as pltpu

GRID_W = 64
ROPE_THETA = 10000.0
EPS = 1e-6
NEG_INF = -1e30
LANES = 128
MXU_WIDTH = 256
ROW_TILE = 256
FFN_ROW_TILE = 768
FF_CHUNK = 512
VMEM_LIMIT = 56 * 1024 * 1024

GQA_HEADS, GQA_KV_HEADS, GQA_HEAD_DIM = 8, 2, 128
MLA_HEADS, MLA_Q_LORA, MLA_KV_LORA, MLA_NOPE, MLA_ROPE, MLA_V = 8, 384, 256, 128, 64, 128
WIN_HEADS, WIN_KV_HEADS, WIN_HEAD_DIM, WINDOW = 16, 2, 64, 128
DIFF_HEADS, DIFF_HEAD_DIM = 8, 64
N_EXPERTS = 8

F32 = jnp.float32
BF16 = jnp.bfloat16


def _cparams(*sem):
    return pltpu.CompilerParams(dimension_semantics=sem, vmem_limit_bytes=VMEM_LIMIT)


def _dot(a, b):
    return jnp.dot(a, b, preferred_element_type=F32)


def _dot_nt(a, b):
    return lax.dot_general(a, b, (((1,), (1,)), ((), ())), preferred_element_type=F32)


def _split_bf16(v):
    hi = v.astype(BF16)
    lo = (v - hi.astype(F32)).astype(BF16)
    return hi, lo


def _adaln_kernel(c_ref, w_ref, b_ref, o_ref):
    c = c_ref[...]
    sc = c * (1.0 / (1.0 + jnp.exp(-c)))
    o_ref[0] = jnp.dot(sc, w_ref[0], preferred_element_type=F32,
                       precision=lax.Precision.HIGHEST) + b_ref[0]


def adaln(cond, ada_w, ada_b):
    depth, d, n = ada_w.shape
    rows = cond.shape[0]
    bn = 1536
    return pl.pallas_call(
        _adaln_kernel,
        grid=(depth, n // bn),
        in_specs=[
            pl.BlockSpec((rows, d), lambda l, j: (0, 0)),
            pl.BlockSpec((1, d, bn), lambda l, j: (l, 0, j)),
            pl.BlockSpec((1, 1, bn), lambda l, j: (l, 0, j)),
        ],
        out_specs=pl.BlockSpec((1, rows, bn), lambda l, j: (l, 0, j)),
        out_shape=jax.ShapeDtypeStruct((depth, rows, n), F32),
        compiler_params=_cparams("parallel", "parallel"),
        name="adaln",
    )(cond, ada_w, ada_b.reshape(depth, 1, n))


def _mod_spec(k, n_ctx_tiles, d):
    return pl.BlockSpec((None, None, None, 1, d),
                        lambda b, i, *_: (b, jnp.minimum(i // n_ctx_tiles, 1), k, 0, 0))


def _norm_mod(x, g, shift, scale):
    ms = jnp.mean(x * x, axis=-1, keepdims=True)
    return x * lax.rsqrt(ms + EPS) * g * (1.0 + scale) + shift


def _norm_mod_router_kernel(x_ref, g_ref, sh_ref, sc_ref, r_ref, h_ref, cw_ref):
    d = x_ref.shape[2]
    h = _norm_mod(x_ref[0], g_ref[...], sh_ref[...], sc_ref[...])
    h_ref[0, :, :d] = h.astype(BF16)
    logits = jnp.dot(h, r_ref[...], preferred_element_type=F32, precision=lax.Precision.HIGHEST)
    lane = lax.broadcasted_iota(jnp.int32, logits.shape, 1)
    logits = jnp.where(lane < N_EXPERTS, logits, -jnp.inf)
    m1 = jnp.max(logits, axis=-1, keepdims=True)
    i1 = jnp.min(jnp.where(logits == m1, lane, LANES), axis=-1, keepdims=True)
    rest = jnp.where(lane == i1, -jnp.inf, logits)
    m2 = jnp.max(rest, axis=-1, keepdims=True)
    i2 = jnp.min(jnp.where(rest == m2, lane, LANES), axis=-1, keepdims=True)
    e2 = jnp.exp(m2 - m1)
    den = 1.0 + e2
    g1, g2 = 1.0 / den, e2 / den
    cw_ref[0] = jnp.where(lane == 0, i1.astype(F32), jnp.where(lane == 1, i2.astype(F32),
                          jnp.where(lane == 2, g1, jnp.where(lane == 3, g2, 0.0))))
    g1_hi, g2_hi = g1.astype(BF16).astype(F32), g2.astype(BF16).astype(F32)
    ext = jnp.where(lane == ROUTE_G1, g1_hi, jnp.where(lane == ROUTE_G1 + 1, g1 - g1_hi,
          jnp.where(lane == ROUTE_G2, g2_hi, jnp.where(lane == ROUTE_G2 + 1, g2 - g2_hi,
          jnp.where(lane == ROUTE_E1, i1.astype(F32), jnp.where(lane == ROUTE_E2, i2.astype(F32), 0.0))))))
    h_ref[0, :, d:] = ext.astype(BF16)


ROUTE_G1, ROUTE_G2, ROUTE_E1, ROUTE_E2 = 0, 2, 4, 5


def norm_mod_router(xa, gain, mods, k_shift, k_scale, n_ctx, router):
    b, ta, d = xa.shape
    bm = ROW_TILE
    nct = n_ctx // bm
    rpad = jnp.zeros((d, LANES), F32).at[:, :N_EXPERTS].set(router)
    return pl.pallas_call(
        _norm_mod_router_kernel, grid=(b, ta // bm),
        in_specs=[
            pl.BlockSpec((1, bm, d), lambda b_, i: (b_, i, 0)),
            pl.BlockSpec((1, d), lambda b_, i: (0, 0)),
            _mod_spec(k_shift, nct, d),
            _mod_spec(k_scale, nct, d),
            pl.BlockSpec((d, LANES), lambda b_, i: (0, 0)),
        ],
        out_specs=[pl.BlockSpec((1, bm, d + LANES), lambda b_, i: (b_, i, 0)),
                   pl.BlockSpec((1, bm, LANES), lambda b_, i: (b_, i, 0))],
        out_shape=[jax.ShapeDtypeStruct((b, ta, d + LANES), BF16), jax.ShapeDtypeStruct((b, ta, LANES), F32)],
        compiler_params=_cparams("parallel", "parallel"), name="norm_mod_router",
    )(xa, gain.reshape(1, d), mods, mods, rpad)


def _proj_kernel(*refs, rope_blocks, rope_half, prenorm, norm_blocks):
    if prenorm:
        x_ref, g_ref, sh_ref, sc_ref = refs[:4]
        h = _norm_mod(x_ref[0], g_ref[...], sh_ref[...], sc_ref[...]).astype(BF16)
        refs = refs[4:]
    else:
        h = refs[0][0]
        refs = refs[1:]
    y = _dot(h, refs[0][...])
    n = y.shape[1]
    s_hi, s_lo = _split_bf16(y * y)
    if norm_blocks is None:
        _, e_ref, et_ref, igs_ref, u_ref, gain_ref, c_ref, s_ref, o_ref = refs
        ss = _dot(s_hi, e_ref[...]) + _dot(s_lo, e_ref[...])
        inv = lax.rsqrt(ss * igs_ref[...] + EPS)
        i_hi, i_lo = _split_bf16(inv)
        fac = _dot(i_hi, et_ref[...]) + _dot(i_lo, et_ref[...]) + u_ref[...]
        z = y * (fac * gain_ref[...])
    else:
        _, j_ref, igs_ref, u_ref, gain_ref, c_ref, s_ref, o_ref = refs
        zs = []
        for blk in range(n // MXU_WIDTH):
            sl = slice(blk * MXU_WIDTH, (blk + 1) * MXU_WIDTH)
            if blk in norm_blocks:
                ss = _dot(s_hi[:, sl], j_ref[blk]) + _dot(s_lo[:, sl], j_ref[blk])
                u = u_ref[:, sl]
                fac = lax.rsqrt(ss * igs_ref[:, sl] + EPS) * (1.0 - u) + u
                zs.append(y[:, sl] * (fac * gain_ref[:, sl]))
            else:
                zs.append(y[:, sl] * gain_ref[:, sl])
        z = jnp.concatenate(zs, axis=1)
    cos = c_ref[...]
    sin = s_ref[...]
    if rope_half == LANES // 2:
        first = None
    else:
        lane = lax.broadcasted_iota(jnp.int32, cos.shape, 1)
        first = (lane % (2 * rope_half)) < rope_half
    for blk in range(n // LANES):
        zb = z[:, blk * LANES:(blk + 1) * LANES]
        if blk in rope_blocks:
            if first is None:
                rot = pltpu.roll(zb, LANES // 2, 1)
            else:
                rot = jnp.where(first, pltpu.roll(zb, LANES - rope_half, 1), pltpu.roll(zb, rope_half, 1))
            zb = zb * cos + rot * sin
        o_ref[0, :, blk * LANES:(blk + 1) * LANES] = zb.astype(BF16)


def proj(h, kin_block, w, groups, gain, rope_tabs, rope_blocks, rope_half, name, prenorm=None):
    b, ta, _ = h.shape
    k, n = w.shape
    bm = ROW_TILE
    mw = MXU_WIDTH
    u = np.ones((1, n), np.float32)
    for start, size in groups:
        u[0, start:start + size] = 0.0
    local = n % mw == 0 and all(start // mw == (start + size - 1) // mw for start, size in groups)
    if local:
        jm = np.zeros((n // mw, mw, mw), np.float32)
        igs = np.zeros((1, n), np.float32)
        for start, size in groups:
            blk, o = divmod(start, mw)
            jm[blk, o:o + size, o:o + size] = 1.0
            igs[0, start:start + size] = 1.0 / size
        norm_blocks = frozenset(start // mw for start, _ in groups)
        norm_args = [jnp.asarray(jm, BF16), jnp.asarray(igs)]
        norm_specs = [pl.BlockSpec((n // mw, mw, mw), lambda b_, i: (0, 0, 0)),
                      pl.BlockSpec((1, n), lambda b_, i: (0, 0))]
    else:
        e = np.zeros((n, LANES), np.float32)
        igs = np.zeros((1, LANES), np.float32)
        for gi, (start, size) in enumerate(groups):
            e[start:start + size, gi] = 1.0
            igs[0, gi] = 1.0 / size
        norm_blocks = None
        norm_args = [jnp.asarray(e, BF16), jnp.asarray(e.T, BF16), jnp.asarray(igs)]
        norm_specs = [pl.BlockSpec((n, LANES), lambda b_, i: (0, 0)),
                      pl.BlockSpec((LANES, n), lambda b_, i: (0, 0)),
                      pl.BlockSpec((1, LANES), lambda b_, i: (0, 0))]
    cos, sin = rope_tabs
    kern = functools.partial(_proj_kernel, rope_blocks=frozenset(rope_blocks), rope_half=rope_half,
                             prenorm=prenorm is not None, norm_blocks=norm_blocks)
    lead_specs = [pl.BlockSpec((1, bm, k), lambda b_, i: (b_, i, kin_block))]
    lead_args = [h]
    if prenorm is not None:
        ngain, mods, k_shift, k_scale, n_ctx = prenorm
        nct = n_ctx // bm
        lead_specs += [pl.BlockSpec((1, k), lambda b_, i: (0, 0)), _mod_spec(k_shift, nct, k),
                       _mod_spec(k_scale, nct, k)]
        lead_args += [ngain.reshape(1, k), mods, mods]
    return pl.pallas_call(
        kern,
        grid=(b, ta // bm),
        in_specs=lead_specs + [pl.BlockSpec((k, n), lambda b_, i: (0, 0))] + norm_specs + [
            pl.BlockSpec((1, n), lambda b_, i: (0, 0)),
            pl.BlockSpec((1, n), lambda b_, i: (0, 0)),
            pl.BlockSpec((bm, LANES), lambda b_, i: (i, 0)),
            pl.BlockSpec((bm, LANES), lambda b_, i: (i, 0)),
        ],
        out_specs=pl.BlockSpec((1, bm, n), lambda b_, i: (b_, i, 0)),
        out_shape=jax.ShapeDtypeStruct((b, ta, n), BF16),
        compiler_params=_cparams("parallel", "parallel"),
        name=name,
    )(*lead_args, w, *norm_args, jnp.asarray(u), gain.reshape(1, n).astype(F32), cos, sin)


LOG2E = math.log2(math.e)


def _softmax_pv(s, v):
    m = jnp.max(s, axis=-1, keepdims=True)
    p = jnp.exp2(s - m)
    l = jnp.sum(p, axis=-1, keepdims=True)
    return _dot(p.astype(BF16), v) / l


def _gqa_attn_kernel(q_ref, k_ref, v_ref, o_ref, *, n_ctx, group, hd):
    i = pl.program_id(2)

    def attend(nk):
        k = k_ref[0, :nk, :]
        v = v_ref[0, :nk, :]
        for g in range(group):
            q = q_ref[0, :, g * hd:(g + 1) * hd]
            o = _softmax_pv(_dot_nt(q, k), v)
            o_ref[0, :, g * hd:(g + 1) * hd] = o.astype(BF16)

    nct = n_ctx // ROW_TILE

    @pl.when(i < nct)
    def _():
        attend(n_ctx)

    @pl.when(i >= nct)
    def _():
        attend(k_ref.shape[1])


def gqa_attention(qkv, n_ctx):
    b, ta, _ = qkv.shape
    hd, group, kvh = GQA_HEAD_DIM, GQA_HEADS // GQA_KV_HEADS, GQA_KV_HEADS
    bq = ROW_TILE
    qw = group * hd
    kb = GQA_HEADS
    vb = GQA_HEADS + kvh
    kern = functools.partial(_gqa_attn_kernel, n_ctx=n_ctx, group=group, hd=hd)
    return pl.pallas_call(
        kern,
        grid=(b, kvh, ta // bq),
        in_specs=[
            pl.BlockSpec((1, bq, qw), lambda b_, h, i: (b_, i, h)),
            pl.BlockSpec((1, ta, hd), lambda b_, h, i: (b_, 0, kb + h)),
            pl.BlockSpec((1, ta, hd), lambda b_, h, i: (b_, 0, vb + h)),
        ],
        out_specs=pl.BlockSpec((1, bq, qw), lambda b_, h, i: (b_, i, h)),
        out_shape=jax.ShapeDtypeStruct((b, ta, GQA_HEADS * hd), BF16),
        compiler_params=_cparams("parallel", "parallel", "arbitrary"),
        name="gqa_attn",
    )(qkv, qkv, qkv)


def _mla_attn_kernel(q_ref, kn_ref, kp_ref, v_ref, o_ref, kcat_ref, *, n_ctx):
    i = pl.program_id(2)

    @pl.when(i == 0)
    def _():
        kcat_ref[:, :LANES] = kn_ref[0]
        kcat_ref[:, LANES:] = kp_ref[0]

    def attend(nk):
        s = _dot_nt(q_ref[0], kcat_ref[:nk, :])
        o_ref[0] = _softmax_pv(s, v_ref[0, :nk, :]).astype(BF16)

    nct = n_ctx // ROW_TILE

    @pl.when(i < nct)
    def _():
        attend(n_ctx)

    @pl.when(i >= nct)
    def _():
        attend(kn_ref.shape[1])


def mla_attention(q, kv, dn, kpe_block, n_ctx):
    b, ta, _ = q.shape
    bq = ROW_TILE
    hh = MLA_HEADS
    kern = functools.partial(_mla_attn_kernel, n_ctx=n_ctx)
    return pl.pallas_call(
        kern,
        grid=(b, hh, ta // bq),
        in_specs=[
            pl.BlockSpec((1, bq, 2 * LANES), lambda b_, h, i: (b_, i, h)),
            pl.BlockSpec((1, ta, LANES), lambda b_, h, i: (b_, 0, 2 * h)),
            pl.BlockSpec((1, ta, LANES), lambda b_, h, i: (b_, 0, kpe_block)),
            pl.BlockSpec((1, ta, LANES), lambda b_, h, i: (b_, 0, 2 * h + 1)),
        ],
        out_specs=pl.BlockSpec((1, bq, LANES), lambda b_, h, i: (b_, i, h)),
        out_shape=jax.ShapeDtypeStruct((b, ta, hh * MLA_V), BF16),
        scratch_shapes=[pltpu.VMEM((ta, 2 * LANES), BF16)],
        compiler_params=_cparams("parallel", "parallel", "arbitrary"),
        name="mla_attn",
    )(q, kv, dn, kv)


def _half_masks(shape):
    lane = lax.broadcasted_iota(jnp.int32, shape, 1)
    return lane < (LANES // 2)


def _diff_attn_kernel(q_ref, k_ref, v_ref, lam_ref, sub_ref, o_ref, *, n_ctx, lam_init):
    i = pl.program_id(2)
    lp = lam_ref[...]
    lam = (jnp.exp(jnp.sum(lp[0:1] * lp[1:2], axis=-1, keepdims=True))
           - jnp.exp(jnp.sum(lp[2:3] * lp[3:4], axis=-1, keepdims=True)) + lam_init)

    def attend(nk):
        k = k_ref[0, :nk, :]
        v = v_ref[0, :nk, :]

        def probs(s, weight):
            m = jnp.max(s, axis=-1, keepdims=True)
            e = jnp.exp2(s - m)
            return e * (weight / jnp.sum(e, axis=-1, keepdims=True))

        q = q_ref[0]
        lo = _half_masks(q.shape)
        zero = jnp.zeros_like(q)
        p = probs(_dot_nt(jnp.where(lo, q, zero), k), 1.0) - probs(_dot_nt(jnp.where(lo, zero, q), k), lam)
        o = _dot(p.astype(BF16), v)
        ms = jnp.mean(o * o, axis=-1, keepdims=True)
        o_ref[0] = (o * lax.rsqrt(ms + EPS) * sub_ref[...] * (1.0 - lam_init)).astype(BF16)

    nct = n_ctx // ROW_TILE

    @pl.when(i < nct)
    def _():
        attend(n_ctx)

    @pl.when(i >= nct)
    def _():
        attend(k_ref.shape[1])


def diff_attention(qkv, lam_p, subln, lam_init, n_ctx):
    b, ta, _ = qkv.shape
    bq = ROW_TILE
    hh = DIFF_HEADS
    kern = functools.partial(_diff_attn_kernel, n_ctx=n_ctx, lam_init=lam_init)
    return pl.pallas_call(
        kern,
        grid=(b, hh, ta // bq),
        in_specs=[
            pl.BlockSpec((1, bq, LANES), lambda b_, h, i: (b_, i, h)),
            pl.BlockSpec((1, ta, LANES), lambda b_, h, i: (b_, 0, hh + h)),
            pl.BlockSpec((1, ta, LANES), lambda b_, h, i: (b_, 0, 2 * hh + h)),
            pl.BlockSpec((4, DIFF_HEAD_DIM), lambda b_, h, i: (0, 0)),
            pl.BlockSpec((1, LANES), lambda b_, h, i: (0, 0)),
        ],
        out_specs=pl.BlockSpec((1, bq, LANES), lambda b_, h, i: (b_, i, h)),
        out_shape=jax.ShapeDtypeStruct((b, ta, hh * 2 * DIFF_HEAD_DIM), BF16),
        compiler_params=_cparams("parallel", "parallel", "arbitrary"),
        name="diff_attn",
    )(qkv, qkv, qkv, lam_p.astype(F32), subln.reshape(1, LANES).astype(F32))


def _win_attn_kernel(sink_ref, q_ref, k_ref, v_ref, o_ref, *, n_ctx, seq, n_pairs):
    i = pl.program_id(1)
    bq = q_ref.shape[1]
    band = bq + 2 * WINDOW
    nct = n_ctx // bq

    def head_out(qm, kv_idx, pair, blocks):
        sk = sink_ref[kv_idx * n_pairs + pair]
        ss = []
        m = None
        for k, _, mask in blocks:
            s = _dot_nt(qm, k)
            if mask is not None:
                s = jnp.where(mask, s, NEG_INF)
            ss.append(s)
            bm_ = jnp.max(s, axis=-1, keepdims=True)
            m = bm_ if m is None else jnp.maximum(m, bm_)
        m = jnp.maximum(m, sk)
        l = jnp.exp2(sk - m)
        o = None
        for s, (_, v, _) in zip(ss, blocks):
            p = jnp.exp2(s - m)
            l = l + jnp.sum(p, axis=-1, keepdims=True)
            pv = _dot(p.astype(BF16), v)
            o = pv if o is None else o + pv
        return o / l

    def run(blocks):
        for pair in range(n_pairs):
            q = q_ref[0, :, pair * LANES:(pair + 1) * LANES]
            lo = _half_masks(q.shape)
            zero = jnp.zeros_like(q)
            o0 = head_out(jnp.where(lo, q, zero), 0, pair, blocks)
            o1 = head_out(jnp.where(lo, zero, q), 1, pair, blocks)
            o_ref[0, :, pair * LANES:(pair + 1) * LANES] = jnp.where(lo, o0, o1).astype(BF16)

    @pl.when(i < nct)
    def _():
        run([(k_ref[0, :n_ctx, :], v_ref[0, :n_ctx, :], None)])

    @pl.when(i >= nct)
    def _():
        q0 = (i - nct) * bq
        start = jnp.clip(q0 - WINDOW, 0, seq - band)
        start = pl.multiple_of(start, WINDOW)
        qpos = q0 + lax.broadcasted_iota(jnp.int32, (bq, band), 0)
        kpos = start + lax.broadcasted_iota(jnp.int32, (bq, band), 1)
        mask = jnp.abs(qpos - kpos) <= WINDOW
        kb = k_ref[0, pl.ds(n_ctx + start, band), :]
        vb = v_ref[0, pl.ds(n_ctx + start, band), :]
        run([(k_ref[0, :n_ctx, :], v_ref[0, :n_ctx, :], None), (kb, vb, mask)])


def win_attention(qkv, sink, n_ctx):
    b, ta, _ = qkv.shape
    bq = ROW_TILE
    n_pairs = WIN_HEADS // WIN_KV_HEADS
    qw = n_pairs * LANES
    kern = functools.partial(_win_attn_kernel, n_ctx=n_ctx, seq=ta - n_ctx, n_pairs=n_pairs)
    return pl.pallas_call(
        kern,
        grid_spec=pltpu.PrefetchScalarGridSpec(
            num_scalar_prefetch=1,
            grid=(b, ta // bq),
            in_specs=[
                pl.BlockSpec((1, bq, qw), lambda b_, i, s: (b_, i, 0)),
                pl.BlockSpec((1, ta, LANES), lambda b_, i, s: (b_, 0, n_pairs)),
                pl.BlockSpec((1, ta, LANES), lambda b_, i, s: (b_, 0, n_pairs + 1)),
            ],
            out_specs=pl.BlockSpec((1, bq, qw), lambda b_, i, s: (b_, i, 0)),
        ),
        out_shape=jax.ShapeDtypeStruct((b, ta, qw), BF16),
        compiler_params=_cparams("parallel", "arbitrary"),
        name="win_attn",
    )(sink.astype(F32), qkv, qkv, qkv)


def _out_proj_kernel(o_ref, w_ref, x_ref, g_ref, y_ref):
    y_ref[0] = x_ref[0] + g_ref[...] * _dot(o_ref[0], w_ref[...])


def out_proj_residual(o, w, xa, mods, k_gate, n_ctx):
    b, ta, d = xa.shape
    kdim = o.shape[2]
    bm = ROW_TILE
    return pl.pallas_call(
        _out_proj_kernel,
        grid=(b, ta // bm),
        in_specs=[
            pl.BlockSpec((1, bm, kdim), lambda b_, i: (b_, i, 0)),
            pl.BlockSpec((kdim, d), lambda b_, i: (0, 0)),
            pl.BlockSpec((1, bm, d), lambda b_, i: (b_, i, 0)),
            _mod_spec(k_gate, n_ctx // bm, d),
        ],
        out_specs=pl.BlockSpec((1, bm, d), lambda b_, i: (b_, i, 0)),
        out_shape=jax.ShapeDtypeStruct(xa.shape, F32),
        input_output_aliases={2: 0},
        compiler_params=_cparams("parallel", "parallel"),
        name="out_proj",
    )(o, w, xa, mods)


def _ffn_kernel(ng_ref, wg_ref, wu_ref, w2_ref, x_ref, mc_ref, ml_ref, y_ref, acc_ref, h_ref, *,
                n_ctx, k_shift, k_scale, k_gate):
    i = pl.program_id(1)
    f = pl.program_id(2)
    bm = acc_ref.shape[0]

    def row_mod(k):
        row = i * bm + lax.broadcasted_iota(jnp.int32, (bm, 1), 0)
        return jnp.where(row < n_ctx, mc_ref[k], ml_ref[k])

    @pl.when(f == 0)
    def _():
        acc_ref[...] = jnp.zeros_like(acc_ref)
        h_ref[...] = _norm_mod(x_ref[0], ng_ref[...], row_mod(k_shift), row_mod(k_scale)).astype(BF16)

    h = h_ref[...]
    g = _dot(h, wg_ref[...])
    u = _dot(h, wu_ref[...])
    act = (g * (1.0 / (1.0 + jnp.exp(-g))) * u).astype(BF16)
    acc_ref[...] += _dot(act, w2_ref[...])

    @pl.when(f == pl.num_programs(2) - 1)
    def _():
        y_ref[0] = x_ref[0] + row_mod(k_gate) * acc_ref[...]


def ffn_residual(norm_gain, w13, w2, xa, mods, k_shift, k_scale, k_gate, n_ctx):
    b, ta, d = xa.shape
    ff = w2.shape[0]
    bm, fk = FFN_ROW_TILE, FF_CHUNK
    nf = ff // fk
    kern = functools.partial(_ffn_kernel, n_ctx=n_ctx, k_shift=k_shift, k_scale=k_scale, k_gate=k_gate)
    in_specs = [
        pl.BlockSpec((1, d), lambda b_, i, f: (0, 0)),
        pl.BlockSpec((d, fk), lambda b_, i, f: (0, f)),
        pl.BlockSpec((d, fk), lambda b_, i, f: (0, nf + f)),
        pl.BlockSpec((fk, d), lambda b_, i, f: (f, 0)),
        pl.BlockSpec((1, bm, d), lambda b_, i, f: (b_, i, 0)),
        pl.BlockSpec((None, None, 6, 1, d), lambda b_, i, f: (b_, 0, 0, 0, 0)),
        pl.BlockSpec((None, None, 6, 1, d), lambda b_, i, f: (b_, 1, 0, 0, 0)),
    ]
    return pl.pallas_call(
        kern,
        grid=(b, ta // bm, nf),
        in_specs=in_specs,
        out_specs=pl.BlockSpec((1, bm, d), lambda b_, i, f: (b_, i, 0)),
        out_shape=jax.ShapeDtypeStruct(xa.shape, F32),
        scratch_shapes=[pltpu.VMEM((bm, d), F32), pltpu.VMEM((bm, d), BF16)],
        input_output_aliases={4: 0},
        compiler_params=_cparams("parallel", "parallel", "arbitrary"),
        name="ffn",
    )(norm_gain.reshape(1, d), w13, w13, w2, xa, mods, mods)


MOE_SLOT_TILE = 512
MOE_GATHER_CHUNK = 512
MOE_COMBINE_CHUNK = 768
_VALID, _FIRST, _LAST = 1, 2, 4


def _one_hot_bf16(cond):
    return jnp.where(cond, 1.0, 0.0).astype(BF16)


def _moe_gather_kernel(tile_ref, chunk_ref, flag_ref, p1_ref, p2_ref, h_ref, o_ref):
    k = pl.program_id(0)
    fl = flag_ref[k]

    def picked():
        bm, ch = o_ref.shape[0], h_ref.shape[0]
        slot = tile_ref[k] * bm + lax.broadcasted_iota(jnp.int32, (bm, ch), 0)
        return _dot(_one_hot_bf16((p1_ref[...] == slot) | (p2_ref[...] == slot)), h_ref[...]).astype(BF16)

    @pl.when((fl & _FIRST) != 0)
    def _():
        o_ref[...] = picked()

    @pl.when(((fl & _VALID) != 0) & ((fl & _FIRST) == 0))
    def _():
        o_ref[...] += picked()


def _moe_ffn_kernel(te_ref, nu_ref, xs_ref, wg_ref, wu_ref, w2_ref, y_ref, acc_ref):
    i = pl.program_id(0)
    f = pl.program_id(1)
    d = acc_ref.shape[1]

    @pl.when(i < nu_ref[0])
    def _():
        @pl.when(f == 0)
        def _():
            acc_ref[...] = jnp.zeros_like(acc_ref)

        xs = xs_ref[:, :d]
        g = _dot(xs, wg_ref[...])
        u = _dot(xs, wu_ref[...])
        act = (g * (1.0 / (1.0 + jnp.exp(-g))) * u).astype(BF16)
        acc_ref[...] += _dot(act, w2_ref[...])

        @pl.when(f == pl.num_programs(1) - 1)
        def _():
            rt = xs_ref[:, d:].astype(F32)
            first = rt[:, ROUTE_E1:ROUTE_E1 + 1] == te_ref[i].astype(F32)
            gate = jnp.where(first, rt[:, ROUTE_G1:ROUTE_G1 + 1] + rt[:, ROUTE_G1 + 1:ROUTE_G1 + 2],
                             rt[:, ROUTE_G2:ROUTE_G2 + 1] + rt[:, ROUTE_G2 + 1:ROUTE_G2 + 2])
            y_ref[...] = (acc_ref[...] * gate).astype(BF16)


def _moe_combine_kernel(chunk_ref, tile_ref, flag_ref, p1_ref, p2_ref, y_ref, x_ref, gc_ref, gl_ref,
                        o_ref, acc_ref, *, n_ctx, chunks_per_batch):
    k = pl.program_id(0)
    fl = flag_ref[k]
    ch, bm = acc_ref.shape[0], y_ref.shape[0]

    @pl.when((fl & _FIRST) != 0)
    def _():
        acc_ref[...] = jnp.zeros_like(acc_ref)

    @pl.when((fl & _VALID) != 0)
    def _():
        slot = tile_ref[k] * bm + lax.broadcasted_iota(jnp.int32, (ch, bm), 1)
        w = _one_hot_bf16((p1_ref[...] == slot) | (p2_ref[...] == slot))
        acc_ref[...] += _dot(w, y_ref[...])

    @pl.when((fl & _LAST) != 0)
    def _():
        row = (chunk_ref[k] % chunks_per_batch) * ch + lax.broadcasted_iota(jnp.int32, (ch, 1), 0)
        gate = jnp.where(row < n_ctx, gc_ref[...], gl_ref[...])
        o_ref[...] = x_ref[...] + gate * acc_ref[...]


def _expand_items(counts, n_items):
    off = jnp.cumsum(counts)
    total = off[-1]
    k = jnp.minimum(jnp.arange(n_items, dtype=jnp.int32), total - 1)
    grp = jnp.minimum(jnp.sum(off[None, :] <= k[:, None], axis=1).astype(jnp.int32), counts.shape[0] - 1)
    local = k - (off[grp] - counts[grp])
    valid = jnp.arange(n_items, dtype=jnp.int32) < total
    return grp, local, valid


def moe_residual(h, route, w13, w2, xa, mods, k_gate, n_ctx, ctx_active):
    b, ta, d = xa.shape
    t = b * ta
    ne, ff = w2.shape[0], w2.shape[1]
    bm, gch, cch, fk = MOE_SLOT_TILE, MOE_GATHER_CHUNK, MOE_COMBINE_CHUNK, FF_CHUNK
    nf = ff // fk
    nt = (2 * t) // bm + ne
    ns = nt * bm
    i32 = jnp.int32

    dx = h.shape[2]
    r = route.reshape(t, LANES)
    e1, e2 = r[:, 0].astype(i32), r[:, 1].astype(i32)
    tok = jnp.arange(t, dtype=i32)
    active = jnp.ones((t,), bool) if ctx_active else (tok % ta) >= n_ctx

    eid = jnp.arange(ne, dtype=i32)[:, None]
    member = ((e1[None] == eid) | (e2[None] == eid)) & active[None]
    csum = jnp.cumsum(member.astype(i32), axis=1)
    cap = (csum[:, -1] + bm - 1) // bm * bm
    end = jnp.cumsum(cap)
    start = end - cap
    n_used = (end[-1] // bm).astype(i32)

    def slot_of(e_sel):
        rank = jnp.take_along_axis(csum, e_sel[None], axis=0)[0] - 1
        return jnp.where(active, start[e_sel] + rank, -1)

    pos1, pos2 = slot_of(e1), slot_of(e2)
    tile_ids = jnp.arange(nt, dtype=i32)
    tile_expert = jnp.minimum(jnp.sum(end[None] <= (tile_ids * bm)[:, None], axis=1), ne - 1).astype(i32)

    r0 = tile_ids * bm - start[tile_expert]
    r1 = jnp.minimum(r0 + bm, csum[tile_expert, -1]) - 1
    ngc = t // gch
    through_g = csum[:, gch - 1::gch][tile_expert]
    c_lo = jnp.minimum(jnp.sum(through_g <= r0[:, None], axis=1), ngc - 1).astype(i32)
    c_hi = jnp.minimum(jnp.sum(through_g <= r1[:, None], axis=1), ngc - 1).astype(i32)
    n_ch = jnp.where(tile_ids < n_used, c_hi - c_lo + 1, 0)
    ni_g = ne * (t // gch) + nt
    g_tile, g_local, g_valid = _expand_items(n_ch, ni_g)
    g_chunk = c_lo[g_tile] + g_local
    g_flag = g_valid * (_VALID + _FIRST * (g_local == 0) + _LAST * (g_local == n_ch[g_tile] - 1))

    xs = pl.pallas_call(
        _moe_gather_kernel,
        grid_spec=pltpu.PrefetchScalarGridSpec(
            num_scalar_prefetch=3,
            grid=(ni_g,),
            in_specs=[
                pl.BlockSpec((None, 1, gch), lambda k, tl, cn, fl: (cn[k], 0, 0)),
                pl.BlockSpec((None, 1, gch), lambda k, tl, cn, fl: (cn[k], 0, 0)),
                pl.BlockSpec((gch, dx), lambda k, tl, cn, fl: (cn[k], 0)),
            ],
            out_specs=pl.BlockSpec((bm, dx), lambda k, tl, cn, fl: (tl[k], 0)),
        ),
        out_shape=jax.ShapeDtypeStruct((ns, dx), BF16),
        compiler_params=_cparams("arbitrary"),
        name="moe_gather",
    )(g_tile, g_chunk, g_flag.astype(i32), pos1.reshape(t // gch, 1, gch), pos2.reshape(t // gch, 1, gch),
      h.reshape(t, dx))

    def last_used(i, nu):
        return jnp.minimum(i, nu[0] - 1)

    def f_eff(i, f, nu):
        return jnp.where(i < nu[0], f, nf - 1)

    y = pl.pallas_call(
        _moe_ffn_kernel,
        grid_spec=pltpu.PrefetchScalarGridSpec(
            num_scalar_prefetch=2,
            grid=(nt, nf),
            in_specs=[
                pl.BlockSpec((bm, dx), lambda i, f, te, nu: (last_used(i, nu), 0)),
                pl.BlockSpec((None, d, fk), lambda i, f, te, nu: (te[i], 0, f_eff(i, f, nu))),
                pl.BlockSpec((None, d, fk), lambda i, f, te, nu: (te[i], 0, nf + f_eff(i, f, nu))),
                pl.BlockSpec((None, fk, d), lambda i, f, te, nu: (te[i], f_eff(i, f, nu), 0)),
            ],
            out_specs=pl.BlockSpec((bm, d), lambda i, f, te, nu: (last_used(i, nu), 0)),
            scratch_shapes=[pltpu.VMEM((bm, d), F32)],
        ),
        out_shape=jax.ShapeDtypeStruct((ns, d), BF16),
        compiler_params=_cparams("arbitrary", "arbitrary"),
        name="moe_ffn",
    )(tile_expert, n_used.reshape(1), xs, w13, w13, w2)

    nc = t // cch
    cs0 = jnp.concatenate([jnp.zeros((ne, 1), i32), csum], axis=1)[:, ::cch]
    before, through = cs0[:, :-1].T, cs0[:, 1:].T
    t_lo = (start[None] + before) // bm
    t_hi = (start[None] + through - 1) // bm
    n_tl = jnp.where(through > before, t_hi - t_lo + 1, 0).reshape(-1)
    ni_c = ne * nc + nt
    c_pair, c_local, c_valid = _expand_items(n_tl, ni_c)
    c_chunk = c_pair // ne
    c_tile = t_lo.reshape(-1)[c_pair] + c_local
    off = jnp.cumsum(n_tl)
    k_eff = jnp.minimum(jnp.arange(ni_c, dtype=i32), off[-1] - 1)
    chunk_first = (off - n_tl)[c_chunk * ne]
    chunk_last = off[c_chunk * ne + ne - 1] - 1
    c_flag = c_valid * (_VALID + _FIRST * (k_eff == chunk_first) + _LAST * (k_eff == chunk_last))

    cpb = ta // cch
    kern = functools.partial(_moe_combine_kernel, n_ctx=n_ctx, chunks_per_batch=cpb)
    out = pl.pallas_call(
        kern,
        grid_spec=pltpu.PrefetchScalarGridSpec(
            num_scalar_prefetch=3,
            grid=(ni_c,),
            in_specs=[
                pl.BlockSpec((cch, 1), lambda k, cn, tl, fl: (cn[k], 0)),
                pl.BlockSpec((cch, 1), lambda k, cn, tl, fl: (cn[k], 0)),
                pl.BlockSpec((bm, d), lambda k, cn, tl, fl: (tl[k], 0)),
                pl.BlockSpec((cch, d), lambda k, cn, tl, fl: (cn[k], 0)),
                pl.BlockSpec((None, None, None, 1, d), lambda k, cn, tl, fl: (cn[k] // cpb, 0, k_gate, 0, 0)),
                pl.BlockSpec((None, None, None, 1, d), lambda k, cn, tl, fl: (cn[k] // cpb, 1, k_gate, 0, 0)),
            ],
            out_specs=pl.BlockSpec((cch, d), lambda k, cn, tl, fl: (cn[k], 0)),
            scratch_shapes=[pltpu.VMEM((cch, d), F32)],
        ),
        out_shape=jax.ShapeDtypeStruct((t, d), F32),
        input_output_aliases={6: 0},
        compiler_params=_cparams("arbitrary"),
        name="moe_combine",
    )(c_chunk, c_tile, c_flag.astype(i32), pos1.reshape(t, 1), pos2.reshape(t, 1), y, xa.reshape(t, d), mods, mods)
    return out.reshape(b, ta, d)


def _rope_tables(seq, n_ctx, rot_dim):
    rows = seq // GRID_W
    quarter = rot_dim // 4
    inv_freq = ROPE_THETA ** (-jnp.arange(quarter, dtype=F32) / quarter)
    row = jnp.repeat(jnp.arange(rows, dtype=F32), GRID_W)
    col = jnp.tile(jnp.arange(GRID_W, dtype=F32), rows)
    ang = jnp.concatenate([row[:, None] * inv_freq, col[:, None] * inv_freq], axis=-1)
    cos, sin = jnp.cos(ang), jnp.sin(ang)
    reps = LANES // rot_dim
    c = jnp.tile(jnp.concatenate([cos, cos], axis=-1), (1, reps))
    s = jnp.tile(jnp.concatenate([-sin, sin], axis=-1), (1, reps))
    c = jnp.concatenate([jnp.ones((n_ctx, LANES), F32), c], axis=0)
    s = jnp.concatenate([jnp.zeros((n_ctx, LANES), F32), s], axis=0)
    return c, s


def kernel(x, c, ctx, c_ctx, ada_w, ada_b, norm_mix, norm_ffn, gqa_wqkv, gqa_q_gain, gqa_k_gain, gqa_wo, mla_wdown, mla_qa_gain, mla_kva_gain, mla_wuq, mla_wukv, mla_q_gain, mla_k_gain, mla_wo, win_wqkv, win_q_gain, win_k_gain, win_sink, win_wo, diff_wqkv, diff_q_gain, diff_k_gain, diff_lambda, diff_subln, diff_wo, ffn_w13, ffn_w2, moe_router, moe_w13, moe_w2):
    b, seq, d = x.shape
    n_ctx = ctx.shape[1]
    depth = ada_w.shape[0]
    ones = lambda n: jnp.ones((n,), F32)
    zeros = lambda n: jnp.zeros((n,), F32)

    crows = -(-(b + 1) // 8) * 8
    cond = jnp.zeros((crows, d), F32).at[:b].set(c).at[b].set(c_ctx)
    mods_all = adaln(cond, ada_w, ada_b)

    xa = jnp.concatenate([ctx, x], axis=1)

    for i in range(depth):
        need_ctx = i < depth - 1
        j = i // 4
        kind = i % 4
        ml = mods_all[i, :b].reshape(b, 6, 1, d)
        mc = jnp.broadcast_to(mods_all[i, b].reshape(1, 6, 1, d), (b, 6, 1, d))
        mods = jnp.stack([mc, ml], axis=1)

        prenorm = (norm_mix[i], mods, 0, 1, n_ctx)
        if kind == 0:
            hd = GQA_HEAD_DIM
            scale = hd ** -0.5 * LOG2E
            nq, nk = GQA_HEADS * hd, GQA_KV_HEADS * hd
            groups = [(g * hd, hd) for g in range(GQA_HEADS + GQA_KV_HEADS)]
            gain = jnp.concatenate([jnp.tile(gqa_q_gain[j] * scale, GQA_HEADS),
                                    jnp.tile(gqa_k_gain[j], GQA_KV_HEADS), ones(nk)])
            qkv = proj(xa, 0, gqa_wqkv[j].astype(BF16), groups, gain, _rope_tables(seq, n_ctx, hd),
                       range((nq + nk) // LANES), hd // 2, "gqa_qkv", prenorm)
            o = gqa_attention(qkv, n_ctx)
            w_o = gqa_wo[j]
        elif kind == 1:
            scale = (MLA_NOPE + MLA_ROPE) ** -0.5 * LOG2E
            hh = MLA_HEADS
            wd = mla_wdown[j]
            pad = jnp.zeros((d, LANES - MLA_ROPE), F32)
            wd2 = jnp.concatenate([wd[:, :MLA_Q_LORA], wd[:, MLA_Q_LORA + MLA_KV_LORA:], pad,
                                   wd[:, MLA_Q_LORA:MLA_Q_LORA + MLA_KV_LORA]], axis=1)
            kpe0 = MLA_Q_LORA
            ckv0 = MLA_Q_LORA + LANES
            groups = [(0, MLA_Q_LORA), (kpe0, MLA_ROPE), (ckv0, MLA_KV_LORA)]
            gain = jnp.concatenate([mla_qa_gain[j], mla_k_gain[j][MLA_NOPE:], zeros(LANES - MLA_ROPE),
                                    mla_kva_gain[j]])
            rope = _rope_tables(seq, n_ctx, MLA_ROPE)
            dn = proj(xa, 0, wd2.astype(BF16), groups, gain, rope, [kpe0 // LANES], MLA_ROPE // 2, "mla_down",
                      prenorm)
            wq = mla_wuq[j].reshape(MLA_Q_LORA, hh, MLA_NOPE + MLA_ROPE)
            wq2 = jnp.pad(wq, ((0, 0), (0, 0), (0, LANES - MLA_ROPE))).reshape(MLA_Q_LORA, hh * 2 * LANES)
            groups = ([(g * 2 * LANES, MLA_NOPE) for g in range(hh)]
                      + [(g * 2 * LANES + MLA_NOPE, MLA_ROPE) for g in range(hh)])
            qg = jnp.concatenate([mla_q_gain[j] * scale, zeros(LANES - MLA_ROPE)])
            gain = jnp.tile(qg, hh)
            q = proj(dn, 0, wq2.astype(BF16), groups, gain, rope, range(1, 2 * hh, 2), MLA_ROPE // 2, "mla_q")
            groups = [(g * (MLA_NOPE + MLA_V), MLA_NOPE) for g in range(hh)]
            gain = jnp.tile(jnp.concatenate([mla_k_gain[j][:MLA_NOPE], ones(MLA_V)]), hh)
            kv = proj(dn, ckv0 // MLA_KV_LORA, mla_wukv[j].astype(BF16), groups, gain, rope, [], MLA_ROPE // 2, "mla_kv")
            o = mla_attention(q, kv, dn, kpe0 // LANES, n_ctx)
            w_o = mla_wo[j]
        elif kind == 2:
            hd = WIN_HEAD_DIM
            scale = hd ** -0.5 * LOG2E
            n_pairs = WIN_HEADS // WIN_KV_HEADS
            nq = WIN_HEADS * hd
            perm = np.array([(kv * n_pairs + g) * hd + t for g in range(n_pairs)
                             for kv in range(WIN_KV_HEADS) for t in range(hd)])
            wq = win_wqkv[j]
            w2_ = jnp.concatenate([wq[:, perm], wq[:, nq:]], axis=1)
            groups = [(g * hd, hd) for g in range(WIN_HEADS + WIN_KV_HEADS)]
            gain = jnp.concatenate([jnp.tile(win_q_gain[j] * scale, WIN_HEADS),
                                    jnp.tile(win_k_gain[j], WIN_KV_HEADS), ones(WIN_KV_HEADS * hd)])
            qkv = proj(xa, 0, w2_.astype(BF16), groups, gain, _rope_tables(seq, n_ctx, hd),
                       range(n_pairs + 1), hd // 2, "win_qkv", prenorm)
            o = win_attention(qkv, win_sink[j] * LOG2E, n_ctx)
            w_o = win_wo[j][perm, :]
        else:
            hd = DIFF_HEAD_DIM
            scale = hd ** -0.5 * LOG2E
            lam_init = 0.8 - 0.6 * math.exp(-0.3 * i)
            nqk = 2 * DIFF_HEADS
            groups = [(g * hd, hd) for g in range(2 * nqk)]
            gain = jnp.concatenate([jnp.tile(diff_q_gain[j] * scale, nqk), jnp.tile(diff_k_gain[j], nqk),
                                    ones(DIFF_HEADS * 2 * hd)])
            qkv = proj(xa, 0, diff_wqkv[j].astype(BF16), groups, gain, _rope_tables(seq, n_ctx, hd),
                       range(2 * DIFF_HEADS), hd // 2, "diff_qkv", prenorm)
            o = diff_attention(qkv, diff_lambda[j], diff_subln[j], lam_init, n_ctx)
            w_o = diff_wo[j]
        xa = out_proj_residual(o, w_o.astype(BF16), xa, mods, 2, n_ctx)

        jj = i // 2
        if i % 2 == 0:
            xa = ffn_residual(norm_ffn[i], ffn_w13[jj].astype(BF16), ffn_w2[jj].astype(BF16), xa, mods, 3, 4, 5,
                              n_ctx)
        else:
            h, route = norm_mod_router(xa, norm_ffn[i], mods, 3, 4, n_ctx, moe_router[jj])
            xa = moe_residual(h, route, moe_w13[jj].astype(BF16), moe_w2[jj].astype(BF16), xa, mods, 5,
                              n_ctx, need_ctx)
    return xa[:, n_ctx:]
```

```python
import functools
import math

import jax
import jax.numpy as jnp
import numpy as np
from jax import lax
from jax.experimental import pallas as pl
from jax.experimental.pallas import tpu as pltpu

GRID_W = 64
ROPE_THETA = 10000.0
EPS = 1e-6
NEG_INF = -1e30
LANES = 128
MXU_WIDTH = 256
ROW_TILE = 256
FFN_ROW_TILE = 768
FF_CHUNK = 512
VMEM_LIMIT = 56 * 1024 * 1024

GQA_HEADS, GQA_KV_HEADS, GQA_HEAD_DIM = 8, 2, 128
MLA_HEADS, MLA_Q_LORA, MLA_KV_LORA, MLA_NOPE, MLA_ROPE, MLA_V = 8, 384, 256, 128, 64, 128
WIN_HEADS, WIN_KV_HEADS, WIN_HEAD_DIM, WINDOW = 16, 2, 64, 128
DIFF_HEADS, DIFF_HEAD_DIM = 8, 64
N_EXPERTS = 8

F32 = jnp.float32
BF16 = jnp.bfloat16


def _cparams(*sem):
    return pltpu.CompilerParams(dimension_semantics=sem, vmem_limit_bytes=VMEM_LIMIT)


def _dot(a, b):
    return jnp.dot(a, b, preferred_element_type=F32)


def _dot_nt(a, b):
    return lax.dot_general(a, b, (((1,), (1,)), ((), ())), preferred_element_type=F32)


def _split_bf16(v):
    hi = v.astype(BF16)
    lo = (v - hi.astype(F32)).astype(BF16)
    return hi, lo


def _adaln_kernel(c_ref, w_ref, b_ref, o_ref):
    c = c_ref[...]
    sc = c * (1.0 / (1.0 + jnp.exp(-c)))
    o_ref[0] = jnp.dot(sc, w_ref[0], preferred_element_type=F32,
                       precision=lax.Precision.HIGHEST) + b_ref[0]


def adaln(cond, ada_w, ada_b):
    depth, d, n = ada_w.shape
    rows = cond.shape[0]
    bn = 1536
    return pl.pallas_call(
        _adaln_kernel,
        grid=(depth, n // bn),
        in_specs=[
            pl.BlockSpec((rows, d), lambda l, j: (0, 0)),
            pl.BlockSpec((1, d, bn), lambda l, j: (l, 0, j)),
            pl.BlockSpec((1, 1, bn), lambda l, j: (l, 0, j)),
        ],
        out_specs=pl.BlockSpec((1, rows, bn), lambda l, j: (l, 0, j)),
        out_shape=jax.ShapeDtypeStruct((depth, rows, n), F32),
        compiler_params=_cparams("parallel", "parallel"),
        name="adaln",
    )(cond, ada_w, ada_b.reshape(depth, 1, n))


def _mod_spec(k, n_ctx_tiles, d):
    return pl.BlockSpec((None, None, None, 1, d),
                        lambda b, i, *_: (b, jnp.minimum(i // n_ctx_tiles, 1), k, 0, 0))


def _norm_mod(x, g, shift, scale):
    ms = jnp.mean(x * x, axis=-1, keepdims=True)
    return x * lax.rsqrt(ms + EPS) * g * (1.0 + scale) + shift


def _norm_mod_router_kernel(x_ref, g_ref, sh_ref, sc_ref, r_ref, h_ref, cw_ref):
    d = x_ref.shape[2]
    h = _norm_mod(x_ref[0], g_ref[...], sh_ref[...], sc_ref[...])
    h_ref[0, :, :d] = h.astype(BF16)
    logits = jnp.dot(h, r_ref[...], preferred_element_type=F32, precision=lax.Precision.HIGHEST)
    lane = lax.broadcasted_iota(jnp.int32, logits.shape, 1)
    logits = jnp.where(lane < N_EXPERTS, logits, -jnp.inf)
    m1 = jnp.max(logits, axis=-1, keepdims=True)
    i1 = jnp.min(jnp.where(logits == m1, lane, LANES), axis=-1, keepdims=True)
    rest = jnp.where(lane == i1, -jnp.inf, logits)
    m2 = jnp.max(rest, axis=-1, keepdims=True)
    i2 = jnp.min(jnp.where(rest == m2, lane, LANES), axis=-1, keepdims=True)
    e2 = jnp.exp(m2 - m1)
    den = 1.0 + e2
    g1, g2 = 1.0 / den, e2 / den
    cw_ref[0] = jnp.where(lane == 0, i1.astype(F32), jnp.where(lane == 1, i2.astype(F32),
                          jnp.where(lane == 2, g1, jnp.where(lane == 3, g2, 0.0))))
    g1_hi, g2_hi = g1.astype(BF16).astype(F32), g2.astype(BF16).astype(F32)
    ext = jnp.where(lane == ROUTE_G1, g1_hi, jnp.where(lane == ROUTE_G1 + 1, g1 - g1_hi,
          jnp.where(lane == ROUTE_G2, g2_hi, jnp.where(lane == ROUTE_G2 + 1, g2 - g2_hi,
          jnp.where(lane == ROUTE_E1, i1.astype(F32), jnp.where(lane == ROUTE_E2, i2.astype(F32), 0.0))))))
    h_ref[0, :, d:] = ext.astype(BF16)


ROUTE_G1, ROUTE_G2, ROUTE_E1, ROUTE_E2 = 0, 2, 4, 5


def norm_mod_router(xa, gain, mods, k_shift, k_scale, n_ctx, router):
    b, ta, d = xa.shape
    bm = ROW_TILE
    nct = n_ctx // bm
    rpad = jnp.zeros((d, LANES), F32).at[:, :N_EXPERTS].set(router)
    return pl.pallas_call(
        _norm_mod_router_kernel, grid=(b, ta // bm),
        in_specs=[
            pl.BlockSpec((1, bm, d), lambda b_, i: (b_, i, 0)),
            pl.BlockSpec((1, d), lambda b_, i: (0, 0)),
            _mod_spec(k_shift, nct, d),
            _mod_spec(k_scale, nct, d),
            pl.BlockSpec((d, LANES), lambda b_, i: (0, 0)),
        ],
        out_specs=[pl.BlockSpec((1, bm, d + LANES), lambda b_, i: (b_, i, 0)),
                   pl.BlockSpec((1, bm, LANES), lambda b_, i: (b_, i, 0))],
        out_shape=[jax.ShapeDtypeStruct((b, ta, d + LANES), BF16), jax.ShapeDtypeStruct((b, ta, LANES), F32)],
        compiler_params=_cparams("parallel", "parallel"), name="norm_mod_router",
    )(xa, gain.reshape(1, d), mods, mods, rpad)


def _proj_kernel(*refs, rope_blocks, rope_half, prenorm, norm_blocks):
    if prenorm:
        x_ref, g_ref, sh_ref, sc_ref = refs[:4]
        h = _norm_mod(x_ref[0], g_ref[...], sh_ref[...], sc_ref[...]).astype(BF16)
        refs = refs[4:]
    else:
        h = refs[0][0]
        refs = refs[1:]
    y = _dot(h, refs[0][...])
    n = y.shape[1]
    s_hi, s_lo = _split_bf16(y * y)
    if norm_blocks is None:
        _, e_ref, et_ref, igs_ref, u_ref, gain_ref, c_ref, s_ref, o_ref = refs
        ss = _dot(s_hi, e_ref[...]) + _dot(s_lo, e_ref[...])
        inv = lax.rsqrt(ss * igs_ref[...] + EPS)
        i_hi, i_lo = _split_bf16(inv)
        fac = _dot(i_hi, et_ref[...]) + _dot(i_lo, et_ref[...]) + u_ref[...]
        z = y * (fac * gain_ref[...])
    else:
        _, j_ref, igs_ref, u_ref, gain_ref, c_ref, s_ref, o_ref = refs
        zs = []
        for blk in range(n // MXU_WIDTH):
            sl = slice(blk * MXU_WIDTH, (blk + 1) * MXU_WIDTH)
            if blk in norm_blocks:
                ss = _dot(s_hi[:, sl], j_ref[blk]) + _dot(s_lo[:, sl], j_ref[blk])
                u = u_ref[:, sl]
                fac = lax.rsqrt(ss * igs_ref[:, sl] + EPS) * (1.0 - u) + u
                zs.append(y[:, sl] * (fac * gain_ref[:, sl]))
            else:
                zs.append(y[:, sl] * gain_ref[:, sl])
        z = jnp.concatenate(zs, axis=1)
    cos = c_ref[...]
    sin = s_ref[...]
    if rope_half == LANES // 2:
        first = None
    else:
        lane = lax.broadcasted_iota(jnp.int32, cos.shape, 1)
        first = (lane % (2 * rope_half)) < rope_half
    for blk in range(n // LANES):
        zb = z[:, blk * LANES:(blk + 1) * LANES]
        if blk in rope_blocks:
            if first is None:
                rot = pltpu.roll(zb, LANES // 2, 1)
            else:
                rot = jnp.where(first, pltpu.roll(zb, LANES - rope_half, 1), pltpu.roll(zb, rope_half, 1))
            zb = zb * cos + rot * sin
        o_ref[0, :, blk * LANES:(blk + 1) * LANES] = zb.astype(BF16)


def proj(h, kin_block, w, groups, gain, rope_tabs, rope_blocks, rope_half, name, prenorm=None):
    b, ta, _ = h.shape
    k, n = w.shape
    bm = ROW_TILE
    mw = MXU_WIDTH
    u = np.ones((1, n), np.float32)
    for start, size in groups:
        u[0, start:start + size] = 0.0
    local = n % mw == 0 and all(start // mw == (start + size - 1) // mw for start, size in groups)
    if local:
        jm = np.zeros((n // mw, mw, mw), np.float32)
        igs = np.zeros((1, n), np.float32)
        for start, size in groups:
            blk, o = divmod(start, mw)
            jm[blk, o:o + size, o:o + size] = 1.0
            igs[0, start:start + size] = 1.0 / size
        norm_blocks = frozenset(start // mw for start, _ in groups)
        norm_args = [jnp.asarray(jm, BF16), jnp.asarray(igs)]
        norm_specs = [pl.BlockSpec((n // mw, mw, mw), lambda b_, i: (0, 0, 0)),
                      pl.BlockSpec((1, n), lambda b_, i: (0, 0))]
    else:
        e = np.zeros((n, LANES), np.float32)
        igs = np.zeros((1, LANES), np.float32)
        for gi, (start, size) in enumerate(groups):
            e[start:start + size, gi] = 1.0
            igs[0, gi] = 1.0 / size
        norm_blocks = None
        norm_args = [jnp.asarray(e, BF16), jnp.asarray(e.T, BF16), jnp.asarray(igs)]
        norm_specs = [pl.BlockSpec((n, LANES), lambda b_, i: (0, 0)),
                      pl.BlockSpec((LANES, n), lambda b_, i: (0, 0)),
                      pl.BlockSpec((1, LANES), lambda b_, i: (0, 0))]
    cos, sin = rope_tabs
    kern = functools.partial(_proj_kernel, rope_blocks=frozenset(rope_blocks), rope_half=rope_half,
                             prenorm=prenorm is not None, norm_blocks=norm_blocks)
    lead_specs = [pl.BlockSpec((1, bm, k), lambda b_, i: (b_, i, kin_block))]
    lead_args = [h]
    if prenorm is not None:
        ngain, mods, k_shift, k_scale, n_ctx = prenorm
        nct = n_ctx // bm
        lead_specs += [pl.BlockSpec((1, k), lambda b_, i: (0, 0)), _mod_spec(k_shift, nct, k),
                       _mod_spec(k_scale, nct, k)]
        lead_args += [ngain.reshape(1, k), mods, mods]
    return pl.pallas_call(
        kern,
        grid=(b, ta // bm),
        in_specs=lead_specs + [pl.BlockSpec((k, n), lambda b_, i: (0, 0))] + norm_specs + [
            pl.BlockSpec((1, n), lambda b_, i: (0, 0)),
            pl.BlockSpec((1, n), lambda b_, i: (0, 0)),
            pl.BlockSpec((bm, LANES), lambda b_, i: (i, 0)),
            pl.BlockSpec((bm, LANES), lambda b_, i: (i, 0)),
        ],
        out_specs=pl.BlockSpec((1, bm, n), lambda b_, i: (b_, i, 0)),
        out_shape=jax.ShapeDtypeStruct((b, ta, n), BF16),
        compiler_params=_cparams("parallel", "parallel"),
        name=name,
    )(*lead_args, w, *norm_args, jnp.asarray(u), gain.reshape(1, n).astype(F32), cos, sin)


LOG2E = math.log2(math.e)


def _softmax_pv(s, v):
    m = jnp.max(s, axis=-1, keepdims=True)
    p = jnp.exp2(s - m)
    l = jnp.sum(p, axis=-1, keepdims=True)
    return _dot(p.astype(BF16), v) / l


def _gqa_attn_kernel(q_ref, k_ref, v_ref, o_ref, *, n_ctx, group, hd):
    i = pl.program_id(2)

    def attend(nk):
        k = k_ref[0, :nk, :]
        v = v_ref[0, :nk, :]
        for g in range(group):
            q = q_ref[0, :, g * hd:(g + 1) * hd]
            o = _softmax_pv(_dot_nt(q, k), v)
            o_ref[0, :, g * hd:(g + 1) * hd] = o.astype(BF16)

    nct = n_ctx // ROW_TILE

    @pl.when(i < nct)
    def _():
        attend(n_ctx)

    @pl.when(i >= nct)
    def _():
        attend(k_ref.shape[1])


def gqa_attention(qkv, n_ctx):
    b, ta, _ = qkv.shape
    hd, group, kvh = GQA_HEAD_DIM, GQA_HEADS // GQA_KV_HEADS, GQA_KV_HEADS
    bq = ROW_TILE
    qw = group * hd
    kb = GQA_HEADS
    vb = GQA_HEADS + kvh
    kern = functools.partial(_gqa_attn_kernel, n_ctx=n_ctx, group=group, hd=hd)
    return pl.pallas_call(
        kern,
        grid=(b, kvh, ta // bq),
        in_specs=[
            pl.BlockSpec((1, bq, qw), lambda b_, h, i: (b_, i, h)),
            pl.BlockSpec((1, ta, hd), lambda b_, h, i: (b_, 0, kb + h)),
            pl.BlockSpec((1, ta, hd), lambda b_, h, i: (b_, 0, vb + h)),
        ],
        out_specs=pl.BlockSpec((1, bq, qw), lambda b_, h, i: (b_, i, h)),
        out_shape=jax.ShapeDtypeStruct((b, ta, GQA_HEADS * hd), BF16),
        compiler_params=_cparams("parallel", "parallel", "arbitrary"),
        name="gqa_attn",
    )(qkv, qkv, qkv)


def _mla_attn_kernel(q_ref, kn_ref, kp_ref, v_ref, o_ref, kcat_ref, *, n_ctx):
    i = pl.program_id(2)

    @pl.when(i == 0)
    def _():
        kcat_ref[:, :LANES] = kn_ref[0]
        kcat_ref[:, LANES:] = kp_ref[0]

    def attend(nk):
        s = _dot_nt(q_ref[0], kcat_ref[:nk, :])
        o_ref[0] = _softmax_pv(s, v_ref[0, :nk, :]).astype(BF16)

    nct = n_ctx // ROW_TILE

    @pl.when(i < nct)
    def _():
        attend(n_ctx)

    @pl.when(i >= nct)
    def _():
        attend(kn_ref.shape[1])


def mla_attention(q, kv, dn, kpe_block, n_ctx):
    b, ta, _ = q.shape
    bq = ROW_TILE
    hh = MLA_HEADS
    kern = functools.partial(_mla_attn_kernel, n_ctx=n_ctx)
    return pl.pallas_call(
        kern,
        grid=(b, hh, ta // bq),
        in_specs=[
            pl.BlockSpec((1, bq, 2 * LANES), lambda b_, h, i: (b_, i, h)),
            pl.BlockSpec((1, ta, LANES), lambda b_, h, i: (b_, 0, 2 * h)),
            pl.BlockSpec((1, ta, LANES), lambda b_, h, i: (b_, 0, kpe_block)),
            pl.BlockSpec((1, ta, LANES), lambda b_, h, i: (b_, 0, 2 * h + 1)),
        ],
        out_specs=pl.BlockSpec((1, bq, LANES), lambda b_, h, i: (b_, i, h)),
        out_shape=jax.ShapeDtypeStruct((b, ta, hh * MLA_V), BF16),
        scratch_shapes=[pltpu.VMEM((ta, 2 * LANES), BF16)],
        compiler_params=_cparams("parallel", "parallel", "arbitrary"),
        name="mla_attn",
    )(q, kv, dn, kv)


def _half_masks(shape):
    lane = lax.broadcasted_iota(jnp.int32, shape, 1)
    return lane < (LANES // 2)


def _diff_attn_kernel(q_ref, k_ref, v_ref, lam_ref, sub_ref, o_ref, *, n_ctx, lam_init):
    i = pl.program_id(2)
    lp = lam_ref[...]
    lam = (jnp.exp(jnp.sum(lp[0:1] * lp[1:2], axis=-1, keepdims=True))
           - jnp.exp(jnp.sum(lp[2:3] * lp[3:4], axis=-1, keepdims=True)) + lam_init)

    def attend(nk):
        k = k_ref[0, :nk, :]
        v = v_ref[0, :nk, :]

        def probs(s, weight):
            m = jnp.max(s, axis=-1, keepdims=True)
            e = jnp.exp2(s - m)
            return e * (weight / jnp.sum(e, axis=-1, keepdims=True))

        q = q_ref[0]
        lo = _half_masks(q.shape)
        zero = jnp.zeros_like(q)
        p = probs(_dot_nt(jnp.where(lo, q, zero), k), 1.0) - probs(_dot_nt(jnp.where(lo, zero, q), k), lam)
        o = _dot(p.astype(BF16), v)
        ms = jnp.mean(o * o, axis=-1, keepdims=True)
        o_ref[0] = (o * lax.rsqrt(ms + EPS) * sub_ref[...] * (1.0 - lam_init)).astype(BF16)

    nct = n_ctx // ROW_TILE

    @pl.when(i < nct)
    def _():
        attend(n_ctx)

    @pl.when(i >= nct)
    def _():
        attend(k_ref.shape[1])


def diff_attention(qkv, lam_p, subln, lam_init, n_ctx):
    b, ta, _ = qkv.shape
    bq = ROW_TILE
    hh = DIFF_HEADS
    kern = functools.partial(_diff_attn_kernel, n_ctx=n_ctx, lam_init=lam_init)
    return pl.pallas_call(
        kern,
        grid=(b, hh, ta // bq),
        in_specs=[
            pl.BlockSpec((1, bq, LANES), lambda b_, h, i: (b_, i, h)),
            pl.BlockSpec((1, ta, LANES), lambda b_, h, i: (b_, 0, hh + h)),
            pl.BlockSpec((1, ta, LANES), lambda b_, h, i: (b_, 0, 2 * hh + h)),
            pl.BlockSpec((4, DIFF_HEAD_DIM), lambda b_, h, i: (0, 0)),
            pl.BlockSpec((1, LANES), lambda b_, h, i: (0, 0)),
        ],
        out_specs=pl.BlockSpec((1, bq, LANES), lambda b_, h, i: (b_, i, h)),
        out_shape=jax.ShapeDtypeStruct((b, ta, hh * 2 * DIFF_HEAD_DIM), BF16),
        compiler_params=_cparams("parallel", "parallel", "arbitrary"),
        name="diff_attn",
    )(qkv, qkv, qkv, lam_p.astype(F32), subln.reshape(1, LANES).astype(F32))


def _win_attn_kernel(sink_ref, q_ref, k_ref, v_ref, o_ref, *, n_ctx, seq, n_pairs):
    i = pl.program_id(1)
    bq = q_ref.shape[1]
    band = bq + 2 * WINDOW
    nct = n_ctx // bq

    def head_out(qm, kv_idx, pair, blocks):
        sk = sink_ref[kv_idx * n_pairs + pair]
        ss = []
        m = None
        for k, _, mask in blocks:
            s = _dot_nt(qm, k)
            if mask is not None:
                s = jnp.where(mask, s, NEG_INF)
            ss.append(s)
            bm_ = jnp.max(s, axis=-1, keepdims=True)
            m = bm_ if m is None else jnp.maximum(m, bm_)
        m = jnp.maximum(m, sk)
        l = jnp.exp2(sk - m)
        o = None
        for s, (_, v, _) in zip(ss, blocks):
            p = jnp.exp2(s - m)
            l = l + jnp.sum(p, axis=-1, keepdims=True)
            pv = _dot(p.astype(BF16), v)
            o = pv if o is None else o + pv
        return o / l

    def run(blocks):
        for pair in range(n_pairs):
            q = q_ref[0, :, pair * LANES:(pair + 1) * LANES]
            lo = _half_masks(q.shape)
            zero = jnp.zeros_like(q)
            o0 = head_out(jnp.where(lo, q, zero), 0, pair, blocks)
            o1 = head_out(jnp.where(lo, zero, q), 1, pair, blocks)
            o_ref[0, :, pair * LANES:(pair + 1) * LANES] = jnp.where(lo, o0, o1).astype(BF16)

    @pl.when(i < nct)
    def _():
        run([(k_ref[0, :n_ctx, :], v_ref[0, :n_ctx, :], None)])

    @pl.when(i >= nct)
    def _():
        q0 = (i - nct) * bq
        start = jnp.clip(q0 - WINDOW, 0, seq - band)
        start = pl.multiple_of(start, WINDOW)
        qpos = q0 + lax.broadcasted_iota(jnp.int32, (bq, band), 0)
        kpos = start + lax.broadcasted_iota(jnp.int32, (bq, band), 1)
        mask = jnp.abs(qpos - kpos) <= WINDOW
        kb = k_ref[0, pl.ds(n_ctx + start, band), :]
        vb = v_ref[0, pl.ds(n_ctx + start, band), :]
        run([(k_ref[0, :n_ctx, :], v_ref[0, :n_ctx, :], None), (kb, vb, mask)])


def win_attention(qkv, sink, n_ctx):
    b, ta, _ = qkv.shape
    bq = ROW_TILE
    n_pairs = WIN_HEADS // WIN_KV_HEADS
    qw = n_pairs * LANES
    kern = functools.partial(_win_attn_kernel, n_ctx=n_ctx, seq=ta - n_ctx, n_pairs=n_pairs)
    return pl.pallas_call(
        kern,
        grid_spec=pltpu.PrefetchScalarGridSpec(
            num_scalar_prefetch=1,
            grid=(b, ta // bq),
            in_specs=[
                pl.BlockSpec((1, bq, qw), lambda b_, i, s: (b_, i, 0)),
                pl.BlockSpec((1, ta, LANES), lambda b_, i, s: (b_, 0, n_pairs)),
                pl.BlockSpec((1, ta, LANES), lambda b_, i, s: (b_, 0, n_pairs + 1)),
            ],
            out_specs=pl.BlockSpec((1, bq, qw), lambda b_, i, s: (b_, i, 0)),
        ),
        out_shape=jax.ShapeDtypeStruct((b, ta, qw), BF16),
        compiler_params=_cparams("parallel", "arbitrary"),
        name="win_attn",
    )(sink.astype(F32), qkv, qkv, qkv)


def _out_proj_kernel(o_ref, w_ref, x_ref, g_ref, y_ref):
    y_ref[0] = x_ref[0] + g_ref[...] * _dot(o_ref[0], w_ref[...])


def out_proj_residual(o, w, xa, mods, k_gate, n_ctx):
    b, ta, d = xa.shape
    kdim = o.shape[2]
    bm = ROW_TILE
    return pl.pallas_call(
        _out_proj_kernel,
        grid=(b, ta // bm),
        in_specs=[
            pl.BlockSpec((1, bm, kdim), lambda b_, i: (b_, i, 0)),
            pl.BlockSpec((kdim, d), lambda b_, i: (0, 0)),
            pl.BlockSpec((1, bm, d), lambda b_, i: (b_, i, 0)),
            _mod_spec(k_gate, n_ctx // bm, d),
        ],
        out_specs=pl.BlockSpec((1, bm, d), lambda b_, i: (b_, i, 0)),
        out_shape=jax.ShapeDtypeStruct(xa.shape, F32),
        input_output_aliases={2: 0},
        compiler_params=_cparams("parallel", "parallel"),
        name="out_proj",
    )(o, w, xa, mods)


def _ffn_kernel(ng_ref, wg_ref, wu_ref, w2_ref, x_ref, mc_ref, ml_ref, y_ref, acc_ref, h_ref, *,
                n_ctx, k_shift, k_scale, k_gate):
    i = pl.program_id(1)
    f = pl.program_id(2)
    bm = acc_ref.shape[0]

    def row_mod(k):
        row = i * bm + lax.broadcasted_iota(jnp.int32, (bm, 1), 0)
        return jnp.where(row < n_ctx, mc_ref[k], ml_ref[k])

    @pl.when(f == 0)
    def _():
        acc_ref[...] = jnp.zeros_like(acc_ref)
        h_ref[...] = _norm_mod(x_ref[0], ng_ref[...], row_mod(k_shift), row_mod(k_scale)).astype(BF16)

    h = h_ref[...]
    g = _dot(h, wg_ref[...])
    u = _dot(h, wu_ref[...])
    act = (g * (1.0 / (1.0 + jnp.exp(-g))) * u).astype(BF16)
    acc_ref[...] += _dot(act, w2_ref[...])

    @pl.when(f == pl.num_programs(2) - 1)
    def _():
        y_ref[0] = x_ref[0] + row_mod(k_gate) * acc_ref[...]


def ffn_residual(norm_gain, w13, w2, xa, mods, k_shift, k_scale, k_gate, n_ctx):
    b, ta, d = xa.shape
    ff = w2.shape[0]
    bm, fk = FFN_ROW_TILE, FF_CHUNK
    nf = ff // fk
    kern = functools.partial(_ffn_kernel, n_ctx=n_ctx, k_shift=k_shift, k_scale=k_scale, k_gate=k_gate)
    in_specs = [
        pl.BlockSpec((1, d), lambda b_, i, f: (0, 0)),
        pl.BlockSpec((d, fk), lambda b_, i, f: (0, f)),
        pl.BlockSpec((d, fk), lambda b_, i, f: (0, nf + f)),
        pl.BlockSpec((fk, d), lambda b_, i, f: (f, 0)),
        pl.BlockSpec((1, bm, d), lambda b_, i, f: (b_, i, 0)),
        pl.BlockSpec((None, None, 6, 1, d), lambda b_, i, f: (b_, 0, 0, 0, 0)),
        pl.BlockSpec((None, None, 6, 1, d), lambda b_, i, f: (b_, 1, 0, 0, 0)),
    ]
    return pl.pallas_call(
        kern,
        grid=(b, ta // bm, nf),
        in_specs=in_specs,
        out_specs=pl.BlockSpec((1, bm, d), lambda b_, i, f: (b_, i, 0)),
        out_shape=jax.ShapeDtypeStruct(xa.shape, F32),
        scratch_shapes=[pltpu.VMEM((bm, d), F32), pltpu.VMEM((bm, d), BF16)],
        input_output_aliases={4: 0},
        compiler_params=_cparams("parallel", "parallel", "arbitrary"),
        name="ffn",
    )(norm_gain.reshape(1, d), w13, w13, w2, xa, mods, mods)


MOE_SLOT_TILE = 512
MOE_GATHER_CHUNK = 512
MOE_GATHER_FANIN = 6
MOE_COMBINE_CHUNK = 768
_VALID, _FIRST, _LAST = 1, 2, 4


def _one_hot_bf16(cond):
    return jnp.where(cond, 1.0, 0.0).astype(BF16)


def _moe_gather_kernel(tile_ref, base_ref, hi_ref, flag_ref, *refs, fan):
    p1_refs, p2_refs, h_refs, o_ref = refs[:fan], refs[fan:2 * fan], refs[2 * fan:3 * fan], refs[3 * fan]
    k = pl.program_id(0)
    fl = flag_ref[k]

    def picked():
        bm, ch = o_ref.shape[0], h_refs[0].shape[0]
        slot = tile_ref[k] * bm + lax.broadcasted_iota(jnp.int32, (bm, ch), 0)
        acc = None
        for j in range(fan):
            slot_j = jnp.where(base_ref[k] + j <= hi_ref[k], slot, -2)
            part = _dot(_one_hot_bf16((p1_refs[j][...] == slot_j) | (p2_refs[j][...] == slot_j)), h_refs[j][...])
            acc = part if acc is None else acc + part
        return acc.astype(BF16)

    @pl.when((fl & _FIRST) != 0)
    def _():
        o_ref[...] = picked()

    @pl.when(((fl & _VALID) != 0) & ((fl & _FIRST) == 0))
    def _():
        o_ref[...] += picked()


def _moe_ffn_kernel(te_ref, nu_ref, xs_ref, wg_ref, wu_ref, w2_ref, y_ref, acc_ref):
    i = pl.program_id(0)
    f = pl.program_id(1)
    d = acc_ref.shape[1]

    @pl.when(i < nu_ref[0])
    def _():
        @pl.when(f == 0)
        def _():
            acc_ref[...] = jnp.zeros_like(acc_ref)

        xs = xs_ref[:, :d]
        g = _dot(xs, wg_ref[...])
        u = _dot(xs, wu_ref[...])
        act = (g * (1.0 / (1.0 + jnp.exp(-g))) * u).astype(BF16)
        acc_ref[...] += _dot(act, w2_ref[...])

        @pl.when(f == pl.num_programs(1) - 1)
        def _():
            rt = xs_ref[:, d:].astype(F32)
            first = rt[:, ROUTE_E1:ROUTE_E1 + 1] == te_ref[i].astype(F32)
            gate = jnp.where(first, rt[:, ROUTE_G1:ROUTE_G1 + 1] + rt[:, ROUTE_G1 + 1:ROUTE_G1 + 2],
                             rt[:, ROUTE_G2:ROUTE_G2 + 1] + rt[:, ROUTE_G2 + 1:ROUTE_G2 + 2])
            y_ref[...] = (acc_ref[...] * gate).astype(BF16)


def _moe_combine_kernel(chunk_ref, tile_ref, flag_ref, p1_ref, p2_ref, y_ref, x_ref, gc_ref, gl_ref,
                        o_ref, acc_ref, *, n_ctx, chunks_per_batch):
    k = pl.program_id(0)
    fl = flag_ref[k]
    ch, bm = acc_ref.shape[0], y_ref.shape[0]

    @pl.when((fl & _FIRST) != 0)
    def _():
        acc_ref[...] = jnp.zeros_like(acc_ref)

    @pl.when((fl & _VALID) != 0)
    def _():
        slot = tile_ref[k] * bm + lax.broadcasted_iota(jnp.int32, (ch, bm), 1)
        w = _one_hot_bf16((p1_ref[...] == slot) | (p2_ref[...] == slot))
        acc_ref[...] += _dot(w, y_ref[...])

    @pl.when((fl & _LAST) != 0)
    def _():
        row = (chunk_ref[k] % chunks_per_batch) * ch + lax.broadcasted_iota(jnp.int32, (ch, 1), 0)
        gate = jnp.where(row < n_ctx, gc_ref[...], gl_ref[...])
        o_ref[...] = x_ref[...] + gate * acc_ref[...]


def _expand_items(counts, n_items):
    off = jnp.cumsum(counts)
    total = off[-1]
    k = jnp.minimum(jnp.arange(n_items, dtype=jnp.int32), total - 1)
    grp = jnp.minimum(jnp.sum(off[None, :] <= k[:, None], axis=1).astype(jnp.int32), counts.shape[0] - 1)
    local = k - (off[grp] - counts[grp])
    valid = jnp.arange(n_items, dtype=jnp.int32) < total
    return grp, local, valid


def moe_residual(h, route, w13, w2, xa, mods, k_gate, n_ctx, ctx_active):
    b, ta, d = xa.shape
    t = b * ta
    ne, ff = w2.shape[0], w2.shape[1]
    bm, gch, cch, fk = MOE_SLOT_TILE, MOE_GATHER_CHUNK, MOE_COMBINE_CHUNK, FF_CHUNK
    nf = ff // fk
    nt = (2 * t) // bm + ne
    ns = nt * bm
    i32 = jnp.int32

    dx = h.shape[2]
    r = route.reshape(t, LANES)
    e1, e2 = r[:, 0].astype(i32), r[:, 1].astype(i32)
    tok = jnp.arange(t, dtype=i32)
    active = jnp.ones((t,), bool) if ctx_active else (tok % ta) >= n_ctx

    eid = jnp.arange(ne, dtype=i32)[:, None]
    member = ((e1[None] == eid) | (e2[None] == eid)) & active[None]
    csum = jnp.cumsum(member.astype(i32), axis=1)
    cap = (csum[:, -1] + bm - 1) // bm * bm
    end = jnp.cumsum(cap)
    start = end - cap
    n_used = (end[-1] // bm).astype(i32)

    def slot_of(e_sel):
        rank = jnp.take_along_axis(csum, e_sel[None], axis=0)[0] - 1
        return jnp.where(active, start[e_sel] + rank, -1)

    pos1, pos2 = slot_of(e1), slot_of(e2)
    tile_ids = jnp.arange(nt, dtype=i32)
    tile_expert = jnp.minimum(jnp.sum(end[None] <= (tile_ids * bm)[:, None], axis=1), ne - 1).astype(i32)

    r0 = tile_ids * bm - start[tile_expert]
    r1 = jnp.minimum(r0 + bm, csum[tile_expert, -1]) - 1
    ngc = t // gch
    through_g = csum[:, gch - 1::gch][tile_expert]
    c_lo = jnp.minimum(jnp.sum(through_g <= r0[:, None], axis=1), ngc - 1).astype(i32)
    c_hi = jnp.minimum(jnp.sum(through_g <= r1[:, None], axis=1), ngc - 1).astype(i32)
    n_ch = jnp.where(tile_ids < n_used, c_hi - c_lo + 1, 0)
    fan = MOE_GATHER_FANIN
    n_rd = (n_ch + fan - 1) // fan
    ni_g = (ne * ngc + nt) // fan + nt + 1
    g_tile, g_round, g_valid = _expand_items(n_rd, ni_g)
    g_base = c_lo[g_tile] + g_round * fan
    g_hi = c_hi[g_tile]
    g_flag = g_valid * (_VALID + _FIRST * (g_round == 0))

    def chunk_of(j):
        return lambda k, tl, bs, hi, fl: jnp.minimum(bs[k] + j, hi[k])

    row_specs = [pl.BlockSpec((None, 1, gch), lambda k, *s, c=chunk_of(j): (c(k, *s), 0, 0)) for j in range(fan)]
    h_specs = [pl.BlockSpec((gch, dx), lambda k, *s, c=chunk_of(j): (c(k, *s), 0)) for j in range(fan)]
    pos1_rows, pos2_rows, h_rows = pos1.reshape(ngc, 1, gch), pos2.reshape(ngc, 1, gch), h.reshape(t, dx)
    xs = pl.pallas_call(
        functools.partial(_moe_gather_kernel, fan=fan),
        grid_spec=pltpu.PrefetchScalarGridSpec(
            num_scalar_prefetch=4,
            grid=(ni_g,),
            in_specs=row_specs + row_specs + h_specs,
            out_specs=pl.BlockSpec((bm, dx), lambda k, tl, bs, hi, fl: (tl[k], 0)),
        ),
        out_shape=jax.ShapeDtypeStruct((ns, dx), BF16),
        compiler_params=_cparams("arbitrary"),
        name="moe_gather",
    )(g_tile, g_base, g_hi, g_flag.astype(i32), *([pos1_rows] * fan), *([pos2_rows] * fan), *([h_rows] * fan))

    def last_used(i, nu):
        return jnp.minimum(i, nu[0] - 1)

    def f_eff(i, f, nu):
        return jnp.where(i < nu[0], f, nf - 1)

    y = pl.pallas_call(
        _moe_ffn_kernel,
        grid_spec=pltpu.PrefetchScalarGridSpec(
            num_scalar_prefetch=2,
            grid=(nt, nf),
            in_specs=[
                pl.BlockSpec((bm, dx), lambda i, f, te, nu: (last_used(i, nu), 0)),
                pl.BlockSpec((None, d, fk), lambda i, f, te, nu: (te[i], 0, f_eff(i, f, nu))),
                pl.BlockSpec((None, d, fk), lambda i, f, te, nu: (te[i], 0, nf + f_eff(i, f, nu))),
                pl.BlockSpec((None, fk, d), lambda i, f, te, nu: (te[i], f_eff(i, f, nu), 0)),
            ],
            out_specs=pl.BlockSpec((bm, d), lambda i, f, te, nu: (last_used(i, nu), 0)),
            scratch_shapes=[pltpu.VMEM((bm, d), F32)],
        ),
        out_shape=jax.ShapeDtypeStruct((ns, d), BF16),
        compiler_params=_cparams("arbitrary", "arbitrary"),
        name="moe_ffn",
    )(tile_expert, n_used.reshape(1), xs, w13, w13, w2)

    nc = t // cch
    cs0 = jnp.concatenate([jnp.zeros((ne, 1), i32), csum], axis=1)[:, ::cch]
    before, through = cs0[:, :-1].T, cs0[:, 1:].T
    t_lo = (start[None] + before) // bm
    t_hi = (start[None] + through - 1) // bm
    n_tl = jnp.where(through > before, t_hi - t_lo + 1, 0).reshape(-1)
    ni_c = ne * nc + nt
    c_pair, c_local, c_valid = _expand_items(n_tl, ni_c)
    c_chunk = c_pair // ne
    c_tile = t_lo.reshape(-1)[c_pair] + c_local
    off = jnp.cumsum(n_tl)
    k_eff = jnp.minimum(jnp.arange(ni_c, dtype=i32), off[-1] - 1)
    chunk_first = (off - n_tl)[c_chunk * ne]
    chunk_last = off[c_chunk * ne + ne - 1] - 1
    c_flag = c_valid * (_VALID + _FIRST * (k_eff == chunk_first) + _LAST * (k_eff == chunk_last))

    cpb = ta // cch
    kern = functools.partial(_moe_combine_kernel, n_ctx=n_ctx, chunks_per_batch=cpb)
    out = pl.pallas_call(
        kern,
        grid_spec=pltpu.PrefetchScalarGridSpec(
            num_scalar_prefetch=3,
            grid=(ni_c,),
            in_specs=[
                pl.BlockSpec((cch, 1), lambda k, cn, tl, fl: (cn[k], 0)),
                pl.BlockSpec((cch, 1), lambda k, cn, tl, fl: (cn[k], 0)),
                pl.BlockSpec((bm, d), lambda k, cn, tl, fl: (tl[k], 0)),
                pl.BlockSpec((cch, d), lambda k, cn, tl, fl: (cn[k], 0)),
                pl.BlockSpec((None, None, None, 1, d), lambda k, cn, tl, fl: (cn[k] // cpb, 0, k_gate, 0, 0)),
                pl.BlockSpec((None, None, None, 1, d), lambda k, cn, tl, fl: (cn[k] // cpb, 1, k_gate, 0, 0)),
            ],
            out_specs=pl.BlockSpec((cch, d), lambda k, cn, tl, fl: (cn[k], 0)),
            scratch_shapes=[pltpu.VMEM((cch, d), F32)],
        ),
        out_shape=jax.ShapeDtypeStruct((t, d), F32),
        input_output_aliases={6: 0},
        compiler_params=_cparams("arbitrary"),
        name="moe_combine",
    )(c_chunk, c_tile, c_flag.astype(i32), pos1.reshape(t, 1), pos2.reshape(t, 1), y, xa.reshape(t, d), mods, mods)
    return out.reshape(b, ta, d)


def _rope_tables(seq, n_ctx, rot_dim):
    rows = seq // GRID_W
    quarter = rot_dim // 4
    inv_freq = ROPE_THETA ** (-jnp.arange(quarter, dtype=F32) / quarter)
    row = jnp.repeat(jnp.arange(rows, dtype=F32), GRID_W)
    col = jnp.tile(jnp.arange(GRID_W, dtype=F32), rows)
    ang = jnp.concatenate([row[:, None] * inv_freq, col[:, None] * inv_freq], axis=-1)
    cos, sin = jnp.cos(ang), jnp.sin(ang)
    reps = LANES // rot_dim
    c = jnp.tile(jnp.concatenate([cos, cos], axis=-1), (1, reps))
    s = jnp.tile(jnp.concatenate([-sin, sin], axis=-1), (1, reps))
    c = jnp.concatenate([jnp.ones((n_ctx, LANES), F32), c], axis=0)
    s = jnp.concatenate([jnp.zeros((n_ctx, LANES), F32), s], axis=0)
    return c, s


def kernel(x, c, ctx, c_ctx, ada_w, ada_b, norm_mix, norm_ffn, gqa_wqkv, gqa_q_gain, gqa_k_gain, gqa_wo, mla_wdown, mla_qa_gain, mla_kva_gain, mla_wuq, mla_wukv, mla_q_gain, mla_k_gain, mla_wo, win_wqkv, win_q_gain, win_k_gain, win_sink, win_wo, diff_wqkv, diff_q_gain, diff_k_gain, diff_lambda, diff_subln, diff_wo, ffn_w13, ffn_w2, moe_router, moe_w13, moe_w2):
    b, seq, d = x.shape
    n_ctx = ctx.shape[1]
    depth = ada_w.shape[0]
    ones = lambda n: jnp.ones((n,), F32)
    zeros = lambda n: jnp.zeros((n,), F32)

    crows = -(-(b + 1) // 8) * 8
    cond = jnp.zeros((crows, d), F32).at[:b].set(c).at[b].set(c_ctx)
    mods_all = adaln(cond, ada_w, ada_b)

    xa = jnp.concatenate([ctx, x], axis=1)

    for i in range(depth):
        need_ctx = i < depth - 1
        j = i // 4
        kind = i % 4
        ml = mods_all[i, :b].reshape(b, 6, 1, d)
        mc = jnp.broadcast_to(mods_all[i, b].reshape(1, 6, 1, d), (b, 6, 1, d))
        mods = jnp.stack([mc, ml], axis=1)

        prenorm = (norm_mix[i], mods, 0, 1, n_ctx)
        if kind == 0:
            hd = GQA_HEAD_DIM
            scale = hd ** -0.5 * LOG2E
            nq, nk = GQA_HEADS * hd, GQA_KV_HEADS * hd
            groups = [(g * hd, hd) for g in range(GQA_HEADS + GQA_KV_HEADS)]
            gain = jnp.concatenate([jnp.tile(gqa_q_gain[j] * scale, GQA_HEADS),
                                    jnp.tile(gqa_k_gain[j], GQA_KV_HEADS), ones(nk)])
            qkv = proj(xa, 0, gqa_wqkv[j].astype(BF16), groups, gain, _rope_tables(seq, n_ctx, hd),
                       range((nq + nk) // LANES), hd // 2, "gqa_qkv", prenorm)
            o = gqa_attention(qkv, n_ctx)
            w_o = gqa_wo[j]
        elif kind == 1:
            scale = (MLA_NOPE + MLA_ROPE) ** -0.5 * LOG2E
            hh = MLA_HEADS
            wd = mla_wdown[j]
            pad = jnp.zeros((d, LANES - MLA_ROPE), F32)
            wd2 = jnp.concatenate([wd[:, :MLA_Q_LORA], wd[:, MLA_Q_LORA + MLA_KV_LORA:], pad,
                                   wd[:, MLA_Q_LORA:MLA_Q_LORA + MLA_KV_LORA]], axis=1)
            kpe0 = MLA_Q_LORA
            ckv0 = MLA_Q_LORA + LANES
            groups = [(0, MLA_Q_LORA), (kpe0, MLA_ROPE), (ckv0, MLA_KV_LORA)]
            gain = jnp.concatenate([mla_qa_gain[j], mla_k_gain[j][MLA_NOPE:], zeros(LANES - MLA_ROPE),
                                    mla_kva_gain[j]])
            rope = _rope_tables(seq, n_ctx, MLA_ROPE)
            dn = proj(xa, 0, wd2.astype(BF16), groups, gain, rope, [kpe0 // LANES], MLA_ROPE // 2, "mla_down",
                      prenorm)
            wq = mla_wuq[j].reshape(MLA_Q_LORA, hh, MLA_NOPE + MLA_ROPE)
            wq2 = jnp.pad(wq, ((0, 0), (0, 0), (0, LANES - MLA_ROPE))).reshape(MLA_Q_LORA, hh * 2 * LANES)
            groups = ([(g * 2 * LANES, MLA_NOPE) for g in range(hh)]
                      + [(g * 2 * LANES + MLA_NOPE, MLA_ROPE) for g in range(hh)])
            qg = jnp.concatenate([mla_q_gain[j] * scale, zeros(LANES - MLA_ROPE)])
            gain = jnp.tile(qg, hh)
            q = proj(dn, 0, wq2.astype(BF16), groups, gain, rope, range(1, 2 * hh, 2), MLA_ROPE // 2, "mla_q")
            groups = [(g * (MLA_NOPE + MLA_V), MLA_NOPE) for g in range(hh)]
            gain = jnp.tile(jnp.concatenate([mla_k_gain[j][:MLA_NOPE], ones(MLA_V)]), hh)
            kv = proj(dn, ckv0 // MLA_KV_LORA, mla_wukv[j].astype(BF16), groups, gain, rope, [], MLA_ROPE // 2, "mla_kv")
            o = mla_attention(q, kv, dn, kpe0 // LANES, n_ctx)
            w_o = mla_wo[j]
        elif kind == 2:
            hd = WIN_HEAD_DIM
            scale = hd ** -0.5 * LOG2E
            n_pairs = WIN_HEADS // WIN_KV_HEADS
            nq = WIN_HEADS * hd
            perm = np.array([(kv * n_pairs + g) * hd + t for g in range(n_pairs)
                             for kv in range(WIN_KV_HEADS) for t in range(hd)])
            wq = win_wqkv[j]
            w2_ = jnp.concatenate([wq[:, perm], wq[:, nq:]], axis=1)
            groups = [(g * hd, hd) for g in range(WIN_HEADS + WIN_KV_HEADS)]
            gain = jnp.concatenate([jnp.tile(win_q_gain[j] * scale, WIN_HEADS),
                                    jnp.tile(win_k_gain[j], WIN_KV_HEADS), ones(WIN_KV_HEADS * hd)])
            qkv = proj(xa, 0, w2_.astype(BF16), groups, gain, _rope_tables(seq, n_ctx, hd),
                       range(n_pairs + 1), hd // 2, "win_qkv", prenorm)
            o = win_attention(qkv, win_sink[j] * LOG2E, n_ctx)
            w_o = win_wo[j][perm, :]
        else:
            hd = DIFF_HEAD_DIM
            scale = hd ** -0.5 * LOG2E
            lam_init = 0.8 - 0.6 * math.exp(-0.3 * i)
            nqk = 2 * DIFF_HEADS
            groups = [(g * hd, hd) for g in range(2 * nqk)]
            gain = jnp.concatenate([jnp.tile(diff_q_gain[j] * scale, nqk), jnp.tile(diff_k_gain[j], nqk),
                                    ones(DIFF_HEADS * 2 * hd)])
            qkv = proj(xa, 0, diff_wqkv[j].astype(BF16), groups, gain, _rope_tables(seq, n_ctx, hd),
                       range(2 * DIFF_HEADS), hd // 2, "diff_qkv", prenorm)
            o = diff_attention(qkv, diff_lambda[j], diff_subln[j], lam_init, n_ctx)
            w_o = diff_wo[j]
        xa = out_proj_residual(o, w_o.astype(BF16), xa, mods, 2, n_ctx)

        jj = i // 2
        if i % 2 == 0:
            xa = ffn_residual(norm_ffn[i], ffn_w13[jj].astype(BF16), ffn_w2[jj].astype(BF16), xa, mods, 3, 4, 5,
                              n_ctx)
        else:
            h, route = norm_mod_router(xa, norm_ffn[i], mods, 3, 4, n_ctx, moe_router[jj])
            xa = moe_residual(h, route, moe_w13[jj].astype(BF16), moe_w2[jj].astype(BF16), xa, mods, 5,
                              n_ctx, need_ctx)
    return xa[:, n_ctx:]
```

```python
import functools
import math

import jax
import jax.numpy as jnp
import numpy as np
from jax import lax
from jax.experimental import pallas as pl
from jax.experimental.pallas import tpu as pltpu

GRID_W = 64
ROPE_THETA = 10000.0
EPS = 1e-6
NEG_INF = -1e30
LANES = 128
MXU_WIDTH = 256
ROW_TILE = 256
FFN_ROW_TILE = 768
FF_CHUNK = 512
VMEM_LIMIT = 56 * 1024 * 1024

GQA_HEADS, GQA_KV_HEADS, GQA_HEAD_DIM = 8, 2, 128
MLA_HEADS, MLA_Q_LORA, MLA_KV_LORA, MLA_NOPE, MLA_ROPE, MLA_V = 8, 384, 256, 128, 64, 128
WIN_HEADS, WIN_KV_HEADS, WIN_HEAD_DIM, WINDOW = 16, 2, 64, 128
DIFF_HEADS, DIFF_HEAD_DIM = 8, 64
N_EXPERTS = 8

F32 = jnp.float32
BF16 = jnp.bfloat16


def _cparams(*sem):
    return pltpu.CompilerParams(dimension_semantics=sem, vmem_limit_bytes=VMEM_LIMIT)


def _dot(a, b):
    return jnp.dot(a, b, preferred_element_type=F32)


def _dot_nt(a, b):
    return lax.dot_general(a, b, (((1,), (1,)), ((), ())), preferred_element_type=F32)


def _split_bf16(v):
    hi = v.astype(BF16)
    lo = (v - hi.astype(F32)).astype(BF16)
    return hi, lo


def _adaln_kernel(c_ref, w_ref, b_ref, o_ref):
    c = c_ref[...]
    sc = c * (1.0 / (1.0 + jnp.exp(-c)))
    o_ref[0] = jnp.dot(sc, w_ref[0], preferred_element_type=F32,
                       precision=lax.Precision.HIGHEST) + b_ref[0]


def adaln(cond, ada_w, ada_b):
    depth, d, n = ada_w.shape
    rows = cond.shape[0]
    bn = 1536
    return pl.pallas_call(
        _adaln_kernel,
        grid=(depth, n // bn),
        in_specs=[
            pl.BlockSpec((rows, d), lambda l, j: (0, 0)),
            pl.BlockSpec((1, d, bn), lambda l, j: (l, 0, j)),
            pl.BlockSpec((1, 1, bn), lambda l, j: (l, 0, j)),
        ],
        out_specs=pl.BlockSpec((1, rows, bn), lambda l, j: (l, 0, j)),
        out_shape=jax.ShapeDtypeStruct((depth, rows, n), F32),
        compiler_params=_cparams("parallel", "parallel"),
        name="adaln",
    )(cond, ada_w, ada_b.reshape(depth, 1, n))


def _mod_spec(k, n_ctx_tiles, d):
    return pl.BlockSpec((None, None, None, 1, d),
                        lambda b, i, *_: (b, jnp.minimum(i // n_ctx_tiles, 1), k, 0, 0))


def _norm_mod(x, g, shift, scale):
    ms = jnp.mean(x * x, axis=-1, keepdims=True)
    return x * lax.rsqrt(ms + EPS) * g * (1.0 + scale) + shift


def _norm_mod_router_kernel(x_ref, g_ref, sh_ref, sc_ref, r_ref, h_ref, cw_ref):
    d = x_ref.shape[2]
    h = _norm_mod(x_ref[0], g_ref[...], sh_ref[...], sc_ref[...])
    h_ref[0, :, :d] = h.astype(BF16)
    logits = jnp.dot(h, r_ref[...], preferred_element_type=F32, precision=lax.Precision.HIGHEST)
    lane = lax.broadcasted_iota(jnp.int32, logits.shape, 1)
    logits = jnp.where(lane < N_EXPERTS, logits, -jnp.inf)
    m1 = jnp.max(logits, axis=-1, keepdims=True)
    i1 = jnp.min(jnp.where(logits == m1, lane, LANES), axis=-1, keepdims=True)
    rest = jnp.where(lane == i1, -jnp.inf, logits)
    m2 = jnp.max(rest, axis=-1, keepdims=True)
    i2 = jnp.min(jnp.where(rest == m2, lane, LANES), axis=-1, keepdims=True)
    e2 = jnp.exp(m2 - m1)
    den = 1.0 + e2
    g1, g2 = 1.0 / den, e2 / den
    cw_ref[0] = jnp.where(lane == 0, i1.astype(F32), jnp.where(lane == 1, i2.astype(F32),
                          jnp.where(lane == 2, g1, jnp.where(lane == 3, g2, 0.0))))
    g1_hi, g2_hi = g1.astype(BF16).astype(F32), g2.astype(BF16).astype(F32)
    ext = jnp.where(lane == ROUTE_G1, g1_hi, jnp.where(lane == ROUTE_G1 + 1, g1 - g1_hi,
          jnp.where(lane == ROUTE_G2, g2_hi, jnp.where(lane == ROUTE_G2 + 1, g2 - g2_hi,
          jnp.where(lane == ROUTE_E1, i1.astype(F32), jnp.where(lane == ROUTE_E2, i2.astype(F32), 0.0))))))
    h_ref[0, :, d:] = ext.astype(BF16)


ROUTE_G1, ROUTE_G2, ROUTE_E1, ROUTE_E2 = 0, 2, 4, 5


def norm_mod_router(xa, gain, mods, k_shift, k_scale, n_ctx, router):
    b, ta, d = xa.shape
    bm = ROW_TILE
    nct = n_ctx // bm
    rpad = jnp.zeros((d, LANES), F32).at[:, :N_EXPERTS].set(router)
    return pl.pallas_call(
        _norm_mod_router_kernel, grid=(b, ta // bm),
        in_specs=[
            pl.BlockSpec((1, bm, d), lambda b_, i: (b_, i, 0)),
            pl.BlockSpec((1, d), lambda b_, i: (0, 0)),
            _mod_spec(k_shift, nct, d),
            _mod_spec(k_scale, nct, d),
            pl.BlockSpec((d, LANES), lambda b_, i: (0, 0)),
        ],
        out_specs=[pl.BlockSpec((1, bm, d + LANES), lambda b_, i: (b_, i, 0)),
                   pl.BlockSpec((1, bm, LANES), lambda b_, i: (b_, i, 0))],
        out_shape=[jax.ShapeDtypeStruct((b, ta, d + LANES), BF16), jax.ShapeDtypeStruct((b, ta, LANES), F32)],
        compiler_params=_cparams("parallel", "parallel"), name="norm_mod_router",
    )(xa, gain.reshape(1, d), mods, mods, rpad)


def _proj_kernel(*refs, rope_blocks, rope_half, prenorm, norm_blocks):
    if prenorm:
        x_ref, g_ref, sh_ref, sc_ref = refs[:4]
        h = _norm_mod(x_ref[0], g_ref[...], sh_ref[...], sc_ref[...]).astype(BF16)
        refs = refs[4:]
    else:
        h = refs[0][0]
        refs = refs[1:]
    y = _dot(h, refs[0][...])
    n = y.shape[1]
    s_hi, s_lo = _split_bf16(y * y)
    if norm_blocks is None:
        _, e_ref, et_ref, igs_ref, u_ref, gain_ref, c_ref, s_ref, o_ref = refs
        ss = _dot(s_hi, e_ref[...]) + _dot(s_lo, e_ref[...])
        inv = lax.rsqrt(ss * igs_ref[...] + EPS)
        i_hi, i_lo = _split_bf16(inv)
        fac = _dot(i_hi, et_ref[...]) + _dot(i_lo, et_ref[...]) + u_ref[...]
        z = y * (fac * gain_ref[...])
    else:
        _, j_ref, igs_ref, u_ref, gain_ref, c_ref, s_ref, o_ref = refs
        zs = []
        for blk in range(n // MXU_WIDTH):
            sl = slice(blk * MXU_WIDTH, (blk + 1) * MXU_WIDTH)
            if blk in norm_blocks:
                ss = _dot(s_hi[:, sl], j_ref[blk]) + _dot(s_lo[:, sl], j_ref[blk])
                u = u_ref[:, sl]
                fac = lax.rsqrt(ss * igs_ref[:, sl] + EPS) * (1.0 - u) + u
                zs.append(y[:, sl] * (fac * gain_ref[:, sl]))
            else:
                zs.append(y[:, sl] * gain_ref[:, sl])
        z = jnp.concatenate(zs, axis=1)
    cos = c_ref[...]
    sin = s_ref[...]
    if rope_half == LANES // 2:
        first = None
    else:
        lane = lax.broadcasted_iota(jnp.int32, cos.shape, 1)
        first = (lane % (2 * rope_half)) < rope_half
    for blk in range(n // LANES):
        zb = z[:, blk * LANES:(blk + 1) * LANES]
        if blk in rope_blocks:
            if first is None:
                rot = pltpu.roll(zb, LANES // 2, 1)
            else:
                rot = jnp.where(first, pltpu.roll(zb, LANES - rope_half, 1), pltpu.roll(zb, rope_half, 1))
            zb = zb * cos + rot * sin
        o_ref[0, :, blk * LANES:(blk + 1) * LANES] = zb.astype(BF16)


def proj(h, kin_block, w, groups, gain, rope_tabs, rope_blocks, rope_half, name, prenorm=None):
    b, ta, _ = h.shape
    k, n = w.shape
    bm = ROW_TILE
    mw = MXU_WIDTH
    u = np.ones((1, n), np.float32)
    for start, size in groups:
        u[0, start:start + size] = 0.0
    local = n % mw == 0 and all(start // mw == (start + size - 1) // mw for start, size in groups)
    if local:
        jm = np.zeros((n // mw, mw, mw), np.float32)
        igs = np.zeros((1, n), np.float32)
        for start, size in groups:
            blk, o = divmod(start, mw)
            jm[blk, o:o + size, o:o + size] = 1.0
            igs[0, start:start + size] = 1.0 / size
        norm_blocks = frozenset(start // mw for start, _ in groups)
        norm_args = [jnp.asarray(jm, BF16), jnp.asarray(igs)]
        norm_specs = [pl.BlockSpec((n // mw, mw, mw), lambda b_, i: (0, 0, 0)),
                      pl.BlockSpec((1, n), lambda b_, i: (0, 0))]
    else:
        e = np.zeros((n, LANES), np.float32)
        igs = np.zeros((1, LANES), np.float32)
        for gi, (start, size) in enumerate(groups):
            e[start:start + size, gi] = 1.0
            igs[0, gi] = 1.0 / size
        norm_blocks = None
        norm_args = [jnp.asarray(e, BF16), jnp.asarray(e.T, BF16), jnp.asarray(igs)]
        norm_specs = [pl.BlockSpec((n, LANES), lambda b_, i: (0, 0)),
                      pl.BlockSpec((LANES, n), lambda b_, i: (0, 0)),
                      pl.BlockSpec((1, LANES), lambda b_, i: (0, 0))]
    cos, sin = rope_tabs
    kern = functools.partial(_proj_kernel, rope_blocks=frozenset(rope_blocks), rope_half=rope_half,
                             prenorm=prenorm is not None, norm_blocks=norm_blocks)
    lead_specs = [pl.BlockSpec((1, bm, k), lambda b_, i: (b_, i, kin_block))]
    lead_args = [h]
    if prenorm is not None:
        ngain, mods, k_shift, k_scale, n_ctx = prenorm
        nct = n_ctx // bm
        lead_specs += [pl.BlockSpec((1, k), lambda b_, i: (0, 0)), _mod_spec(k_shift, nct, k),
                       _mod_spec(k_scale, nct, k)]
        lead_args += [ngain.reshape(1, k), mods, mods]
    return pl.pallas_call(
        kern,
        grid=(b, ta // bm),
        in_specs=lead_specs + [pl.BlockSpec((k, n), lambda b_, i: (0, 0))] + norm_specs + [
            pl.BlockSpec((1, n), lambda b_, i: (0, 0)),
            pl.BlockSpec((1, n), lambda b_, i: (0, 0)),
            pl.BlockSpec((bm, LANES), lambda b_, i: (i, 0)),
            pl.BlockSpec((bm, LANES), lambda b_, i: (i, 0)),
        ],
        out_specs=pl.BlockSpec((1, bm, n), lambda b_, i: (b_, i, 0)),
        out_shape=jax.ShapeDtypeStruct((b, ta, n), BF16),
        compiler_params=_cparams("parallel", "parallel"),
        name=name,
    )(*lead_args, w, *norm_args, jnp.asarray(u), gain.reshape(1, n).astype(F32), cos, sin)


LOG2E = math.log2(math.e)


def _softmax_pv(s, v):
    m = jnp.max(s, axis=-1, keepdims=True)
    p = jnp.exp2(s - m)
    l = jnp.sum(p, axis=-1, keepdims=True)
    return _dot(p.astype(BF16), v) / l


def _gqa_attn_kernel(q_ref, k_ref, v_ref, o_ref, *, n_ctx, group, hd):
    i = pl.program_id(2)

    def attend(nk):
        k = k_ref[0, :nk, :]
        v = v_ref[0, :nk, :]
        for g in range(group):
            q = q_ref[0, :, g * hd:(g + 1) * hd]
            o = _softmax_pv(_dot_nt(q, k), v)
            o_ref[0, :, g * hd:(g + 1) * hd] = o.astype(BF16)

    nct = n_ctx // ROW_TILE

    @pl.when(i < nct)
    def _():
        attend(n_ctx)

    @pl.when(i >= nct)
    def _():
        attend(k_ref.shape[1])


def gqa_attention(qkv, n_ctx):
    b, ta, _ = qkv.shape
    hd, group, kvh = GQA_HEAD_DIM, GQA_HEADS // GQA_KV_HEADS, GQA_KV_HEADS
    bq = ROW_TILE
    qw = group * hd
    kb = GQA_HEADS
    vb = GQA_HEADS + kvh
    kern = functools.partial(_gqa_attn_kernel, n_ctx=n_ctx, group=group, hd=hd)
    return pl.pallas_call(
        kern,
        grid=(b, kvh, ta // bq),
        in_specs=[
            pl.BlockSpec((1, bq, qw), lambda b_, h, i: (b_, i, h)),
            pl.BlockSpec((1, ta, hd), lambda b_, h, i: (b_, 0, kb + h)),
            pl.BlockSpec((1, ta, hd), lambda b_, h, i: (b_, 0, vb + h)),
        ],
        out_specs=pl.BlockSpec((1, bq, qw), lambda b_, h, i: (b_, i, h)),
        out_shape=jax.ShapeDtypeStruct((b, ta, GQA_HEADS * hd), BF16),
        compiler_params=_cparams("parallel", "parallel", "arbitrary"),
        name="gqa_attn",
    )(qkv, qkv, qkv)


def _mla_attn_kernel(q_ref, kn_ref, kp_ref, v_ref, o_ref, kcat_ref, *, n_ctx):
    i = pl.program_id(2)

    @pl.when(i == 0)
    def _():
        kcat_ref[:, :LANES] = kn_ref[0]
        kcat_ref[:, LANES:] = kp_ref[0]

    def attend(nk):
        s = _dot_nt(q_ref[0], kcat_ref[:nk, :])
        o_ref[0] = _softmax_pv(s, v_ref[0, :nk, :]).astype(BF16)

    nct = n_ctx // ROW_TILE

    @pl.when(i < nct)
    def _():
        attend(n_ctx)

    @pl.when(i >= nct)
    def _():
        attend(kn_ref.shape[1])


def mla_attention(q, kv, dn, kpe_block, n_ctx):
    b, ta, _ = q.shape
    bq = ROW_TILE
    hh = MLA_HEADS
    kern = functools.partial(_mla_attn_kernel, n_ctx=n_ctx)
    return pl.pallas_call(
        kern,
        grid=(b, hh, ta // bq),
        in_specs=[
            pl.BlockSpec((1, bq, 2 * LANES), lambda b_, h, i: (b_, i, h)),
            pl.BlockSpec((1, ta, LANES), lambda b_, h, i: (b_, 0, 2 * h)),
            pl.BlockSpec((1, ta, LANES), lambda b_, h, i: (b_, 0, kpe_block)),
            pl.BlockSpec((1, ta, LANES), lambda b_, h, i: (b_, 0, 2 * h + 1)),
        ],
        out_specs=pl.BlockSpec((1, bq, LANES), lambda b_, h, i: (b_, i, h)),
        out_shape=jax.ShapeDtypeStruct((b, ta, hh * MLA_V), BF16),
        scratch_shapes=[pltpu.VMEM((ta, 2 * LANES), BF16)],
        compiler_params=_cparams("parallel", "parallel", "arbitrary"),
        name="mla_attn",
    )(q, kv, dn, kv)


def _half_masks(shape):
    lane = lax.broadcasted_iota(jnp.int32, shape, 1)
    return lane < (LANES // 2)


def _diff_attn_kernel(q_ref, k_ref, v_ref, lam_ref, sub_ref, o_ref, *, n_ctx, lam_init):
    i = pl.program_id(2)
    lp = lam_ref[...]
    lam = (jnp.exp(jnp.sum(lp[0:1] * lp[1:2], axis=-1, keepdims=True))
           - jnp.exp(jnp.sum(lp[2:3] * lp[3:4], axis=-1, keepdims=True)) + lam_init)

    def attend(nk):
        k = k_ref[0, :nk, :]
        v = v_ref[0, :nk, :]

        def probs(s, weight):
            m = jnp.max(s, axis=-1, keepdims=True)
            e = jnp.exp2(s - m)
            return e * (weight / jnp.sum(e, axis=-1, keepdims=True))

        q = q_ref[0]
        lo = _half_masks(q.shape)
        zero = jnp.zeros_like(q)
        p = probs(_dot_nt(jnp.where(lo, q, zero), k), 1.0) - probs(_dot_nt(jnp.where(lo, zero, q), k), lam)
        o = _dot(p.astype(BF16), v)
        ms = jnp.mean(o * o, axis=-1, keepdims=True)
        o_ref[0] = (o * lax.rsqrt(ms + EPS) * sub_ref[...] * (1.0 - lam_init)).astype(BF16)

    nct = n_ctx // ROW_TILE

    @pl.when(i < nct)
    def _():
        attend(n_ctx)

    @pl.when(i >= nct)
    def _():
        attend(k_ref.shape[1])


def diff_attention(qkv, lam_p, subln, lam_init, n_ctx):
    b, ta, _ = qkv.shape
    bq = ROW_TILE
    hh = DIFF_HEADS
    kern = functools.partial(_diff_attn_kernel, n_ctx=n_ctx, lam_init=lam_init)
    return pl.pallas_call(
        kern,
        grid=(b, hh, ta // bq),
        in_specs=[
            pl.BlockSpec((1, bq, LANES), lambda b_, h, i: (b_, i, h)),
            pl.BlockSpec((1, ta, LANES), lambda b_, h, i: (b_, 0, hh + h)),
            pl.BlockSpec((1, ta, LANES), lambda b_, h, i: (b_, 0, 2 * hh + h)),
            pl.BlockSpec((4, DIFF_HEAD_DIM), lambda b_, h, i: (0, 0)),
            pl.BlockSpec((1, LANES), lambda b_, h, i: (0, 0)),
        ],
        out_specs=pl.BlockSpec((1, bq, LANES), lambda b_, h, i: (b_, i, h)),
        out_shape=jax.ShapeDtypeStruct((b, ta, hh * 2 * DIFF_HEAD_DIM), BF16),
        compiler_params=_cparams("parallel", "parallel", "arbitrary"),
        name="diff_attn",
    )(qkv, qkv, qkv, lam_p.astype(F32), subln.reshape(1, LANES).astype(F32))


def _win_attn_kernel(sink_ref, q_ref, k_ref, v_ref, o_ref, *, n_ctx, seq, n_pairs):
    i = pl.program_id(1)
    bq = q_ref.shape[1]
    band = bq + 2 * WINDOW
    nct = n_ctx // bq

    def head_out(qm, kv_idx, pair, blocks):
        sk = sink_ref[kv_idx * n_pairs + pair]
        ss = []
        m = None
        for k, _, mask in blocks:
            s = _dot_nt(qm, k)
            if mask is not None:
                s = jnp.where(mask, s, NEG_INF)
            ss.append(s)
            bm_ = jnp.max(s, axis=-1, keepdims=True)
            m = bm_ if m is None else jnp.maximum(m, bm_)
        m = jnp.maximum(m, sk)
        l = jnp.exp2(sk - m)
        o = None
        for s, (_, v, _) in zip(ss, blocks):
            p = jnp.exp2(s - m)
            l = l + jnp.sum(p, axis=-1, keepdims=True)
            pv = _dot(p.astype(BF16), v)
            o = pv if o is None else o + pv
        return o / l

    def run(blocks):
        for pair in range(n_pairs):
            q = q_ref[0, :, pair * LANES:(pair + 1) * LANES]
            lo = _half_masks(q.shape)
            zero = jnp.zeros_like(q)
            o0 = head_out(jnp.where(lo, q, zero), 0, pair, blocks)
            o1 = head_out(jnp.where(lo, zero, q), 1, pair, blocks)
            o_ref[0, :, pair * LANES:(pair + 1) * LANES] = jnp.where(lo, o0, o1).astype(BF16)

    @pl.when(i < nct)
    def _():
        run([(k_ref[0, :n_ctx, :], v_ref[0, :n_ctx, :], None)])

    @pl.when(i >= nct)
    def _():
        q0 = (i - nct) * bq
        start = jnp.clip(q0 - WINDOW, 0, seq - band)
        start = pl.multiple_of(start, WINDOW)
        qpos = q0 + lax.broadcasted_iota(jnp.int32, (bq, band), 0)
        kpos = start + lax.broadcasted_iota(jnp.int32, (bq, band), 1)
        mask = jnp.abs(qpos - kpos) <= WINDOW
        kb = k_ref[0, pl.ds(n_ctx + start, band), :]
        vb = v_ref[0, pl.ds(n_ctx + start, band), :]
        run([(k_ref[0, :n_ctx, :], v_ref[0, :n_ctx, :], None), (kb, vb, mask)])


def win_attention(qkv, sink, n_ctx):
    b, ta, _ = qkv.shape
    bq = ROW_TILE
    n_pairs = WIN_HEADS // WIN_KV_HEADS
    qw = n_pairs * LANES
    kern = functools.partial(_win_attn_kernel, n_ctx=n_ctx, seq=ta - n_ctx, n_pairs=n_pairs)
    return pl.pallas_call(
        kern,
        grid_spec=pltpu.PrefetchScalarGridSpec(
            num_scalar_prefetch=1,
            grid=(b, ta // bq),
            in_specs=[
                pl.BlockSpec((1, bq, qw), lambda b_, i, s: (b_, i, 0)),
                pl.BlockSpec((1, ta, LANES), lambda b_, i, s: (b_, 0, n_pairs)),
                pl.BlockSpec((1, ta, LANES), lambda b_, i, s: (b_, 0, n_pairs + 1)),
            ],
            out_specs=pl.BlockSpec((1, bq, qw), lambda b_, i, s: (b_, i, 0)),
        ),
        out_shape=jax.ShapeDtypeStruct((b, ta, qw), BF16),
        compiler_params=_cparams("parallel", "arbitrary"),
        name="win_attn",
    )(sink.astype(F32), qkv, qkv, qkv)


def _out_proj_kernel(o_ref, w_ref, x_ref, g_ref, y_ref):
    y_ref[0] = x_ref[0] + g_ref[...] * _dot(o_ref[0], w_ref[...])


def out_proj_residual(o, w, xa, mods, k_gate, n_ctx):
    b, ta, d = xa.shape
    kdim = o.shape[2]
    bm = ROW_TILE
    return pl.pallas_call(
        _out_proj_kernel,
        grid=(b, ta // bm),
        in_specs=[
            pl.BlockSpec((1, bm, kdim), lambda b_, i: (b_, i, 0)),
            pl.BlockSpec((kdim, d), lambda b_, i: (0, 0)),
            pl.BlockSpec((1, bm, d), lambda b_, i: (b_, i, 0)),
            _mod_spec(k_gate, n_ctx // bm, d),
        ],
        out_specs=pl.BlockSpec((1, bm, d), lambda b_, i: (b_, i, 0)),
        out_shape=jax.ShapeDtypeStruct(xa.shape, F32),
        input_output_aliases={2: 0},
        compiler_params=_cparams("parallel", "parallel"),
        name="out_proj",
    )(o, w, xa, mods)


def _ffn_kernel(ng_ref, wg_ref, wu_ref, w2_ref, x_ref, mc_ref, ml_ref, y_ref, acc_ref, h_ref, *,
                n_ctx, k_shift, k_scale, k_gate):
    i = pl.program_id(1)
    f = pl.program_id(2)
    bm = acc_ref.shape[0]

    def row_mod(k):
        row = i * bm + lax.broadcasted_iota(jnp.int32, (bm, 1), 0)
        return jnp.where(row < n_ctx, mc_ref[k], ml_ref[k])

    @pl.when(f == 0)
    def _():
        acc_ref[...] = jnp.zeros_like(acc_ref)
        h_ref[...] = _norm_mod(x_ref[0], ng_ref[...], row_mod(k_shift), row_mod(k_scale)).astype(BF16)

    h = h_ref[...]
    g = _dot(h, wg_ref[...])
    u = _dot(h, wu_ref[...])
    act = (g * (1.0 / (1.0 + jnp.exp(-g))) * u).astype(BF16)
    acc_ref[...] += _dot(act, w2_ref[...])

    @pl.when(f == pl.num_programs(2) - 1)
    def _():
        y_ref[0] = x_ref[0] + row_mod(k_gate) * acc_ref[...]


def ffn_residual(norm_gain, w13, w2, xa, mods, k_shift, k_scale, k_gate, n_ctx):
    b, ta, d = xa.shape
    ff = w2.shape[0]
    bm, fk = FFN_ROW_TILE, FF_CHUNK
    nf = ff // fk
    kern = functools.partial(_ffn_kernel, n_ctx=n_ctx, k_shift=k_shift, k_scale=k_scale, k_gate=k_gate)
    in_specs = [
        pl.BlockSpec((1, d), lambda b_, i, f: (0, 0)),
        pl.BlockSpec((d, fk), lambda b_, i, f: (0, f)),
        pl.BlockSpec((d, fk), lambda b_, i, f: (0, nf + f)),
        pl.BlockSpec((fk, d), lambda b_, i, f: (f, 0)),
        pl.BlockSpec((1, bm, d), lambda b_, i, f: (b_, i, 0)),
        pl.BlockSpec((None, None, 6, 1, d), lambda b_, i, f: (b_, 0, 0, 0, 0)),
        pl.BlockSpec((None, None, 6, 1, d), lambda b_, i, f: (b_, 1, 0, 0, 0)),
    ]
    return pl.pallas_call(
        kern,
        grid=(b, ta // bm, nf),
        in_specs=in_specs,
        out_specs=pl.BlockSpec((1, bm, d), lambda b_, i, f: (b_, i, 0)),
        out_shape=jax.ShapeDtypeStruct(xa.shape, F32),
        scratch_shapes=[pltpu.VMEM((bm, d), F32), pltpu.VMEM((bm, d), BF16)],
        input_output_aliases={4: 0},
        compiler_params=_cparams("parallel", "parallel", "arbitrary"),
        name="ffn",
    )(norm_gain.reshape(1, d), w13, w13, w2, xa, mods, mods)


MOE_SLOT_TILE = 512
MOE_GATHER_CHUNK = 256
MOE_GATHER_SUBTILES = 2
MOE_GATHER_FANIN = 6
MOE_COMBINE_CHUNK = 768
_VALID, _FIRST, _LAST = 1, 2, 4


def _one_hot_bf16(cond):
    return jnp.where(cond, 1.0, 0.0).astype(BF16)


def _moe_gather_kernel(tile_ref, base_ref, hi_ref, flag_ref, *refs, nsub, fan):
    nin = nsub * fan
    p1_refs, p2_refs, h_refs, o_ref = refs[:nin], refs[nin:2 * nin], refs[2 * nin:3 * nin], refs[3 * nin]
    k = pl.program_id(0)
    fl = flag_ref[k]

    def picked():
        sbm, ch = o_ref.shape[0] // nsub, h_refs[0].shape[0]
        rows = []
        for s in range(nsub):
            slot = tile_ref[k] * (sbm * nsub) + s * sbm + lax.broadcasted_iota(jnp.int32, (sbm, ch), 0)
            acc = None
            for j in range(fan):
                r = s * fan + j
                slot_j = jnp.where(base_ref[k * nsub + s] + j <= hi_ref[k * nsub + s], slot, -2)
                hit = (p1_refs[r][...] == slot_j) | (p2_refs[r][...] == slot_j)
                part = _dot(_one_hot_bf16(hit), h_refs[r][...])
                acc = part if acc is None else acc + part
            rows.append(acc.astype(BF16))
        return jnp.concatenate(rows, axis=0)

    @pl.when((fl & _FIRST) != 0)
    def _():
        o_ref[...] = picked()

    @pl.when(((fl & _VALID) != 0) & ((fl & _FIRST) == 0))
    def _():
        o_ref[...] += picked()


def _moe_ffn_kernel(te_ref, nu_ref, xs_ref, wg_ref, wu_ref, w2_ref, y_ref, acc_ref):
    i = pl.program_id(0)
    f = pl.program_id(1)
    d = acc_ref.shape[1]

    @pl.when(i < nu_ref[0])
    def _():
        @pl.when(f == 0)
        def _():
            acc_ref[...] = jnp.zeros_like(acc_ref)

        xs = xs_ref[:, :d]
        g = _dot(xs, wg_ref[...])
        u = _dot(xs, wu_ref[...])
        act = (g * (1.0 / (1.0 + jnp.exp(-g))) * u).astype(BF16)
        acc_ref[...] += _dot(act, w2_ref[...])

        @pl.when(f == pl.num_programs(1) - 1)
        def _():
            rt = xs_ref[:, d:].astype(F32)
            first = rt[:, ROUTE_E1:ROUTE_E1 + 1] == te_ref[i].astype(F32)
            gate = jnp.where(first, rt[:, ROUTE_G1:ROUTE_G1 + 1] + rt[:, ROUTE_G1 + 1:ROUTE_G1 + 2],
                             rt[:, ROUTE_G2:ROUTE_G2 + 1] + rt[:, ROUTE_G2 + 1:ROUTE_G2 + 2])
            y_ref[...] = (acc_ref[...] * gate).astype(BF16)


def _moe_combine_kernel(chunk_ref, tile_ref, flag_ref, p1_ref, p2_ref, y_ref, x_ref, gc_ref, gl_ref,
                        o_ref, acc_ref, *, n_ctx, chunks_per_batch):
    k = pl.program_id(0)
    fl = flag_ref[k]
    ch, bm = acc_ref.shape[0], y_ref.shape[0]

    @pl.when((fl & _FIRST) != 0)
    def _():
        acc_ref[...] = jnp.zeros_like(acc_ref)

    @pl.when((fl & _VALID) != 0)
    def _():
        slot = tile_ref[k] * bm + lax.broadcasted_iota(jnp.int32, (ch, bm), 1)
        w = _one_hot_bf16((p1_ref[...] == slot) | (p2_ref[...] == slot))
        acc_ref[...] += _dot(w, y_ref[...])

    @pl.when((fl & _LAST) != 0)
    def _():
        row = (chunk_ref[k] % chunks_per_batch) * ch + lax.broadcasted_iota(jnp.int32, (ch, 1), 0)
        gate = jnp.where(row < n_ctx, gc_ref[...], gl_ref[...])
        o_ref[...] = x_ref[...] + gate * acc_ref[...]


def _expand_items(counts, n_items):
    off = jnp.cumsum(counts)
    total = off[-1]
    k = jnp.minimum(jnp.arange(n_items, dtype=jnp.int32), total - 1)
    grp = jnp.minimum(jnp.sum(off[None, :] <= k[:, None], axis=1).astype(jnp.int32), counts.shape[0] - 1)
    local = k - (off[grp] - counts[grp])
    valid = jnp.arange(n_items, dtype=jnp.int32) < total
    return grp, local, valid


def moe_residual(h, route, w13, w2, xa, mods, k_gate, n_ctx, ctx_active):
    b, ta, d = xa.shape
    t = b * ta
    ne, ff = w2.shape[0], w2.shape[1]
    bm, gch, cch, fk = MOE_SLOT_TILE, MOE_GATHER_CHUNK, MOE_COMBINE_CHUNK, FF_CHUNK
    nf = ff // fk
    nt = (2 * t) // bm + ne
    ns = nt * bm
    i32 = jnp.int32

    dx = h.shape[2]
    r = route.reshape(t, LANES)
    e1, e2 = r[:, 0].astype(i32), r[:, 1].astype(i32)
    tok = jnp.arange(t, dtype=i32)
    active = jnp.ones((t,), bool) if ctx_active else (tok % ta) >= n_ctx

    eid = jnp.arange(ne, dtype=i32)[:, None]
    member = ((e1[None] == eid) | (e2[None] == eid)) & active[None]
    csum = jnp.cumsum(member.astype(i32), axis=1)
    cap = (csum[:, -1] + bm - 1) // bm * bm
    end = jnp.cumsum(cap)
    start = end - cap
    n_used = (end[-1] // bm).astype(i32)

    def slot_of(e_sel):
        rank = jnp.take_along_axis(csum, e_sel[None], axis=0)[0] - 1
        return jnp.where(active, start[e_sel] + rank, -1)

    pos1, pos2 = slot_of(e1), slot_of(e2)
    tile_ids = jnp.arange(nt, dtype=i32)
    tile_expert = jnp.minimum(jnp.sum(end[None] <= (tile_ids * bm)[:, None], axis=1), ne - 1).astype(i32)

    nsub, fan = MOE_GATHER_SUBTILES, MOE_GATHER_FANIN
    sbm = bm // nsub
    ngc = t // gch
    sub_ids = jnp.arange(nt * nsub, dtype=i32)
    sub_expert = tile_expert[sub_ids // nsub]
    members = csum[sub_expert, -1]
    r0 = sub_ids * sbm - start[sub_expert]
    r1 = jnp.minimum(r0 + sbm, members) - 1
    through_g = csum[:, gch - 1::gch][sub_expert]
    c_lo = jnp.minimum(jnp.sum(through_g <= r0[:, None], axis=1), ngc - 1).astype(i32)
    c_hi = jnp.minimum(jnp.sum(through_g <= r1[:, None], axis=1), ngc - 1).astype(i32)
    live = (sub_ids // nsub < n_used) & (r0 < members)
    n_ch = jnp.where(live, c_hi - c_lo + 1, 0).reshape(nt, nsub)
    c_lo = c_lo.reshape(nt, nsub)
    c_hi = jnp.where(live, c_hi, -1).reshape(nt, nsub)
    n_rd = (jnp.max(n_ch, axis=1) + fan - 1) // fan
    ni_g = (ne * ngc + nt) // fan + nt + 1
    g_tile, g_round, g_valid = _expand_items(n_rd, ni_g)
    g_base = (c_lo[g_tile] + g_round[:, None] * fan).reshape(-1)
    g_hi = c_hi[g_tile].reshape(-1)
    g_flag = g_valid * (_VALID + _FIRST * (g_round == 0))

    def chunk_of(s, j):
        return lambda k, tl, bs, hi, fl: jnp.clip(bs[k * nsub + s] + j, 0, jnp.maximum(hi[k * nsub + s], 0))

    pairs = [(s, j) for s in range(nsub) for j in range(fan)]
    row_specs = [pl.BlockSpec((None, 1, gch), lambda k, *a, c=chunk_of(s, j): (c(k, *a), 0, 0)) for s, j in pairs]
    h_specs = [pl.BlockSpec((gch, dx), lambda k, *a, c=chunk_of(s, j): (c(k, *a), 0)) for s, j in pairs]
    pos1_rows, pos2_rows, h_rows = pos1.reshape(ngc, 1, gch), pos2.reshape(ngc, 1, gch), h.reshape(t, dx)
    nin = len(pairs)
    xs = pl.pallas_call(
        functools.partial(_moe_gather_kernel, nsub=nsub, fan=fan),
        grid_spec=pltpu.PrefetchScalarGridSpec(
            num_scalar_prefetch=4,
            grid=(ni_g,),
            in_specs=row_specs + row_specs + h_specs,
            out_specs=pl.BlockSpec((bm, dx), lambda k, tl, bs, hi, fl: (tl[k], 0)),
        ),
        out_shape=jax.ShapeDtypeStruct((ns, dx), BF16),
        compiler_params=_cparams("arbitrary"),
        name="moe_gather",
    )(g_tile, g_base, g_hi, g_flag.astype(i32), *([pos1_rows] * nin), *([pos2_rows] * nin), *([h_rows] * nin))

    def last_used(i, nu):
        return jnp.minimum(i, nu[0] - 1)

    def f_eff(i, f, nu):
        return jnp.where(i < nu[0], f, nf - 1)

    y = pl.pallas_call(
        _moe_ffn_kernel,
        grid_spec=pltpu.PrefetchScalarGridSpec(
            num_scalar_prefetch=2,
            grid=(nt, nf),
            in_specs=[
                pl.BlockSpec((bm, dx), lambda i, f, te, nu: (last_used(i, nu), 0)),
                pl.BlockSpec((None, d, fk), lambda i, f, te, nu: (te[i], 0, f_eff(i, f, nu))),
                pl.BlockSpec((None, d, fk), lambda i, f, te, nu: (te[i], 0, nf + f_eff(i, f, nu))),
                pl.BlockSpec((None, fk, d), lambda i, f, te, nu: (te[i], f_eff(i, f, nu), 0)),
            ],
            out_specs=pl.BlockSpec((bm, d), lambda i, f, te, nu: (last_used(i, nu), 0)),
            scratch_shapes=[pltpu.VMEM((bm, d), F32)],
        ),
        out_shape=jax.ShapeDtypeStruct((ns, d), BF16),
        compiler_params=_cparams("arbitrary", "arbitrary"),
        name="moe_ffn",
    )(tile_expert, n_used.reshape(1), xs, w13, w13, w2)

    nc = t // cch
    cs0 = jnp.concatenate([jnp.zeros((ne, 1), i32), csum], axis=1)[:, ::cch]
    before, through = cs0[:, :-1].T, cs0[:, 1:].T
    t_lo = (start[None] + before) // bm
    t_hi = (start[None] + through - 1) // bm
    n_tl = jnp.where(through > before, t_hi - t_lo + 1, 0).reshape(-1)
    ni_c = ne * nc + nt
    c_pair, c_local, c_valid = _expand_items(n_tl, ni_c)
    c_chunk = c_pair // ne
    c_tile = t_lo.reshape(-1)[c_pair] + c_local
    off = jnp.cumsum(n_tl)
    k_eff = jnp.minimum(jnp.arange(ni_c, dtype=i32), off[-1] - 1)
    chunk_first = (off - n_tl)[c_chunk * ne]
    chunk_last = off[c_chunk * ne + ne - 1] - 1
    c_flag = c_valid * (_VALID + _FIRST * (k_eff == chunk_first) + _LAST * (k_eff == chunk_last))

    cpb = ta // cch
    kern = functools.partial(_moe_combine_kernel, n_ctx=n_ctx, chunks_per_batch=cpb)
    out = pl.pallas_call(
        kern,
        grid_spec=pltpu.PrefetchScalarGridSpec(
            num_scalar_prefetch=3,
            grid=(ni_c,),
            in_specs=[
                pl.BlockSpec((cch, 1), lambda k, cn, tl, fl: (cn[k], 0)),
                pl.BlockSpec((cch, 1), lambda k, cn, tl, fl: (cn[k], 0)),
                pl.BlockSpec((bm, d), lambda k, cn, tl, fl: (tl[k], 0)),
                pl.BlockSpec((cch, d), lambda k, cn, tl, fl: (cn[k], 0)),
                pl.BlockSpec((None, None, None, 1, d), lambda k, cn, tl, fl: (cn[k] // cpb, 0, k_gate, 0, 0)),
                pl.BlockSpec((None, None, None, 1, d), lambda k, cn, tl, fl: (cn[k] // cpb, 1, k_gate, 0, 0)),
            ],
            out_specs=pl.BlockSpec((cch, d), lambda k, cn, tl, fl: (cn[k], 0)),
            scratch_shapes=[pltpu.VMEM((cch, d), F32)],
        ),
        out_shape=jax.ShapeDtypeStruct((t, d), F32),
        input_output_aliases={6: 0},
        compiler_params=_cparams("arbitrary"),
        name="moe_combine",
    )(c_chunk, c_tile, c_flag.astype(i32), pos1.reshape(t, 1), pos2.reshape(t, 1), y, xa.reshape(t, d), mods, mods)
    return out.reshape(b, ta, d)


def _rope_tables(seq, n_ctx, rot_dim):
    rows = seq // GRID_W
    quarter = rot_dim // 4
    inv_freq = ROPE_THETA ** (-jnp.arange(quarter, dtype=F32) / quarter)
    row = jnp.repeat(jnp.arange(rows, dtype=F32), GRID_W)
    col = jnp.tile(jnp.arange(GRID_W, dtype=F32), rows)
    ang = jnp.concatenate([row[:, None] * inv_freq, col[:, None] * inv_freq], axis=-1)
    cos, sin = jnp.cos(ang), jnp.sin(ang)
    reps = LANES // rot_dim
    c = jnp.tile(jnp.concatenate([cos, cos], axis=-1), (1, reps))
    s = jnp.tile(jnp.concatenate([-sin, sin], axis=-1), (1, reps))
    c = jnp.concatenate([jnp.ones((n_ctx, LANES), F32), c], axis=0)
    s = jnp.concatenate([jnp.zeros((n_ctx, LANES), F32), s], axis=0)
    return c, s


def kernel(x, c, ctx, c_ctx, ada_w, ada_b, norm_mix, norm_ffn, gqa_wqkv, gqa_q_gain, gqa_k_gain, gqa_wo, mla_wdown, mla_qa_gain, mla_kva_gain, mla_wuq, mla_wukv, mla_q_gain, mla_k_gain, mla_wo, win_wqkv, win_q_gain, win_k_gain, win_sink, win_wo, diff_wqkv, diff_q_gain, diff_k_gain, diff_lambda, diff_subln, diff_wo, ffn_w13, ffn_w2, moe_router, moe_w13, moe_w2):
    b, seq, d = x.shape
    n_ctx = ctx.shape[1]
    depth = ada_w.shape[0]
    ones = lambda n: jnp.ones((n,), F32)
    zeros = lambda n: jnp.zeros((n,), F32)

    crows = -(-(b + 1) // 8) * 8
    cond = jnp.zeros((crows, d), F32).at[:b].set(c).at[b].set(c_ctx)
    mods_all = adaln(cond, ada_w, ada_b)

    xa = jnp.concatenate([ctx, x], axis=1)

    for i in range(depth):
        need_ctx = i < depth - 1
        j = i // 4
        kind = i % 4
        ml = mods_all[i, :b].reshape(b, 6, 1, d)
        mc = jnp.broadcast_to(mods_all[i, b].reshape(1, 6, 1, d), (b, 6, 1, d))
        mods = jnp.stack([mc, ml], axis=1)

        prenorm = (norm_mix[i], mods, 0, 1, n_ctx)
        if kind == 0:
            hd = GQA_HEAD_DIM
            scale = hd ** -0.5 * LOG2E
            nq, nk = GQA_HEADS * hd, GQA_KV_HEADS * hd
            groups = [(g * hd, hd) for g in range(GQA_HEADS + GQA_KV_HEADS)]
            gain = jnp.concatenate([jnp.tile(gqa_q_gain[j] * scale, GQA_HEADS),
                                    jnp.tile(gqa_k_gain[j], GQA_KV_HEADS), ones(nk)])
            qkv = proj(xa, 0, gqa_wqkv[j].astype(BF16), groups, gain, _rope_tables(seq, n_ctx, hd),
                       range((nq + nk) // LANES), hd // 2, "gqa_qkv", prenorm)
            o = gqa_attention(qkv, n_ctx)
            w_o = gqa_wo[j]
        elif kind == 1:
            scale = (MLA_NOPE + MLA_ROPE) ** -0.5 * LOG2E
            hh = MLA_HEADS
            wd = mla_wdown[j]
            pad = jnp.zeros((d, LANES - MLA_ROPE), F32)
            wd2 = jnp.concatenate([wd[:, :MLA_Q_LORA], wd[:, MLA_Q_LORA + MLA_KV_LORA:], pad,
                                   wd[:, MLA_Q_LORA:MLA_Q_LORA + MLA_KV_LORA]], axis=1)
            kpe0 = MLA_Q_LORA
            ckv0 = MLA_Q_LORA + LANES
            groups = [(0, MLA_Q_LORA), (kpe0, MLA_ROPE), (ckv0, MLA_KV_LORA)]
            gain = jnp.concatenate([mla_qa_gain[j], mla_k_gain[j][MLA_NOPE:], zeros(LANES - MLA_ROPE),
                                    mla_kva_gain[j]])
            rope = _rope_tables(seq, n_ctx, MLA_ROPE)
            dn = proj(xa, 0, wd2.astype(BF16), groups, gain, rope, [kpe0 // LANES], MLA_ROPE // 2, "mla_down",
                      prenorm)
            wq = mla_wuq[j].reshape(MLA_Q_LORA, hh, MLA_NOPE + MLA_ROPE)
            wq2 = jnp.pad(wq, ((0, 0), (0, 0), (0, LANES - MLA_ROPE))).reshape(MLA_Q_LORA, hh * 2 * LANES)
            groups = ([(g * 2 * LANES, MLA_NOPE) for g in range(hh)]
                      + [(g * 2 * LANES + MLA_NOPE, MLA_ROPE) for g in range(hh)])
            qg = jnp.concatenate([mla_q_gain[j] * scale, zeros(LANES - MLA_ROPE)])
            gain = jnp.tile(qg, hh)
            q = proj(dn, 0, wq2.astype(BF16), groups, gain, rope, range(1, 2 * hh, 2), MLA_ROPE // 2, "mla_q")
            groups = [(g * (MLA_NOPE + MLA_V), MLA_NOPE) for g in range(hh)]
            gain = jnp.tile(jnp.concatenate([mla_k_gain[j][:MLA_NOPE], ones(MLA_V)]), hh)
            kv = proj(dn, ckv0 // MLA_KV_LORA, mla_wukv[j].astype(BF16), groups, gain, rope, [], MLA_ROPE // 2, "mla_kv")
            o = mla_attention(q, kv, dn, kpe0 // LANES, n_ctx)
            w_o = mla_wo[j]
        elif kind == 2:
            hd = WIN_HEAD_DIM
            scale = hd ** -0.5 * LOG2E
            n_pairs = WIN_HEADS // WIN_KV_HEADS
            nq = WIN_HEADS * hd
            perm = np.array([(kv * n_pairs + g) * hd + t for g in range(n_pairs)
                             for kv in range(WIN_KV_HEADS) for t in range(hd)])
            wq = win_wqkv[j]
            w2_ = jnp.concatenate([wq[:, perm], wq[:, nq:]], axis=1)
            groups = [(g * hd, hd) for g in range(WIN_HEADS + WIN_KV_HEADS)]
            gain = jnp.concatenate([jnp.tile(win_q_gain[j] * scale, WIN_HEADS),
                                    jnp.tile(win_k_gain[j], WIN_KV_HEADS), ones(WIN_KV_HEADS * hd)])
            qkv = proj(xa, 0, w2_.astype(BF16), groups, gain, _rope_tables(seq, n_ctx, hd),
                       range(n_pairs + 1), hd // 2, "win_qkv", prenorm)
            o = win_attention(qkv, win_sink[j] * LOG2E, n_ctx)
            w_o = win_wo[j][perm, :]
        else:
            hd = DIFF_HEAD_DIM
            scale = hd ** -0.5 * LOG2E
            lam_init = 0.8 - 0.6 * math.exp(-0.3 * i)
            nqk = 2 * DIFF_HEADS
            groups = [(g * hd, hd) for g in range(2 * nqk)]
            gain = jnp.concatenate([jnp.tile(diff_q_gain[j] * scale, nqk), jnp.tile(diff_k_gain[j], nqk),
                                    ones(DIFF_HEADS * 2 * hd)])
            qkv = proj(xa, 0, diff_wqkv[j].astype(BF16), groups, gain, _rope_tables(seq, n_ctx, hd),
                       range(2 * DIFF_HEADS), hd // 2, "diff_qkv", prenorm)
            o = diff_attention(qkv, diff_lambda[j], diff_subln[j], lam_init, n_ctx)
            w_o = diff_wo[j]
        xa = out_proj_residual(o, w_o.astype(BF16), xa, mods, 2, n_ctx)

        jj = i // 2
        if i % 2 == 0:
            xa = ffn_residual(norm_ffn[i], ffn_w13[jj].astype(BF16), ffn_w2[jj].astype(BF16), xa, mods, 3, 4, 5,
                              n_ctx)
        else:
            h, route = norm_mod_router(xa, norm_ffn[i], mods, 3, 4, n_ctx, moe_router[jj])
            xa = moe_residual(h, route, moe_w13[jj].astype(BF16), moe_w2[jj].astype(BF16), xa, mods, 5,
                              n_ctx, need_ctx)
    return xa[:, n_ctx:]
```

```python
import functools
import math

import jax
import jax.numpy as jnp
import numpy as np
from jax import lax
from jax.experimental import pallas as pl
from jax.experimental.pallas import tpu as pltpu

GRID_W = 64
ROPE_THETA = 10000.0
EPS = 1e-6
NEG_INF = -1e30
LANES = 128
MXU_WIDTH = 256
ROW_TILE = 256
FFN_ROW_TILE = 768
FF_CHUNK = 512
VMEM_LIMIT = 56 * 1024 * 1024

GQA_HEADS, GQA_KV_HEADS, GQA_HEAD_DIM = 8, 2, 128
MLA_HEADS, MLA_Q_LORA, MLA_KV_LORA, MLA_NOPE, MLA_ROPE, MLA_V = 8, 384, 256, 128, 64, 128
WIN_HEADS, WIN_KV_HEADS, WIN_HEAD_DIM, WINDOW = 16, 2, 64, 128
DIFF_HEADS, DIFF_HEAD_DIM = 8, 64
N_EXPERTS = 8

F32 = jnp.float32
BF16 = jnp.bfloat16


def _cparams(*sem):
    return pltpu.CompilerParams(dimension_semantics=sem, vmem_limit_bytes=VMEM_LIMIT)


def _dot(a, b):
    return jnp.dot(a, b, preferred_element_type=F32)


def _dot_nt(a, b):
    return lax.dot_general(a, b, (((1,), (1,)), ((), ())), preferred_element_type=F32)


def _split_bf16(v):
    hi = v.astype(BF16)
    lo = (v - hi.astype(F32)).astype(BF16)
    return hi, lo


def _adaln_kernel(c_ref, w_ref, b_ref, o_ref):
    c = c_ref[...]
    sc = c * (1.0 / (1.0 + jnp.exp(-c)))
    o_ref[0] = jnp.dot(sc, w_ref[0], preferred_element_type=F32,
                       precision=lax.Precision.HIGHEST) + b_ref[0]


def adaln(cond, ada_w, ada_b):
    depth, d, n = ada_w.shape
    rows = cond.shape[0]
    bn = 1536
    return pl.pallas_call(
        _adaln_kernel,
        grid=(depth, n // bn),
        in_specs=[
            pl.BlockSpec((rows, d), lambda l, j: (0, 0)),
            pl.BlockSpec((1, d, bn), lambda l, j: (l, 0, j)),
            pl.BlockSpec((1, 1, bn), lambda l, j: (l, 0, j)),
        ],
        out_specs=pl.BlockSpec((1, rows, bn), lambda l, j: (l, 0, j)),
        out_shape=jax.ShapeDtypeStruct((depth, rows, n), F32),
        compiler_params=_cparams("parallel", "parallel"),
        name="adaln",
    )(cond, ada_w, ada_b.reshape(depth, 1, n))


def _mod_spec(k, n_ctx_tiles, d):
    return pl.BlockSpec((None, None, None, 1, d),
                        lambda b, i, *_: (b, jnp.minimum(i // n_ctx_tiles, 1), k, 0, 0))


def _norm_mod(x, g, shift, scale):
    ms = jnp.mean(x * x, axis=-1, keepdims=True)
    return x * lax.rsqrt(ms + EPS) * g * (1.0 + scale) + shift


def _norm_mod_router_kernel(x_ref, g_ref, sh_ref, sc_ref, r_ref, h_ref, cw_ref):
    d = x_ref.shape[2]
    h = _norm_mod(x_ref[0], g_ref[...], sh_ref[...], sc_ref[...])
    h_ref[0, :, :d] = h.astype(BF16)
    logits = jnp.dot(h, r_ref[...], preferred_element_type=F32, precision=lax.Precision.HIGHEST)
    lane = lax.broadcasted_iota(jnp.int32, logits.shape, 1)
    logits = jnp.where(lane < N_EXPERTS, logits, -jnp.inf)
    m1 = jnp.max(logits, axis=-1, keepdims=True)
    i1 = jnp.min(jnp.where(logits == m1, lane, LANES), axis=-1, keepdims=True)
    rest = jnp.where(lane == i1, -jnp.inf, logits)
    m2 = jnp.max(rest, axis=-1, keepdims=True)
    i2 = jnp.min(jnp.where(rest == m2, lane, LANES), axis=-1, keepdims=True)
    e2 = jnp.exp(m2 - m1)
    den = 1.0 + e2
    g1, g2 = 1.0 / den, e2 / den
    cw_ref[0] = jnp.where(lane == 0, i1.astype(F32), jnp.where(lane == 1, i2.astype(F32),
                          jnp.where(lane == 2, g1, jnp.where(lane == 3, g2, 0.0))))
    g1_hi, g2_hi = g1.astype(BF16).astype(F32), g2.astype(BF16).astype(F32)
    ext = jnp.where(lane == ROUTE_G1, g1_hi, jnp.where(lane == ROUTE_G1 + 1, g1 - g1_hi,
          jnp.where(lane == ROUTE_G2, g2_hi, jnp.where(lane == ROUTE_G2 + 1, g2 - g2_hi,
          jnp.where(lane == ROUTE_E1, i1.astype(F32), jnp.where(lane == ROUTE_E2, i2.astype(F32), 0.0))))))
    h_ref[0, :, d:] = ext.astype(BF16)


ROUTE_G1, ROUTE_G2, ROUTE_E1, ROUTE_E2 = 0, 2, 4, 5


def norm_mod_router(xa, gain, mods, k_shift, k_scale, n_ctx, router):
    b, ta, d = xa.shape
    bm = ROW_TILE
    nct = n_ctx // bm
    rpad = jnp.zeros((d, LANES), F32).at[:, :N_EXPERTS].set(router)
    return pl.pallas_call(
        _norm_mod_router_kernel, grid=(b, ta // bm),
        in_specs=[
            pl.BlockSpec((1, bm, d), lambda b_, i: (b_, i, 0)),
            pl.BlockSpec((1, d), lambda b_, i: (0, 0)),
            _mod_spec(k_shift, nct, d),
            _mod_spec(k_scale, nct, d),
            pl.BlockSpec((d, LANES), lambda b_, i: (0, 0)),
        ],
        out_specs=[pl.BlockSpec((1, bm, d + LANES), lambda b_, i: (b_, i, 0)),
                   pl.BlockSpec((1, bm, LANES), lambda b_, i: (b_, i, 0))],
        out_shape=[jax.ShapeDtypeStruct((b, ta, d + LANES), BF16), jax.ShapeDtypeStruct((b, ta, LANES), F32)],
        compiler_params=_cparams("parallel", "parallel"), name="norm_mod_router",
    )(xa, gain.reshape(1, d), mods, mods, rpad)


def _proj_kernel(*refs, rope_blocks, rope_half, prenorm, norm_blocks):
    if prenorm:
        x_ref, g_ref, sh_ref, sc_ref = refs[:4]
        h = _norm_mod(x_ref[0], g_ref[...], sh_ref[...], sc_ref[...]).astype(BF16)
        refs = refs[4:]
    else:
        h = refs[0][0]
        refs = refs[1:]
    y = _dot(h, refs[0][...])
    n = y.shape[1]
    s_hi, s_lo = _split_bf16(y * y)
    if norm_blocks is None:
        _, e_ref, et_ref, igs_ref, u_ref, gain_ref, c_ref, s_ref, o_ref = refs
        ss = _dot(s_hi, e_ref[...]) + _dot(s_lo, e_ref[...])
        inv = lax.rsqrt(ss * igs_ref[...] + EPS)
        i_hi, i_lo = _split_bf16(inv)
        fac = _dot(i_hi, et_ref[...]) + _dot(i_lo, et_ref[...]) + u_ref[...]
        z = y * (fac * gain_ref[...])
    else:
        _, j_ref, igs_ref, u_ref, gain_ref, c_ref, s_ref, o_ref = refs
        zs = []
        for blk in range(n // MXU_WIDTH):
            sl = slice(blk * MXU_WIDTH, (blk + 1) * MXU_WIDTH)
            if blk in norm_blocks:
                ss = _dot(s_hi[:, sl], j_ref[blk]) + _dot(s_lo[:, sl], j_ref[blk])
                u = u_ref[:, sl]
                fac = lax.rsqrt(ss * igs_ref[:, sl] + EPS) * (1.0 - u) + u
                zs.append(y[:, sl] * (fac * gain_ref[:, sl]))
            else:
                zs.append(y[:, sl] * gain_ref[:, sl])
        z = jnp.concatenate(zs, axis=1)
    cos = c_ref[...]
    sin = s_ref[...]
    if rope_half == LANES // 2:
        first = None
    else:
        lane = lax.broadcasted_iota(jnp.int32, cos.shape, 1)
        first = (lane % (2 * rope_half)) < rope_half
    for blk in range(n // LANES):
        zb = z[:, blk * LANES:(blk + 1) * LANES]
        if blk in rope_blocks:
            if first is None:
                rot = pltpu.roll(zb, LANES // 2, 1)
            else:
                rot = jnp.where(first, pltpu.roll(zb, LANES - rope_half, 1), pltpu.roll(zb, rope_half, 1))
            zb = zb * cos + rot * sin
        o_ref[0, :, blk * LANES:(blk + 1) * LANES] = zb.astype(BF16)


def proj(h, kin_block, w, groups, gain, rope_tabs, rope_blocks, rope_half, name, prenorm=None):
    b, ta, _ = h.shape
    k, n = w.shape
    bm = ROW_TILE
    mw = MXU_WIDTH
    u = np.ones((1, n), np.float32)
    for start, size in groups:
        u[0, start:start + size] = 0.0
    local = n % mw == 0 and all(start // mw == (start + size - 1) // mw for start, size in groups)
    if local:
        jm = np.zeros((n // mw, mw, mw), np.float32)
        igs = np.zeros((1, n), np.float32)
        for start, size in groups:
            blk, o = divmod(start, mw)
            jm[blk, o:o + size, o:o + size] = 1.0
            igs[0, start:start + size] = 1.0 / size
        norm_blocks = frozenset(start // mw for start, _ in groups)
        norm_args = [jnp.asarray(jm, BF16), jnp.asarray(igs)]
        norm_specs = [pl.BlockSpec((n // mw, mw, mw), lambda b_, i: (0, 0, 0)),
                      pl.BlockSpec((1, n), lambda b_, i: (0, 0))]
    else:
        e = np.zeros((n, LANES), np.float32)
        igs = np.zeros((1, LANES), np.float32)
        for gi, (start, size) in enumerate(groups):
            e[start:start + size, gi] = 1.0
            igs[0, gi] = 1.0 / size
        norm_blocks = None
        norm_args = [jnp.asarray(e, BF16), jnp.asarray(e.T, BF16), jnp.asarray(igs)]
        norm_specs = [pl.BlockSpec((n, LANES), lambda b_, i: (0, 0)),
                      pl.BlockSpec((LANES, n), lambda b_, i: (0, 0)),
                      pl.BlockSpec((1, LANES), lambda b_, i: (0, 0))]
    cos, sin = rope_tabs
    kern = functools.partial(_proj_kernel, rope_blocks=frozenset(rope_blocks), rope_half=rope_half,
                             prenorm=prenorm is not None, norm_blocks=norm_blocks)
    lead_specs = [pl.BlockSpec((1, bm, k), lambda b_, i: (b_, i, kin_block))]
    lead_args = [h]
    if prenorm is not None:
        ngain, mods, k_shift, k_scale, n_ctx = prenorm
        nct = n_ctx // bm
        lead_specs += [pl.BlockSpec((1, k), lambda b_, i: (0, 0)), _mod_spec(k_shift, nct, k),
                       _mod_spec(k_scale, nct, k)]
        lead_args += [ngain.reshape(1, k), mods, mods]
    return pl.pallas_call(
        kern,
        grid=(b, ta // bm),
        in_specs=lead_specs + [pl.BlockSpec((k, n), lambda b_, i: (0, 0))] + norm_specs + [
            pl.BlockSpec((1, n), lambda b_, i: (0, 0)),
            pl.BlockSpec((1, n), lambda b_, i: (0, 0)),
            pl.BlockSpec((bm, LANES), lambda b_, i: (i, 0)),
            pl.BlockSpec((bm, LANES), lambda b_, i: (i, 0)),
        ],
        out_specs=pl.BlockSpec((1, bm, n), lambda b_, i: (b_, i, 0)),
        out_shape=jax.ShapeDtypeStruct((b, ta, n), BF16),
        compiler_params=_cparams("parallel", "parallel"),
        name=name,
    )(*lead_args, w, *norm_args, jnp.asarray(u), gain.reshape(1, n).astype(F32), cos, sin)


LOG2E = math.log2(math.e)


def _softmax_pv(s, v):
    m = jnp.max(s, axis=-1, keepdims=True)
    p = jnp.exp2(s - m)
    l = jnp.sum(p, axis=-1, keepdims=True)
    return _dot(p.astype(BF16), v) / l


def _gqa_attn_kernel(q_ref, k_ref, v_ref, o_ref, *, n_ctx, group, hd):
    i = pl.program_id(2)

    def attend(nk):
        k = k_ref[0, :nk, :]
        v = v_ref[0, :nk, :]
        for g in range(group):
            q = q_ref[0, :, g * hd:(g + 1) * hd]
            o = _softmax_pv(_dot_nt(q, k), v)
            o_ref[0, :, g * hd:(g + 1) * hd] = o.astype(BF16)

    nct = n_ctx // ROW_TILE

    @pl.when(i < nct)
    def _():
        attend(n_ctx)

    @pl.when(i >= nct)
    def _():
        attend(k_ref.shape[1])


def gqa_attention(qkv, n_ctx):
    b, ta, _ = qkv.shape
    hd, group, kvh = GQA_HEAD_DIM, GQA_HEADS // GQA_KV_HEADS, GQA_KV_HEADS
    bq = ROW_TILE
    qw = group * hd
    kb = GQA_HEADS
    vb = GQA_HEADS + kvh
    kern = functools.partial(_gqa_attn_kernel, n_ctx=n_ctx, group=group, hd=hd)
    return pl.pallas_call(
        kern,
        grid=(b, kvh, ta // bq),
        in_specs=[
            pl.BlockSpec((1, bq, qw), lambda b_, h, i: (b_, i, h)),
            pl.BlockSpec((1, ta, hd), lambda b_, h, i: (b_, 0, kb + h)),
            pl.BlockSpec((1, ta, hd), lambda b_, h, i: (b_, 0, vb + h)),
        ],
        out_specs=pl.BlockSpec((1, bq, qw), lambda b_, h, i: (b_, i, h)),
        out_shape=jax.ShapeDtypeStruct((b, ta, GQA_HEADS * hd), BF16),
        compiler_params=_cparams("parallel", "parallel", "arbitrary"),
        name="gqa_attn",
    )(qkv, qkv, qkv)


def _mla_attn_kernel(q_ref, kn_ref, kp_ref, v_ref, o_ref, kcat_ref, *, n_ctx):
    i = pl.program_id(2)

    @pl.when(i == 0)
    def _():
        kcat_ref[:, :LANES] = kn_ref[0]
        kcat_ref[:, LANES:] = kp_ref[0]

    def attend(nk):
        s = _dot_nt(q_ref[0], kcat_ref[:nk, :])
        o_ref[0] = _softmax_pv(s, v_ref[0, :nk, :]).astype(BF16)

    nct = n_ctx // ROW_TILE

    @pl.when(i < nct)
    def _():
        attend(n_ctx)

    @pl.when(i >= nct)
    def _():
        attend(kn_ref.shape[1])


def mla_attention(q, kv, dn, kpe_block, n_ctx):
    b, ta, _ = q.shape
    bq = ROW_TILE
    hh = MLA_HEADS
    kern = functools.partial(_mla_attn_kernel, n_ctx=n_ctx)
    return pl.pallas_call(
        kern,
        grid=(b, hh, ta // bq),
        in_specs=[
            pl.BlockSpec((1, bq, 2 * LANES), lambda b_, h, i: (b_, i, h)),
            pl.BlockSpec((1, ta, LANES), lambda b_, h, i: (b_, 0, 2 * h)),
            pl.BlockSpec((1, ta, LANES), lambda b_, h, i: (b_, 0, kpe_block)),
            pl.BlockSpec((1, ta, LANES), lambda b_, h, i: (b_, 0, 2 * h + 1)),
        ],
        out_specs=pl.BlockSpec((1, bq, LANES), lambda b_, h, i: (b_, i, h)),
        out_shape=jax.ShapeDtypeStruct((b, ta, hh * MLA_V), BF16),
        scratch_shapes=[pltpu.VMEM((ta, 2 * LANES), BF16)],
        compiler_params=_cparams("parallel", "parallel", "arbitrary"),
        name="mla_attn",
    )(q, kv, dn, kv)


def _half_masks(shape):
    lane = lax.broadcasted_iota(jnp.int32, shape, 1)
    return lane < (LANES // 2)


def _diff_attn_kernel(q_ref, k_ref, v_ref, lam_ref, sub_ref, o_ref, *, n_ctx, lam_init):
    i = pl.program_id(2)
    lp = lam_ref[...]
    lam = (jnp.exp(jnp.sum(lp[0:1] * lp[1:2], axis=-1, keepdims=True))
           - jnp.exp(jnp.sum(lp[2:3] * lp[3:4], axis=-1, keepdims=True)) + lam_init)

    def attend(nk):
        k = k_ref[0, :nk, :]
        v = v_ref[0, :nk, :]

        def probs(s, weight):
            m = jnp.max(s, axis=-1, keepdims=True)
            e = jnp.exp2(s - m)
            return e * (weight / jnp.sum(e, axis=-1, keepdims=True))

        q = q_ref[0]
        lo = _half_masks(q.shape)
        zero = jnp.zeros_like(q)
        p = probs(_dot_nt(jnp.where(lo, q, zero), k), 1.0) - probs(_dot_nt(jnp.where(lo, zero, q), k), lam)
        o = _dot(p.astype(BF16), v)
        ms = jnp.mean(o * o, axis=-1, keepdims=True)
        o_ref[0] = (o * lax.rsqrt(ms + EPS) * sub_ref[...] * (1.0 - lam_init)).astype(BF16)

    nct = n_ctx // ROW_TILE

    @pl.when(i < nct)
    def _():
        attend(n_ctx)

    @pl.when(i >= nct)
    def _():
        attend(k_ref.shape[1])


def diff_attention(qkv, lam_p, subln, lam_init, n_ctx):
    b, ta, _ = qkv.shape
    bq = ROW_TILE
    hh = DIFF_HEADS
    kern = functools.partial(_diff_attn_kernel, n_ctx=n_ctx, lam_init=lam_init)
    return pl.pallas_call(
        kern,
        grid=(b, hh, ta // bq),
        in_specs=[
            pl.BlockSpec((1, bq, LANES), lambda b_, h, i: (b_, i, h)),
            pl.BlockSpec((1, ta, LANES), lambda b_, h, i: (b_, 0, hh + h)),
            pl.BlockSpec((1, ta, LANES), lambda b_, h, i: (b_, 0, 2 * hh + h)),
            pl.BlockSpec((4, DIFF_HEAD_DIM), lambda b_, h, i: (0, 0)),
            pl.BlockSpec((1, LANES), lambda b_, h, i: (0, 0)),
        ],
        out_specs=pl.BlockSpec((1, bq, LANES), lambda b_, h, i: (b_, i, h)),
        out_shape=jax.ShapeDtypeStruct((b, ta, hh * 2 * DIFF_HEAD_DIM), BF16),
        compiler_params=_cparams("parallel", "parallel", "arbitrary"),
        name="diff_attn",
    )(qkv, qkv, qkv, lam_p.astype(F32), subln.reshape(1, LANES).astype(F32))


def _win_attn_kernel(sink_ref, q_ref, k_ref, v_ref, o_ref, *, n_ctx, seq, n_pairs):
    i = pl.program_id(1)
    bq = q_ref.shape[1]
    band = bq + 2 * WINDOW
    nct = n_ctx // bq

    def head_out(qm, kv_idx, pair, blocks):
        sk = sink_ref[kv_idx * n_pairs + pair]
        ss = []
        m = None
        for k, _, mask in blocks:
            s = _dot_nt(qm, k)
            if mask is not None:
                s = jnp.where(mask, s, NEG_INF)
            ss.append(s)
            bm_ = jnp.max(s, axis=-1, keepdims=True)
            m = bm_ if m is None else jnp.maximum(m, bm_)
        m = jnp.maximum(m, sk)
        l = jnp.exp2(sk - m)
        o = None
        for s, (_, v, _) in zip(ss, blocks):
            p = jnp.exp2(s - m)
            l = l + jnp.sum(p, axis=-1, keepdims=True)
            pv = _dot(p.astype(BF16), v)
            o = pv if o is None else o + pv
        return o / l

    def run(blocks):
        for pair in range(n_pairs):
            q = q_ref[0, :, pair * LANES:(pair + 1) * LANES]
            lo = _half_masks(q.shape)
            zero = jnp.zeros_like(q)
            o0 = head_out(jnp.where(lo, q, zero), 0, pair, blocks)
            o1 = head_out(jnp.where(lo, zero, q), 1, pair, blocks)
            o_ref[0, :, pair * LANES:(pair + 1) * LANES] = jnp.where(lo, o0, o1).astype(BF16)

    @pl.when(i < nct)
    def _():
        run([(k_ref[0, :n_ctx, :], v_ref[0, :n_ctx, :], None)])

    @pl.when(i >= nct)
    def _():
        q0 = (i - nct) * bq
        start = jnp.clip(q0 - WINDOW, 0, seq - band)
        start = pl.multiple_of(start, WINDOW)
        qpos = q0 + lax.broadcasted_iota(jnp.int32, (bq, band), 0)
        kpos = start + lax.broadcasted_iota(jnp.int32, (bq, band), 1)
        mask = jnp.abs(qpos - kpos) <= WINDOW
        kb = k_ref[0, pl.ds(n_ctx + start, band), :]
        vb = v_ref[0, pl.ds(n_ctx + start, band), :]
        run([(k_ref[0, :n_ctx, :], v_ref[0, :n_ctx, :], None), (kb, vb, mask)])


def win_attention(qkv, sink, n_ctx):
    b, ta, _ = qkv.shape
    bq = ROW_TILE
    n_pairs = WIN_HEADS // WIN_KV_HEADS
    qw = n_pairs * LANES
    kern = functools.partial(_win_attn_kernel, n_ctx=n_ctx, seq=ta - n_ctx, n_pairs=n_pairs)
    return pl.pallas_call(
        kern,
        grid_spec=pltpu.PrefetchScalarGridSpec(
            num_scalar_prefetch=1,
            grid=(b, ta // bq),
            in_specs=[
                pl.BlockSpec((1, bq, qw), lambda b_, i, s: (b_, i, 0)),
                pl.BlockSpec((1, ta, LANES), lambda b_, i, s: (b_, 0, n_pairs)),
                pl.BlockSpec((1, ta, LANES), lambda b_, i, s: (b_, 0, n_pairs + 1)),
            ],
            out_specs=pl.BlockSpec((1, bq, qw), lambda b_, i, s: (b_, i, 0)),
        ),
        out_shape=jax.ShapeDtypeStruct((b, ta, qw), BF16),
        compiler_params=_cparams("parallel", "arbitrary"),
        name="win_attn",
    )(sink.astype(F32), qkv, qkv, qkv)


def _out_proj_kernel(o_ref, w_ref, x_ref, gc_ref, gl_ref, y_ref, *, n_ctx):
    bm = x_ref.shape[1]
    row = pl.program_id(1) * bm + lax.broadcasted_iota(jnp.int32, (bm, 1), 0)
    gate = jnp.where(row < n_ctx, gc_ref[...], gl_ref[...])
    y_ref[0] = x_ref[0] + gate * _dot(o_ref[0], w_ref[...])


def out_proj_residual(o, w, xa, mods, k_gate, n_ctx):
    b, ta, d = xa.shape
    kdim = o.shape[2]
    bm = FFN_ROW_TILE
    return pl.pallas_call(
        functools.partial(_out_proj_kernel, n_ctx=n_ctx),
        grid=(b, ta // bm),
        in_specs=[
            pl.BlockSpec((1, bm, kdim), lambda b_, i: (b_, i, 0)),
            pl.BlockSpec((kdim, d), lambda b_, i: (0, 0)),
            pl.BlockSpec((1, bm, d), lambda b_, i: (b_, i, 0)),
            pl.BlockSpec((None, None, None, 1, d), lambda b_, i: (b_, 0, k_gate, 0, 0)),
            pl.BlockSpec((None, None, None, 1, d), lambda b_, i: (b_, 1, k_gate, 0, 0)),
        ],
        out_specs=pl.BlockSpec((1, bm, d), lambda b_, i: (b_, i, 0)),
        out_shape=jax.ShapeDtypeStruct(xa.shape, F32),
        input_output_aliases={2: 0},
        compiler_params=_cparams("parallel", "parallel"),
        name="out_proj",
    )(o, w, xa, mods, mods)


def _ffn_kernel(ng_ref, wg_ref, wu_ref, w2_ref, x_ref, mc_ref, ml_ref, y_ref, acc_ref, h_ref, *,
                n_ctx, k_shift, k_scale, k_gate):
    i = pl.program_id(1)
    f = pl.program_id(2)
    bm = acc_ref.shape[0]

    def row_mod(k):
        row = i * bm + lax.broadcasted_iota(jnp.int32, (bm, 1), 0)
        return jnp.where(row < n_ctx, mc_ref[k], ml_ref[k])

    @pl.when(f == 0)
    def _():
        acc_ref[...] = jnp.zeros_like(acc_ref)
        h_ref[...] = _norm_mod(x_ref[0], ng_ref[...], row_mod(k_shift), row_mod(k_scale)).astype(BF16)

    h = h_ref[...]
    g = _dot(h, wg_ref[...])
    u = _dot(h, wu_ref[...])
    act = (g * (1.0 / (1.0 + jnp.exp(-g))) * u).astype(BF16)
    acc_ref[...] += _dot(act, w2_ref[...])

    @pl.when(f == pl.num_programs(2) - 1)
    def _():
        y_ref[0] = x_ref[0] + row_mod(k_gate) * acc_ref[...]


def ffn_residual(norm_gain, w13, w2, xa, mods, k_shift, k_scale, k_gate, n_ctx):
    b, ta, d = xa.shape
    ff = w2.shape[0]
    bm, fk = FFN_ROW_TILE, FF_CHUNK
    nf = ff // fk
    kern = functools.partial(_ffn_kernel, n_ctx=n_ctx, k_shift=k_shift, k_scale=k_scale, k_gate=k_gate)
    in_specs = [
        pl.BlockSpec((1, d), lambda b_, i, f: (0, 0)),
        pl.BlockSpec((d, fk), lambda b_, i, f: (0, f)),
        pl.BlockSpec((d, fk), lambda b_, i, f: (0, nf + f)),
        pl.BlockSpec((fk, d), lambda b_, i, f: (f, 0)),
        pl.BlockSpec((1, bm, d), lambda b_, i, f: (b_, i, 0)),
        pl.BlockSpec((None, None, 6, 1, d), lambda b_, i, f: (b_, 0, 0, 0, 0)),
        pl.BlockSpec((None, None, 6, 1, d), lambda b_, i, f: (b_, 1, 0, 0, 0)),
    ]
    return pl.pallas_call(
        kern,
        grid=(b, ta // bm, nf),
        in_specs=in_specs,
        out_specs=pl.BlockSpec((1, bm, d), lambda b_, i, f: (b_, i, 0)),
        out_shape=jax.ShapeDtypeStruct(xa.shape, F32),
        scratch_shapes=[pltpu.VMEM((bm, d), F32), pltpu.VMEM((bm, d), BF16)],
        input_output_aliases={4: 0},
        compiler_params=_cparams("parallel", "parallel", "arbitrary"),
        name="ffn",
    )(norm_gain.reshape(1, d), w13, w13, w2, xa, mods, mods)


MOE_SLOT_TILE = 512
MOE_GATHER_CHUNK = 256
MOE_GATHER_SUBTILES = 2
MOE_GATHER_FANIN = 6
MOE_COMBINE_CHUNK = 768
_VALID, _FIRST, _LAST = 1, 2, 4


def _one_hot_bf16(cond):
    return jnp.where(cond, 1.0, 0.0).astype(BF16)


def _moe_gather_kernel(tile_ref, base_ref, hi_ref, flag_ref, *refs, nsub, fan):
    nin = nsub * fan
    p1_refs, p2_refs, h_refs, o_ref = refs[:nin], refs[nin:2 * nin], refs[2 * nin:3 * nin], refs[3 * nin]
    k = pl.program_id(0)
    fl = flag_ref[k]

    def picked():
        sbm, ch = o_ref.shape[0] // nsub, h_refs[0].shape[0]
        rows = []
        for s in range(nsub):
            slot = tile_ref[k] * (sbm * nsub) + s * sbm + lax.broadcasted_iota(jnp.int32, (sbm, ch), 0)
            acc = None
            for j in range(fan):
                r = s * fan + j
                slot_j = jnp.where(base_ref[k * nsub + s] + j <= hi_ref[k * nsub + s], slot, -2)
                hit = (p1_refs[r][...] == slot_j) | (p2_refs[r][...] == slot_j)
                part = _dot(_one_hot_bf16(hit), h_refs[r][...])
                acc = part if acc is None else acc + part
            rows.append(acc.astype(BF16))
        return jnp.concatenate(rows, axis=0)

    @pl.when((fl & _FIRST) != 0)
    def _():
        o_ref[...] = picked()

    @pl.when(((fl & _VALID) != 0) & ((fl & _FIRST) == 0))
    def _():
        o_ref[...] += picked()


def _moe_ffn_kernel(te_ref, nu_ref, xs_ref, wg_ref, wu_ref, w2_ref, y_ref, acc_ref):
    i = pl.program_id(0)
    f = pl.program_id(1)
    d = acc_ref.shape[1]

    @pl.when(i < nu_ref[0])
    def _():
        @pl.when(f == 0)
        def _():
            acc_ref[...] = jnp.zeros_like(acc_ref)

        xs = xs_ref[:, :d]
        g = _dot(xs, wg_ref[...])
        u = _dot(xs, wu_ref[...])
        act = (g * (1.0 / (1.0 + jnp.exp(-g))) * u).astype(BF16)
        acc_ref[...] += _dot(act, w2_ref[...])

        @pl.when(f == pl.num_programs(1) - 1)
        def _():
            rt = xs_ref[:, d:].astype(F32)
            first = rt[:, ROUTE_E1:ROUTE_E1 + 1] == te_ref[i].astype(F32)
            gate = jnp.where(first, rt[:, ROUTE_G1:ROUTE_G1 + 1] + rt[:, ROUTE_G1 + 1:ROUTE_G1 + 2],
                             rt[:, ROUTE_G2:ROUTE_G2 + 1] + rt[:, ROUTE_G2 + 1:ROUTE_G2 + 2])
            y_ref[...] = (acc_ref[...] * gate).astype(BF16)


def _moe_combine_kernel(chunk_ref, tile_ref, flag_ref, p1_ref, p2_ref, y_ref, x_ref, gc_ref, gl_ref,
                        o_ref, acc_ref, *, n_ctx, chunks_per_batch):
    k = pl.program_id(0)
    fl = flag_ref[k]
    ch, bm = acc_ref.shape[0], y_ref.shape[0]

    @pl.when((fl & _FIRST) != 0)
    def _():
        acc_ref[...] = jnp.zeros_like(acc_ref)

    @pl.when((fl & _VALID) != 0)
    def _():
        slot = tile_ref[k] * bm + lax.broadcasted_iota(jnp.int32, (ch, bm), 1)
        w = _one_hot_bf16((p1_ref[...] == slot) | (p2_ref[...] == slot))
        acc_ref[...] += _dot(w, y_ref[...])

    @pl.when((fl & _LAST) != 0)
    def _():
        row = (chunk_ref[k] % chunks_per_batch) * ch + lax.broadcasted_iota(jnp.int32, (ch, 1), 0)
        gate = jnp.where(row < n_ctx, gc_ref[...], gl_ref[...])
        o_ref[...] = x_ref[...] + gate * acc_ref[...]


def _expand_items(counts, n_items):
    off = jnp.cumsum(counts)
    total = off[-1]
    k = jnp.minimum(jnp.arange(n_items, dtype=jnp.int32), total - 1)
    grp = jnp.minimum(jnp.sum(off[None, :] <= k[:, None], axis=1).astype(jnp.int32), counts.shape[0] - 1)
    local = k - (off[grp] - counts[grp])
    valid = jnp.arange(n_items, dtype=jnp.int32) < total
    return grp, local, valid


def moe_residual(h, route, w13, w2, xa, mods, k_gate, n_ctx, ctx_active):
    b, ta, d = xa.shape
    t = b * ta
    ne, ff = w2.shape[0], w2.shape[1]
    bm, gch, cch, fk = MOE_SLOT_TILE, MOE_GATHER_CHUNK, MOE_COMBINE_CHUNK, FF_CHUNK
    nf = ff // fk
    nt = (2 * t) // bm + ne
    ns = nt * bm
    i32 = jnp.int32

    dx = h.shape[2]
    r = route.reshape(t, LANES)
    e1, e2 = r[:, 0].astype(i32), r[:, 1].astype(i32)
    tok = jnp.arange(t, dtype=i32)
    active = jnp.ones((t,), bool) if ctx_active else (tok % ta) >= n_ctx

    eid = jnp.arange(ne, dtype=i32)[:, None]
    member = ((e1[None] == eid) | (e2[None] == eid)) & active[None]
    csum = jnp.cumsum(member.astype(i32), axis=1)
    cap = (csum[:, -1] + bm - 1) // bm * bm
    end = jnp.cumsum(cap)
    start = end - cap
    n_used = (end[-1] // bm).astype(i32)

    def slot_of(e_sel):
        rank = jnp.take_along_axis(csum, e_sel[None], axis=0)[0] - 1
        return jnp.where(active, start[e_sel] + rank, -1)

    pos1, pos2 = slot_of(e1), slot_of(e2)
    tile_ids = jnp.arange(nt, dtype=i32)
    tile_expert = jnp.minimum(jnp.sum(end[None] <= (tile_ids * bm)[:, None], axis=1), ne - 1).astype(i32)

    nsub, fan = MOE_GATHER_SUBTILES, MOE_GATHER_FANIN
    sbm = bm // nsub
    ngc = t // gch
    sub_ids = jnp.arange(nt * nsub, dtype=i32)
    sub_expert = tile_expert[sub_ids // nsub]
    members = csum[sub_expert, -1]
    r0 = sub_ids * sbm - start[sub_expert]
    r1 = jnp.minimum(r0 + sbm, members) - 1
    through_g = csum[:, gch - 1::gch][sub_expert]
    c_lo = jnp.minimum(jnp.sum(through_g <= r0[:, None], axis=1), ngc - 1).astype(i32)
    c_hi = jnp.minimum(jnp.sum(through_g <= r1[:, None], axis=1), ngc - 1).astype(i32)
    live = (sub_ids // nsub < n_used) & (r0 < members)
    n_ch = jnp.where(live, c_hi - c_lo + 1, 0).reshape(nt, nsub)
    c_lo = c_lo.reshape(nt, nsub)
    c_hi = jnp.where(live, c_hi, -1).reshape(nt, nsub)
    n_rd = (jnp.max(n_ch, axis=1) + fan - 1) // fan
    ni_g = (ne * ngc + nt) // fan + nt + 1
    g_tile, g_round, g_valid = _expand_items(n_rd, ni_g)
    g_base = (c_lo[g_tile] + g_round[:, None] * fan).reshape(-1)
    g_hi = c_hi[g_tile].reshape(-1)
    g_flag = g_valid * (_VALID + _FIRST * (g_round == 0))

    def chunk_of(s, j):
        return lambda k, tl, bs, hi, fl: jnp.clip(bs[k * nsub + s] + j, 0, jnp.maximum(hi[k * nsub + s], 0))

    pairs = [(s, j) for s in range(nsub) for j in range(fan)]
    row_specs = [pl.BlockSpec((None, 1, gch), lambda k, *a, c=chunk_of(s, j): (c(k, *a), 0, 0)) for s, j in pairs]
    h_specs = [pl.BlockSpec((gch, dx), lambda k, *a, c=chunk_of(s, j): (c(k, *a), 0)) for s, j in pairs]
    pos1_rows, pos2_rows, h_rows = pos1.reshape(ngc, 1, gch), pos2.reshape(ngc, 1, gch), h.reshape(t, dx)
    nin = len(pairs)
    xs = pl.pallas_call(
        functools.partial(_moe_gather_kernel, nsub=nsub, fan=fan),
        grid_spec=pltpu.PrefetchScalarGridSpec(
            num_scalar_prefetch=4,
            grid=(ni_g,),
            in_specs=row_specs + row_specs + h_specs,
            out_specs=pl.BlockSpec((bm, dx), lambda k, tl, bs, hi, fl: (tl[k], 0)),
        ),
        out_shape=jax.ShapeDtypeStruct((ns, dx), BF16),
        compiler_params=_cparams("arbitrary"),
        name="moe_gather",
    )(g_tile, g_base, g_hi, g_flag.astype(i32), *([pos1_rows] * nin), *([pos2_rows] * nin), *([h_rows] * nin))

    def last_used(i, nu):
        return jnp.minimum(i, nu[0] - 1)

    def f_eff(i, f, nu):
        return jnp.where(i < nu[0], f, nf - 1)

    y = pl.pallas_call(
        _moe_ffn_kernel,
        grid_spec=pltpu.PrefetchScalarGridSpec(
            num_scalar_prefetch=2,
            grid=(nt, nf),
            in_specs=[
                pl.BlockSpec((bm, dx), lambda i, f, te, nu: (last_used(i, nu), 0)),
                pl.BlockSpec((None, d, fk), lambda i, f, te, nu: (te[i], 0, f_eff(i, f, nu))),
                pl.BlockSpec((None, d, fk), lambda i, f, te, nu: (te[i], 0, nf + f_eff(i, f, nu))),
                pl.BlockSpec((None, fk, d), lambda i, f, te, nu: (te[i], f_eff(i, f, nu), 0)),
            ],
            out_specs=pl.BlockSpec((bm, d), lambda i, f, te, nu: (last_used(i, nu), 0)),
            scratch_shapes=[pltpu.VMEM((bm, d), F32)],
        ),
        out_shape=jax.ShapeDtypeStruct((ns, d), BF16),
        compiler_params=_cparams("arbitrary", "arbitrary"),
        name="moe_ffn",
    )(tile_expert, n_used.reshape(1), xs, w13, w13, w2)

    nc = t // cch
    cs0 = jnp.concatenate([jnp.zeros((ne, 1), i32), csum], axis=1)[:, ::cch]
    before, through = cs0[:, :-1].T, cs0[:, 1:].T
    t_lo = (start[None] + before) // bm
    t_hi = (start[None] + through - 1) // bm
    n_tl = jnp.where(through > before, t_hi - t_lo + 1, 0).reshape(-1)
    ni_c = ne * nc + nt
    c_pair, c_local, c_valid = _expand_items(n_tl, ni_c)
    c_chunk = c_pair // ne
    c_tile = t_lo.reshape(-1)[c_pair] + c_local
    off = jnp.cumsum(n_tl)
    k_eff = jnp.minimum(jnp.arange(ni_c, dtype=i32), off[-1] - 1)
    chunk_first = (off - n_tl)[c_chunk * ne]
    chunk_last = off[c_chunk * ne + ne - 1] - 1
    c_flag = c_valid * (_VALID + _FIRST * (k_eff == chunk_first) + _LAST * (k_eff == chunk_last))

    cpb = ta // cch
    kern = functools.partial(_moe_combine_kernel, n_ctx=n_ctx, chunks_per_batch=cpb)
    out = pl.pallas_call(
        kern,
        grid_spec=pltpu.PrefetchScalarGridSpec(
            num_scalar_prefetch=3,
            grid=(ni_c,),
            in_specs=[
                pl.BlockSpec((cch, 1), lambda k, cn, tl, fl: (cn[k], 0)),
                pl.BlockSpec((cch, 1), lambda k, cn, tl, fl: (cn[k], 0)),
                pl.BlockSpec((bm, d), lambda k, cn, tl, fl: (tl[k], 0)),
                pl.BlockSpec((cch, d), lambda k, cn, tl, fl: (cn[k], 0)),
                pl.BlockSpec((None, None, None, 1, d), lambda k, cn, tl, fl: (cn[k] // cpb, 0, k_gate, 0, 0)),
                pl.BlockSpec((None, None, None, 1, d), lambda k, cn, tl, fl: (cn[k] // cpb, 1, k_gate, 0, 0)),
            ],
            out_specs=pl.BlockSpec((cch, d), lambda k, cn, tl, fl: (cn[k], 0)),
            scratch_shapes=[pltpu.VMEM((cch, d), F32)],
        ),
        out_shape=jax.ShapeDtypeStruct((t, d), F32),
        input_output_aliases={6: 0},
        compiler_params=_cparams("arbitrary"),
        name="moe_combine",
    )(c_chunk, c_tile, c_flag.astype(i32), pos1.reshape(t, 1), pos2.reshape(t, 1), y, xa.reshape(t, d), mods, mods)
    return out.reshape(b, ta, d)


def _rope_tables(seq, n_ctx, rot_dim):
    rows = seq // GRID_W
    quarter = rot_dim // 4
    inv_freq = ROPE_THETA ** (-jnp.arange(quarter, dtype=F32) / quarter)
    row = jnp.repeat(jnp.arange(rows, dtype=F32), GRID_W)
    col = jnp.tile(jnp.arange(GRID_W, dtype=F32), rows)
    ang = jnp.concatenate([row[:, None] * inv_freq, col[:, None] * inv_freq], axis=-1)
    cos, sin = jnp.cos(ang), jnp.sin(ang)
    reps = LANES // rot_dim
    c = jnp.tile(jnp.concatenate([cos, cos], axis=-1), (1, reps))
    s = jnp.tile(jnp.concatenate([-sin, sin], axis=-1), (1, reps))
    c = jnp.concatenate([jnp.ones((n_ctx, LANES), F32), c], axis=0)
    s = jnp.concatenate([jnp.zeros((n_ctx, LANES), F32), s], axis=0)
    return c, s


def kernel(x, c, ctx, c_ctx, ada_w, ada_b, norm_mix, norm_ffn, gqa_wqkv, gqa_q_gain, gqa_k_gain, gqa_wo, mla_wdown, mla_qa_gain, mla_kva_gain, mla_wuq, mla_wukv, mla_q_gain, mla_k_gain, mla_wo, win_wqkv, win_q_gain, win_k_gain, win_sink, win_wo, diff_wqkv, diff_q_gain, diff_k_gain, diff_lambda, diff_subln, diff_wo, ffn_w13, ffn_w2, moe_router, moe_w13, moe_w2):
    b, seq, d = x.shape
    n_ctx = ctx.shape[1]
    depth = ada_w.shape[0]
    ones = lambda n: jnp.ones((n,), F32)
    zeros = lambda n: jnp.zeros((n,), F32)

    crows = -(-(b + 1) // 8) * 8
    cond = jnp.zeros((crows, d), F32).at[:b].set(c).at[b].set(c_ctx)
    mods_all = adaln(cond, ada_w, ada_b)

    xa = jnp.concatenate([ctx, x], axis=1)

    for i in range(depth):
        need_ctx = i < depth - 1
        j = i // 4
        kind = i % 4
        ml = mods_all[i, :b].reshape(b, 6, 1, d)
        mc = jnp.broadcast_to(mods_all[i, b].reshape(1, 6, 1, d), (b, 6, 1, d))
        mods = jnp.stack([mc, ml], axis=1)

        prenorm = (norm_mix[i], mods, 0, 1, n_ctx)
        if kind == 0:
            hd = GQA_HEAD_DIM
            scale = hd ** -0.5 * LOG2E
            nq, nk = GQA_HEADS * hd, GQA_KV_HEADS * hd
            groups = [(g * hd, hd) for g in range(GQA_HEADS + GQA_KV_HEADS)]
            gain = jnp.concatenate([jnp.tile(gqa_q_gain[j] * scale, GQA_HEADS),
                                    jnp.tile(gqa_k_gain[j], GQA_KV_HEADS), ones(nk)])
            qkv = proj(xa, 0, gqa_wqkv[j].astype(BF16), groups, gain, _rope_tables(seq, n_ctx, hd),
                       range((nq + nk) // LANES), hd // 2, "gqa_qkv", prenorm)
            o = gqa_attention(qkv, n_ctx)
            w_o = gqa_wo[j]
        elif kind == 1:
            scale = (MLA_NOPE + MLA_ROPE) ** -0.5 * LOG2E
            hh = MLA_HEADS
            wd = mla_wdown[j]
            pad = jnp.zeros((d, LANES - MLA_ROPE), F32)
            wd2 = jnp.concatenate([wd[:, :MLA_Q_LORA], wd[:, MLA_Q_LORA + MLA_KV_LORA:], pad,
                                   wd[:, MLA_Q_LORA:MLA_Q_LORA + MLA_KV_LORA]], axis=1)
            kpe0 = MLA_Q_LORA
            ckv0 = MLA_Q_LORA + LANES
            groups = [(0, MLA_Q_LORA), (kpe0, MLA_ROPE), (ckv0, MLA_KV_LORA)]
            gain = jnp.concatenate([mla_qa_gain[j], mla_k_gain[j][MLA_NOPE:], zeros(LANES - MLA_ROPE),
                                    mla_kva_gain[j]])
            rope = _rope_tables(seq, n_ctx, MLA_ROPE)
            dn = proj(xa, 0, wd2.astype(BF16), groups, gain, rope, [kpe0 // LANES], MLA_ROPE // 2, "mla_down",
                      prenorm)
            wq = mla_wuq[j].reshape(MLA_Q_LORA, hh, MLA_NOPE + MLA_ROPE)
            wq2 = jnp.pad(wq, ((0, 0), (0, 0), (0, LANES - MLA_ROPE))).reshape(MLA_Q_LORA, hh * 2 * LANES)
            groups = ([(g * 2 * LANES, MLA_NOPE) for g in range(hh)]
                      + [(g * 2 * LANES + MLA_NOPE, MLA_ROPE) for g in range(hh)])
            qg = jnp.concatenate([mla_q_gain[j] * scale, zeros(LANES - MLA_ROPE)])
            gain = jnp.tile(qg, hh)
            q = proj(dn, 0, wq2.astype(BF16), groups, gain, rope, range(1, 2 * hh, 2), MLA_ROPE // 2, "mla_q")
            groups = [(g * (MLA_NOPE + MLA_V), MLA_NOPE) for g in range(hh)]
            gain = jnp.tile(jnp.concatenate([mla_k_gain[j][:MLA_NOPE], ones(MLA_V)]), hh)
            kv = proj(dn, ckv0 // MLA_KV_LORA, mla_wukv[j].astype(BF16), groups, gain, rope, [], MLA_ROPE // 2, "mla_kv")
            o = mla_attention(q, kv, dn, kpe0 // LANES, n_ctx)
            w_o = mla_wo[j]
        elif kind == 2:
            hd = WIN_HEAD_DIM
            scale = hd ** -0.5 * LOG2E
            n_pairs = WIN_HEADS // WIN_KV_HEADS
            nq = WIN_HEADS * hd
            perm = np.array([(kv * n_pairs + g) * hd + t for g in range(n_pairs)
                             for kv in range(WIN_KV_HEADS) for t in range(hd)])
            wq = win_wqkv[j]
            w2_ = jnp.concatenate([wq[:, perm], wq[:, nq:]], axis=1)
            groups = [(g * hd, hd) for g in range(WIN_HEADS + WIN_KV_HEADS)]
            gain = jnp.concatenate([jnp.tile(win_q_gain[j] * scale, WIN_HEADS),
                                    jnp.tile(win_k_gain[j], WIN_KV_HEADS), ones(WIN_KV_HEADS * hd)])
            qkv = proj(xa, 0, w2_.astype(BF16), groups, gain, _rope_tables(seq, n_ctx, hd),
                       range(n_pairs + 1), hd // 2, "win_qkv", prenorm)
            o = win_attention(qkv, win_sink[j] * LOG2E, n_ctx)
            w_o = win_wo[j][perm, :]
        else:
            hd = DIFF_HEAD_DIM
            scale = hd ** -0.5 * LOG2E
            lam_init = 0.8 - 0.6 * math.exp(-0.3 * i)
            nqk = 2 * DIFF_HEADS
            groups = [(g * hd, hd) for g in range(2 * nqk)]
            gain = jnp.concatenate([jnp.tile(diff_q_gain[j] * scale, nqk), jnp.tile(diff_k_gain[j], nqk),
                                    ones(DIFF_HEADS * 2 * hd)])
            qkv = proj(xa, 0, diff_wqkv[j].astype(BF16), groups, gain, _rope_tables(seq, n_ctx, hd),
                       range(2 * DIFF_HEADS), hd // 2, "diff_qkv", prenorm)
            o = diff_attention(qkv, diff_lambda[j], diff_subln[j], lam_init, n_ctx)
            w_o = diff_wo[j]
        xa = out_proj_residual(o, w_o.astype(BF16), xa, mods, 2, n_ctx)

        jj = i // 2
        if i % 2 == 0:
            xa = ffn_residual(norm_ffn[i], ffn_w13[jj].astype(BF16), ffn_w2[jj].astype(BF16), xa, mods, 3, 4, 5,
                              n_ctx)
        else:
            h, route = norm_mod_router(xa, norm_ffn[i], mods, 3, 4, n_ctx, moe_router[jj])
            xa = moe_residual(h, route, moe_w13[jj].astype(BF16), moe_w2[jj].astype(BF16), xa, mods, 5,
                              n_ctx, need_ctx)
    return xa[:, n_ctx:]
```
